```python
import math
import jax, jax.numpy as jnp
from jax import lax
import numpy as np

D_MODEL = 1024
BATCH = 8
SEQ = 8192
DEPTH = 2

N_MIXERS = 2
N_A_LAYERS = (DEPTH + 1) // 2
N_B_LAYERS = DEPTH // 2

DIL_CONFIGS = ((128, 1), (512, 4), (2048, 16))
N_GROUPS = len(DIL_CONFIGS)
A_HEADS = 8
A_HEAD_DIM = 128
A_WIDTH = A_HEADS * A_HEAD_DIM
A_QKV_COLS = 3 * N_GROUPS * A_WIDTH

B_HEADS = 8
Q_LORA = 256
KV_LORA = 128
NOPE_DIM = 128
ROPE_DIM = 64
V_DIM = 128
QK_DIM = NOPE_DIM + ROPE_DIM
B_IN_COLS = Q_LORA + KV_LORA + ROPE_DIM

D_FF = 4 * D_MODEL

ROPE_THETA = 10000.0
EPS = 1e-6
Q_BLOCK = 128
NEG_FILL = -1e30

kernel_name = "hybrid_dilated_mla_encoder"


def rms_norm(x, gain):
    xf = x.astype(jnp.float32)
    y = xf * lax.rsqrt(jnp.mean(xf * xf, axis=-1, keepdims=True) + EPS)
    return (y * gain.astype(jnp.float32)).astype(x.dtype)


def rope(x, seq_len):
    d = x.shape[-1]
    pos = jnp.arange(seq_len, dtype=jnp.float32)
    freqs = ROPE_THETA ** (-jnp.arange(0, d, 2, dtype=jnp.float32) / d)
    ang = pos[:, None] * freqs[None, :]
    cos = jnp.concatenate([jnp.cos(ang), jnp.cos(ang)], -1)[:, None, :]
    sin = jnp.concatenate([jnp.sin(ang), jnp.sin(ang)], -1)[:, None, :]
    xf = x.astype(jnp.float32)
    x1, x2 = xf[..., : d // 2], xf[..., d // 2:]
    rot = jnp.concatenate([-x2, x1], -1)
    return (xf * cos + rot * sin).astype(x.dtype)


def dilated_group(q, k, v, window, dilation):
    B, S, H, dh = q.shape
    half = window // (2 * dilation)
    span = dilation * half
    s_pad = -(-S // span) * span
    L = s_pad // dilation
    nb = L // half
    scale = 1.0 / math.sqrt(dh)

    def strided(t, extra):
        t = jnp.pad(t, ((0, 0), (0, s_pad - S), (0, 0), (0, 0))).reshape(B, L, dilation, H, dh)
        return jnp.pad(t, ((0, 0), (extra, extra), (0, 0), (0, 0), (0, 0)))

    qs = strided(q, 0).reshape(B, nb, half, dilation, H, dh).transpose(1, 0, 2, 3, 4, 5)
    ks = strided(k, half)
    vs = strided(v, half)
    qi = jnp.arange(half)[:, None]
    kj = jnp.arange(3 * half)[None, :]
    res = jnp.arange(dilation)[:, None, None]
    band = jnp.abs(qi + half - kj) <= half

    def block(args):
        n, qb = args
        kb = lax.dynamic_slice_in_dim(ks, n * half, 3 * half, axis=1)
        vb = lax.dynamic_slice_in_dim(vs, n * half, 3 * half, axis=1)
        m_k = (n - 1) * half + kj
        pos_k = m_k[None] * dilation + res
        valid = band[None] & (m_k[None] >= 0) & (pos_k < S)
        s = jnp.einsum('bqchd,bkchd->bchqk', qb, kb).astype(jnp.float32) * scale
        s = jnp.where(valid[None, :, None], s, NEG_FILL)
        mx = jnp.max(s, axis=-1, keepdims=True)
        p = jnp.exp(s - mx)
        den = jnp.sum(p, axis=-1)
        o = jnp.einsum('bchqk,bkchd->bqchd', p.astype(vb.dtype), vb).astype(jnp.float32)
        o = o / den.transpose(0, 3, 1, 2)[..., None]
        lse = (mx[..., 0] + jnp.log(den)).transpose(0, 3, 1, 2)
        return o, lse

    o, lse = lax.map(block, (jnp.arange(nb), qs))
    o = o.transpose(1, 0, 2, 3, 4, 5).reshape(B, s_pad, H, dh)[:, :S]
    lse = lse.transpose(1, 0, 2, 3, 4).reshape(B, s_pad, H)[:, :S]
    return o, lse


def dilated_mixer(h, w_qkv, q_gain, k_gain, w_o):
    B, S, _ = h.shape
    qkv = (h @ w_qkv).reshape(B, S, 3, N_GROUPS, A_HEADS, A_HEAD_DIM)
    outs, lses = [], []
    for g, (window, dil) in enumerate(DIL_CONFIGS):
        q = rope(rms_norm(qkv[:, :, 0, g], q_gain[g]), S)
        k = rope(rms_norm(qkv[:, :, 1, g], k_gain[g]), S)
        v = qkv[:, :, 2, g]
        o, lse = dilated_group(q, k, v, window, dil)
        outs.append(o)
        lses.append(lse)
    wts = jax.nn.softmax(jnp.stack(lses, 0), axis=0)
    o = wts[0][..., None] * outs[0]
    for g in range(1, N_GROUPS):
        o = o + wts[g][..., None] * outs[g]
    return o.reshape(B, S, A_WIDTH).astype(h.dtype) @ w_o


def mla_mixer(h, w_in, q_a_gain, w_qb, kv_a_gain, w_kvb, q_gain, k_gain, w_o):
    B, S, _ = h.shape
    lat = h @ w_in
    c_q = lat[..., :Q_LORA]
    c_kv = lat[..., Q_LORA:Q_LORA + KV_LORA]
    k_rope = lat[..., Q_LORA + KV_LORA:]
    q = (rms_norm(c_q, q_a_gain) @ w_qb).reshape(B, S, B_HEADS, QK_DIM)
    kv = (rms_norm(c_kv, kv_a_gain) @ w_kvb).reshape(B, S, B_HEADS, NOPE_DIM + V_DIM)
    k_nope, v = kv[..., :NOPE_DIM], kv[..., NOPE_DIM:]
    k = jnp.concatenate(
        [k_nope, jnp.broadcast_to(k_rope[:, :, None, :], (B, S, B_HEADS, ROPE_DIM))], -1)
    q = rms_norm(q, q_gain)
    k = rms_norm(k, k_gain)
    q = jnp.concatenate([q[..., :NOPE_DIM], rope(q[..., NOPE_DIM:], S)], -1)
    k = jnp.concatenate([k[..., :NOPE_DIM], rope(k[..., NOPE_DIM:], S)], -1)
    scale = 1.0 / math.sqrt(QK_DIM)
    nblk = S // Q_BLOCK
    qb = q.reshape(B, nblk, Q_BLOCK, B_HEADS, QK_DIM).transpose(1, 0, 2, 3, 4)

    def attend(qblk):
        s = jnp.einsum('bqhd,bkhd->bhqk', qblk, k).astype(jnp.float32) * scale
        p = jax.nn.softmax(s, axis=-1)
        return jnp.einsum('bhqk,bkhd->bqhd', p.astype(v.dtype), v)

    o = lax.map(attend, qb)
    o = o.transpose(1, 0, 2, 3, 4).reshape(B, S, B_HEADS * V_DIM)
    return o @ w_o


def sq_relu_mlp(h, w1, w2):
    a = jax.nn.relu(h @ w1)
    return (a * a) @ w2


def setup_inputs(seed: int = 0) -> dict:
    key = jax.random.key(seed)
    ks = jax.random.split(key, 16)

    def w(k, shape, fan_in):
        return jax.random.normal(k, shape, jnp.float32) * (fan_in ** -0.5)

    def gain(k, shape):
        return 1.0 + 0.05 * jax.random.normal(k, shape, jnp.float32)

    return {
        "x": jax.random.normal(ks[0], (BATCH, SEQ, D_MODEL), jnp.float32),
        "norm_mix": gain(ks[1], (DEPTH, D_MODEL)),
        "norm_ffn": gain(ks[2], (DEPTH, D_MODEL)),
        "a_w_qkv": w(ks[3], (N_A_LAYERS, D_MODEL, A_QKV_COLS), D_MODEL),
        "a_q_gain": gain(ks[4], (N_A_LAYERS, N_GROUPS, A_HEAD_DIM)),
        "a_k_gain": gain(ks[5], (N_A_LAYERS, N_GROUPS, A_HEAD_DIM)),
        "a_w_o": w(ks[6], (N_A_LAYERS, A_WIDTH, D_MODEL), A_WIDTH),
        "b_w_in": w(ks[7], (N_B_LAYERS, D_MODEL, B_IN_COLS), D_MODEL),
        "b_q_a_gain": gain(ks[8], (N_B_LAYERS, Q_LORA)),
        "b_w_qb": w(ks[9], (N_B_LAYERS, Q_LORA, B_HEADS * QK_DIM), Q_LORA),
        "b_kv_a_gain": gain(ks[10], (N_B_LAYERS, KV_LORA)),
        "b_w_kvb": w(ks[11], (N_B_LAYERS, KV_LORA, B_HEADS * (NOPE_DIM + V_DIM)), KV_LORA),
        "b_q_gain": gain(ks[12], (N_B_LAYERS, QK_DIM)),
        "b_k_gain": gain(ks[13], (N_B_LAYERS, QK_DIM)),
        "b_w_o": w(ks[14], (N_B_LAYERS, B_HEADS * V_DIM, D_MODEL), B_HEADS * V_DIM),
        "ffn_w1": w(jax.random.fold_in(ks[15], 0), (DEPTH, D_MODEL, D_FF), D_MODEL),
        "ffn_w2": w(jax.random.fold_in(ks[15], 1), (DEPTH, D_FF, D_MODEL), D_FF),
    }


def reference(x, norm_mix, norm_ffn, a_w_qkv, a_q_gain, a_k_gain, a_w_o,
              b_w_in, b_q_a_gain, b_w_qb, b_kv_a_gain, b_w_kvb, b_q_gain, b_k_gain, b_w_o,
              ffn_w1, ffn_w2):
    for i in range(DEPTH):
        h = rms_norm(x, norm_mix[i])
        j = i // N_MIXERS
        if i % N_MIXERS == 0:
            y = dilated_mixer(h, a_w_qkv[j], a_q_gain[j], a_k_gain[j], a_w_o[j])
        else:
            y = mla_mixer(h, b_w_in[j], b_q_a_gain[j], b_w_qb[j], b_kv_a_gain[j],
                          b_w_kvb[j], b_q_gain[j], b_k_gain[j], b_w_o[j])
        x = x + y
        h = rms_norm(x, norm_ffn[i])
        x = x + sq_relu_mlp(h, ffn_w1[i], ffn_w2[i])
    return x
```

```python
import functools
import math

import jax
import jax.numpy as jnp
from jax import lax
from jax.experimental import pallas as pl
from jax.experimental.pallas import tpu as pltpu

EPS = 1e-6
ROPE_THETA = 10000.0
NEG_FILL = -1e30

DIL_CONFIGS = ((128, 1), (512, 4), (2048, 16))
N_GROUPS = len(DIL_CONFIGS)
A_HEADS = 8
A_HEAD_DIM = 128
A_WIDTH = A_HEADS * A_HEAD_DIM

B_HEADS = 8
Q_LORA = 256
KV_LORA = 128
NOPE_DIM = 128
ROPE_DIM = 64
V_DIM = 128
QK_DIM = NOPE_DIM + ROPE_DIM
QK_PAD = 256

LANES = 128
VMEM_LIMIT = 48 * 1024 * 1024

BF16 = jnp.bfloat16
F32 = jnp.float32


def _params(sem):
    return pltpu.CompilerParams(dimension_semantics=sem, vmem_limit_bytes=VMEM_LIMIT)


def _resident(shape):
    zeros = (0,) * len(shape)
    return pl.BlockSpec(shape, lambda *_: zeros, pipeline_mode=pl.Buffered(1))


def _rms_scale(x, width):
    return lax.rsqrt(jnp.sum(x * x, axis=-1, keepdims=True) * (1.0 / width) + EPS)


def _a_proj_kernel(x_ref, nrm_ref, w_ref, gain_ref, cos_ref, sin_ref, o_ref, h_ref):
    j = pl.program_id(1)

    @pl.when(j == 0)
    def _():
        x = x_ref[...]
        h_ref[...] = (x * _rms_scale(x, x.shape[-1]) * nrm_ref[...]).astype(BF16)

    @pl.when(j < 2 * N_GROUPS)
    def _():
        g = gain_ref[...]
        cos = cos_ref[...]
        sin = sin_ref[...]
        for p in range(A_HEADS // 2):
            y2 = jnp.dot(h_ref[...], w_ref[:, p * 2 * LANES:(p + 1) * 2 * LANES],
                         preferred_element_type=F32)
            for q in range(2):
                hh = 2 * p + q
                y = y2[:, q * LANES:(q + 1) * LANES]
                yn = y * _rms_scale(y, A_HEAD_DIM) * g
                out = yn * cos + pltpu.roll(yn, A_HEAD_DIM // 2, 1) * sin
                o_ref[:, hh * LANES:(hh + 1) * LANES] = out.astype(BF16)

    @pl.when(j >= 2 * N_GROUPS)
    def _():
        o_ref[...] = jnp.dot(h_ref[...], w_ref[...], preferred_element_type=F32).astype(BF16)


def _a_proj(x2d, nrm, w_qkv, gains, cos, sin, seq, tm):
    T, D = x2d.shape
    nblk = w_qkv.shape[1] // A_WIDTH
    spt = seq // tm
    return pl.pallas_call(
        _a_proj_kernel,
        grid=(T // tm, nblk),
        in_specs=[
            pl.BlockSpec((tm, D), lambda i, j: (i, 0)),
            pl.BlockSpec((1, D), lambda i, j: (0, 0)),
            pl.BlockSpec((D, A_WIDTH), lambda i, j: (0, j)),
            pl.BlockSpec((None, 1, LANES), lambda i, j: (jnp.minimum(j, 2 * N_GROUPS - 1), 0, 0)),
            pl.BlockSpec((tm, LANES), lambda i, j: (i % spt, 0)),
            pl.BlockSpec((tm, LANES), lambda i, j: (i % spt, 0)),
        ],
        out_specs=pl.BlockSpec((None, tm, A_WIDTH), lambda i, j: (j, i, 0)),
        out_shape=jax.ShapeDtypeStruct((nblk, T, A_WIDTH), BF16),
        scratch_shapes=[pltpu.VMEM((tm, D), BF16)],
        compiler_params=_params(("parallel", "arbitrary")),
        name="a_proj",
    )(x2d, nrm, w_qkv, gains, cos, sin)


def _dil_attn_kernel(q_ref, k_ref, kp_ref, kn_ref, v_ref, vp_ref, vn_ref,
                     o_ref, lse_ref, kx_ref, vx_ref, *, length, chunk, sub, half):
    i = pl.program_id(2)
    kx_ref[0:half] = kp_ref[...]
    kx_ref[half:half + chunk] = k_ref[...]
    kx_ref[half + chunk:] = kn_ref[...]
    vx_ref[0:half] = vp_ref[...]
    vx_ref[half:half + chunk] = v_ref[...]
    vx_ref[half + chunk:] = vn_ref[...]

    nk = sub + 2 * half
    qi = lax.broadcasted_iota(jnp.int32, (sub, nk), 0)
    kj = lax.broadcasted_iota(jnp.int32, (sub, nk), 1)
    band = jnp.abs(qi + half - kj) <= half
    lane = lax.broadcasted_iota(jnp.int32, (sub, LANES), 1)

    def body(j, carry):
        r0 = pl.multiple_of(j * sub, sub)
        base = i * chunk + j * sub
        lo = half - base
        hi = length + half - base
        mask = band & (kj >= lo) & (kj < hi)
        lse_tile = jnp.zeros((sub, LANES), F32)
        for h in range(A_HEADS):
            cols = slice(h * LANES, (h + 1) * LANES)
            q = q_ref[pl.ds(r0, sub), cols]
            k = kx_ref[pl.ds(r0, nk), cols]
            v = vx_ref[pl.ds(r0, nk), cols]
            s = lax.dot_general(q, k, (((1,), (1,)), ((), ())), preferred_element_type=F32)
            s = jnp.where(mask, s, NEG_FILL)
            m = jnp.max(s, axis=-1, keepdims=True)
            p = jnp.exp(s - m)
            den = jnp.sum(p, axis=-1, keepdims=True)
            o = jnp.dot(p.astype(BF16), v, preferred_element_type=F32) / den
            o_ref[pl.ds(r0, sub), cols] = o.astype(BF16)
            lse_tile = jnp.where(lane == h, m + jnp.log(den), lse_tile)
        lse_ref[pl.ds(r0, sub), :] = lse_tile
        return carry

    lax.fori_loop(0, chunk // sub, body, 0)


def _dil_attn(qkv, g, batch, seq, chunk=512, sub=128):
    window, dil = DIL_CONFIGS[g]
    half = window // (2 * dil)
    assert seq % (dil * half) == 0, "sequence must be a whole number of dilation spans"
    length = seq // dil
    chunk = min(chunk, length)
    assert length % chunk == 0 and chunk % sub == 0 and chunk % half == 0
    nblk = qkv.shape[0]
    x = qkv.reshape(nblk, batch, length, dil * A_WIDTH)
    cph = chunk // half
    last = length // half - 1

    def main(which):
        return pl.BlockSpec((None, None, chunk, A_WIDTH),
                            lambda b, c, i: (which * N_GROUPS + g, b, i, c))

    def prev(which):
        return pl.BlockSpec((None, None, half, A_WIDTH),
                            lambda b, c, i: (which * N_GROUPS + g, b, jnp.maximum(i * cph - 1, 0), c))

    def nxt(which):
        return pl.BlockSpec((None, None, half, A_WIDTH),
                            lambda b, c, i: (which * N_GROUPS + g, b, jnp.minimum((i + 1) * cph, last), c))

    kern = functools.partial(_dil_attn_kernel, length=length, chunk=chunk, sub=sub, half=half)
    o, lse = pl.pallas_call(
        kern,
        grid=(batch, dil, length // chunk),
        in_specs=[main(0), main(1), prev(1), nxt(1), main(2), prev(2), nxt(2)],
        out_specs=[
            pl.BlockSpec((None, chunk, A_WIDTH), lambda b, c, i: (b, i, c)),
            pl.BlockSpec((None, chunk, LANES), lambda b, c, i: (b, i, c)),
        ],
        out_shape=[
            jax.ShapeDtypeStruct((batch, length, dil * A_WIDTH), BF16),
            jax.ShapeDtypeStruct((batch, length, dil * LANES), F32),
        ],
        scratch_shapes=[pltpu.VMEM((chunk + 2 * half, A_WIDTH), BF16),
                        pltpu.VMEM((chunk + 2 * half, A_WIDTH), BF16)],
        compiler_params=_params(("parallel", "parallel", "parallel")),
        name=f"dil_attn_g{g}",
    )(x, x, x, x, x, x, x)
    return o.reshape(batch * seq, A_WIDTH), lse.reshape(batch * seq, LANES)


def _merge_kernel(o0_ref, o1_ref, o2_ref, l0_ref, l1_ref, l2_ref, e_ref, out_ref):
    l0, l1, l2 = l0_ref[...], l1_ref[...], l2_ref[...]
    mx = jnp.maximum(jnp.maximum(l0, l1), l2)
    e0, e1, e2 = jnp.exp(l0 - mx), jnp.exp(l1 - mx), jnp.exp(l2 - mx)
    inv = 1.0 / (e0 + e1 + e2)
    acc = None
    for e, o_ref in ((e0, o0_ref), (e1, o1_ref), (e2, o2_ref)):
        w = jnp.dot((e * inv).astype(BF16), e_ref[...], preferred_element_type=F32)
        term = w * o_ref[...].astype(F32)
        acc = term if acc is None else acc + term
    out_ref[...] = acc.astype(BF16)


def _merge(outs, lses, tm):
    T = outs[0].shape[0]
    expand = (jnp.arange(LANES)[:, None] == (jnp.arange(A_WIDTH)[None, :] // A_HEAD_DIM)).astype(BF16)
    o_spec = pl.BlockSpec((tm, A_WIDTH), lambda i: (i, 0))
    l_spec = pl.BlockSpec((tm, LANES), lambda i: (i, 0))
    return pl.pallas_call(
        _merge_kernel,
        grid=(T // tm,),
        in_specs=[o_spec] * 3 + [l_spec] * 3 + [_resident((LANES, A_WIDTH))],
        out_specs=o_spec,
        out_shape=jax.ShapeDtypeStruct((T, A_WIDTH), BF16),
        compiler_params=_params(("parallel",)),
        name="a_merge",
    )(*outs, *lses, expand)


def _ffn_kernel(x_ref, a_ref, wo_ref, nrm_ref, w1_ref, w2_ref, out_ref, *, ff_chunk):
    x1 = x_ref[...] + jnp.dot(a_ref[...], wo_ref[...], preferred_element_type=F32)
    h = (x1 * _rms_scale(x1, x1.shape[-1]) * nrm_ref[...]).astype(BF16)
    acc = x1
    for c in range(w1_ref.shape[1] // ff_chunk):
        cols = slice(c * ff_chunk, (c + 1) * ff_chunk)
        a = jnp.maximum(jnp.dot(h, w1_ref[:, cols], preferred_element_type=F32), 0.0)
        acc = acc + jnp.dot((a * a).astype(BF16), w2_ref[cols, :], preferred_element_type=F32)
    out_ref[...] = acc


def _ffn(x2d, a2d, w_o, nrm, w1, w2, tm, ff_chunk=1024):
    T, D = x2d.shape
    return pl.pallas_call(
        functools.partial(_ffn_kernel, ff_chunk=ff_chunk),
        grid=(T // tm,),
        in_specs=[
            pl.BlockSpec((tm, D), lambda i: (i, 0)),
            pl.BlockSpec((tm, a2d.shape[1]), lambda i: (i, 0)),
            _resident(w_o.shape),
            _resident(nrm.shape),
            _resident(w1.shape),
            _resident(w2.shape),
        ],
        out_specs=pl.BlockSpec((tm, D), lambda i: (i, 0)),
        out_shape=jax.ShapeDtypeStruct((T, D), F32),
        compiler_params=_params(("parallel",)),
        name="outproj_ffn",
    )(x2d, a2d, w_o, nrm, w1, w2)


def _b_proj_kernel(x_ref, nrm_ref, win_ref, qag_ref, kvag_ref, wqb_ref, wkn_ref, wv_ref,
                   gq_ref, gk_ref, cos_ref, sin_ref, q_ref, k_ref, v_ref):
    x = x_ref[...]
    h = (x * _rms_scale(x, x.shape[-1]) * nrm_ref[...]).astype(BF16)
    lat = jnp.dot(h, win_ref[...], preferred_element_type=F32)
    c_q = lat[:, :Q_LORA]
    c_kv = lat[:, Q_LORA:Q_LORA + KV_LORA]
    k_rope = lat[:, Q_LORA + KV_LORA:]
    cqn = (c_q * _rms_scale(c_q, Q_LORA) * qag_ref[...]).astype(BF16)
    ckvn = (c_kv * _rms_scale(c_kv, KV_LORA) * kvag_ref[...]).astype(BF16)
    cos = cos_ref[...]
    sin = sin_ref[...]
    gq = gq_ref[...]
    gk = gk_ref[...]
    ss_kr = jnp.sum(k_rope * k_rope, axis=-1, keepdims=True)
    v_ref[...] = jnp.dot(ckvn, wv_ref[...], preferred_element_type=F32).astype(BF16)

    def rope(t):
        return t * cos + pltpu.roll(t, LANES // 2, 1) * sin

    for hh in range(B_HEADS):
        q = jnp.dot(cqn, wqb_ref[:, hh * QK_PAD:(hh + 1) * QK_PAD], preferred_element_type=F32)
        qn, qr = q[:, :LANES], q[:, LANES:]
        rq = lax.rsqrt(jnp.sum(qn * qn + qr * qr, axis=-1, keepdims=True) * (1.0 / QK_DIM) + EPS)
        q_ref[:, hh * QK_PAD:hh * QK_PAD + LANES] = (qn * rq * gq[:, :LANES]).astype(BF16)
        q_ref[:, hh * QK_PAD + LANES:(hh + 1) * QK_PAD] = rope(qr * rq * gq[:, LANES:]).astype(BF16)

    kn_all = jnp.dot(ckvn, wkn_ref[...], preferred_element_type=F32)
    for hh in range(B_HEADS):
        kn = kn_all[:, hh * LANES:(hh + 1) * LANES]
        rk = lax.rsqrt((jnp.sum(kn * kn, axis=-1, keepdims=True) + ss_kr) * (1.0 / QK_DIM) + EPS)
        k_ref[:, hh * QK_PAD:hh * QK_PAD + LANES] = (kn * rk * gk[:, :LANES]).astype(BF16)
        k_ref[:, hh * QK_PAD + LANES:(hh + 1) * QK_PAD] = rope(k_rope * rk * gk[:, LANES:]).astype(BF16)


def _b_proj(x2d, nrm, w_in, qag, kvag, w_qb, w_kn, w_v, gq, gk, cos, sin, seq, tm):
    T, D = x2d.shape
    spt = seq // tm
    row = lambda i: (i, 0)
    return pl.pallas_call(
        _b_proj_kernel,
        grid=(T // tm,),
        in_specs=[
            pl.BlockSpec((tm, D), row),
            _resident(nrm.shape), _resident(w_in.shape), _resident(qag.shape), _resident(kvag.shape),
            _resident(w_qb.shape), _resident(w_kn.shape), _resident(w_v.shape),
            _resident(gq.shape), _resident(gk.shape),
            pl.BlockSpec((tm, LANES), lambda i: (i % spt, 0)),
            pl.BlockSpec((tm, LANES), lambda i: (i % spt, 0)),
        ],
        out_specs=[
            pl.BlockSpec((tm, B_HEADS * QK_PAD), row),
            pl.BlockSpec((tm, B_HEADS * QK_PAD), row),
            pl.BlockSpec((tm, B_HEADS * V_DIM), row),
        ],
        out_shape=[
            jax.ShapeDtypeStruct((T, B_HEADS * QK_PAD), BF16),
            jax.ShapeDtypeStruct((T, B_HEADS * QK_PAD), BF16),
            jax.ShapeDtypeStruct((T, B_HEADS * V_DIM), BF16),
        ],
        compiler_params=_params(("parallel",)),
        name="b_proj",
    )(x2d, nrm, w_in, qag, kvag, w_qb, w_kn, w_v, gq, gk, cos, sin)


def _mla_attn_kernel(q_ref, k_ref, v_ref, o_ref, *, tk):
    q = q_ref[...]
    tq = q.shape[0]
    nkv = k_ref.shape[0] // tk

    def body(kb, carry):
        m, l, acc = carry
        r0 = pl.multiple_of(kb * tk, tk)
        k = k_ref[pl.ds(r0, tk), :]
        v = v_ref[pl.ds(r0, tk), :]
        s = lax.dot_general(q, k, (((1,), (1,)), ((), ())), preferred_element_type=F32)
        m_new = jnp.maximum(m, jnp.max(s, axis=-1, keepdims=True))
        alpha = jnp.exp(m - m_new)
        p = jnp.exp(s - m_new)
        l = alpha * l + jnp.sum(p, axis=-1, keepdims=True)
        acc = alpha * acc + jnp.dot(p.astype(BF16), v, preferred_element_type=F32)
        return m_new, l, acc

    init = (jnp.full((tq, 1), -jnp.inf, F32), jnp.zeros((tq, 1), F32), jnp.zeros((tq, V_DIM), F32))
    m, l, acc = lax.fori_loop(0, nkv, body, init)
    o_ref[...] = (acc / l).astype(BF16)


def _mla_attn(q, k, v, batch, seq, tq=512, tk=512):
    nq = seq // tq
    return pl.pallas_call(
        functools.partial(_mla_attn_kernel, tk=tk),
        grid=(batch, B_HEADS, nq),
        in_specs=[
            pl.BlockSpec((tq, QK_PAD), lambda b, h, i: (b * nq + i, h)),
            pl.BlockSpec((seq, QK_PAD), lambda b, h, i: (b, h)),
            pl.BlockSpec((seq, V_DIM), lambda b, h, i: (b, h)),
        ],
        out_specs=pl.BlockSpec((tq, V_DIM), lambda b, h, i: (b * nq + i, h)),
        out_shape=jax.ShapeDtypeStruct((batch * seq, B_HEADS * V_DIM), BF16),
        compiler_params=_params(("parallel", "parallel", "parallel")),
        name="mla_attn",
    )(q, k, v)


def _rope_angles(seq, d):
    pos = jnp.arange(seq, dtype=F32)
    freqs = ROPE_THETA ** (-jnp.arange(0, d, 2, dtype=F32) / d)
    ang = pos[:, None] * freqs[None, :]
    return jnp.cos(ang), jnp.sin(ang)


def _spread_rope(t, axis):
    a, b = jnp.split(t, 2, axis=axis)
    z = jnp.zeros_like(a)
    return jnp.concatenate([a, z, b, z], axis=axis)


def kernel(x, norm_mix, norm_ffn, a_w_qkv, a_q_gain, a_k_gain, a_w_o, b_w_in, b_q_a_gain, b_w_qb,
           b_kv_a_gain, b_w_kvb, b_q_gain, b_k_gain, b_w_o, ffn_w1, ffn_w2):
    batch, seq, d_model = x.shape
    T = batch * seq
    x2d = x.reshape(T, d_model)

    cos, sin = _rope_angles(seq, A_HEAD_DIM)
    cos_a = jnp.concatenate([cos, cos], -1)
    sin_a = jnp.concatenate([-sin, sin], -1)
    gains_a = jnp.concatenate([a_q_gain[0] * (1.0 / math.sqrt(A_HEAD_DIM)), a_k_gain[0]], 0)[:, None, :]
    qkv = _a_proj(x2d, norm_mix[0][None, :], a_w_qkv[0].astype(BF16), gains_a, cos_a, sin_a, seq, tm=1024)
    outs, lses = [], []
    for g in range(N_GROUPS):
        o, lse = _dil_attn(qkv, g, batch, seq)
        outs.append(o)
        lses.append(lse)
    merged = _merge(outs, lses, tm=512)
    x2d = _ffn(x2d, merged, a_w_o[0].astype(BF16), norm_ffn[0][None, :],
               ffn_w1[0].astype(BF16), ffn_w2[0].astype(BF16), tm=512)

    cos, sin = _rope_angles(seq, ROPE_DIM)
    cos_b = _spread_rope(jnp.concatenate([cos, cos], -1), -1)
    sin_b = _spread_rope(jnp.concatenate([-sin, sin], -1), -1)
    w_in = b_w_in[0]
    w_in = jnp.concatenate([w_in[:, :Q_LORA + KV_LORA], _spread_rope(w_in[:, Q_LORA + KV_LORA:], 1)], 1)
    w_qb = b_w_qb[0].reshape(Q_LORA, B_HEADS, QK_DIM)
    w_qb = jnp.concatenate([w_qb[..., :NOPE_DIM], _spread_rope(w_qb[..., NOPE_DIM:], 2)], -1)
    w_qb = w_qb.reshape(Q_LORA, B_HEADS * QK_PAD)
    w_kvb = b_w_kvb[0].reshape(KV_LORA, B_HEADS, NOPE_DIM + V_DIM)
    w_kn = w_kvb[..., :NOPE_DIM].reshape(KV_LORA, B_HEADS * NOPE_DIM)
    w_v = w_kvb[..., NOPE_DIM:].reshape(KV_LORA, B_HEADS * V_DIM)

    def head_gain(gain, scale):
        return (jnp.concatenate([gain[:NOPE_DIM], _spread_rope(gain[NOPE_DIM:], 0)]) * scale)[None, :]

    q, k, v = _b_proj(
        x2d, norm_mix[1][None, :], w_in.astype(BF16), b_q_a_gain[0][None, :], b_kv_a_gain[0][None, :],
        w_qb.astype(BF16), w_kn.astype(BF16), w_v.astype(BF16),
        head_gain(b_q_gain[0], 1.0 / math.sqrt(QK_DIM)), head_gain(b_k_gain[0], 1.0),
        cos_b, sin_b, seq, tm=512)
    o = _mla_attn(q, k, v, batch, seq)
    x2d = _ffn(x2d, o, b_w_o[0].astype(BF16), norm_ffn[1][None, :],
               ffn_w1[1].astype(BF16), ffn_w2[1].astype(BF16), tm=512)
    return x2d.reshape(batch, seq, d_model)
```

```python
import functools
import math

import jax
import jax.numpy as jnp
from jax import lax
from jax.experimental import pallas as pl
from jax.experimental.pallas import tpu as pltpu

EPS = 1e-6
ROPE_THETA = 10000.0
NEG_FILL = -1e30
LOG2E = math.log2(math.e)

DIL_CONFIGS = ((128, 1), (512, 4), (2048, 16))
N_GROUPS = len(DIL_CONFIGS)
A_HEADS = 8
A_HEAD_DIM = 128
A_WIDTH = A_HEADS * A_HEAD_DIM

B_HEADS = 8
Q_LORA = 256
KV_LORA = 128
NOPE_DIM = 128
ROPE_DIM = 64
V_DIM = 128
QK_DIM = NOPE_DIM + ROPE_DIM
QK_PAD = 256
V_PAD = 256
CONST_LANE = 32
L_MIN = 2.0 ** -80

LANES = 128
VMEM_LIMIT = 48 * 1024 * 1024

BF16 = jnp.bfloat16
F32 = jnp.float32
NT_DIMS = (((1,), (1,)), ((), ()))


def _params(sem):
    return pltpu.CompilerParams(dimension_semantics=sem, vmem_limit_bytes=VMEM_LIMIT)


def _resident(shape):
    zeros = (0,) * len(shape)
    return pl.BlockSpec(shape, lambda *_: zeros, pipeline_mode=pl.Buffered(1))


def _rms_scale(x, width):
    return lax.rsqrt(jnp.sum(x * x, axis=-1, keepdims=True) * (1.0 / width) + EPS)


def _a_proj_kernel(x_ref, nrm_ref, w_ref, gain_ref, cos_ref, sin_ref, ones_ref, o_ref, h_ref, hs_ref, *, dil):
    j = pl.program_id(1)
    tm = x_ref.shape[0]
    rows = tm // dil

    @pl.when(j == 0)
    def _():
        x = x_ref[...]
        hn = x * _rms_scale(x, x.shape[-1]) * nrm_ref[...]
        if dil == 1:
            h_ref[...] = hn.astype(BF16)
        else:
            for kb in range(hs_ref.shape[0]):
                hs_ref[kb] = hn[:, kb * LANES:(kb + 1) * LANES]
            for c in range(dil):
                for kb in range(hs_ref.shape[0]):
                    h_ref[c * rows:(c + 1) * rows, kb * LANES:(kb + 1) * LANES] = (
                        hs_ref[kb, pl.ds(c, rows, stride=dil), :].astype(BF16))

    def store(col0, val):
        for c in range(dil):
            o_ref[c, :, col0:col0 + LANES] = val[c * rows:(c + 1) * rows].astype(BF16)

    @pl.when(j < 2)
    def _():
        ga, gb = gain_ref[0:1, :], gain_ref[1:2, :]
        cos = cos_ref[...]
        sin = sin_ref[...]
        for p in range(A_HEADS // 2):
            y2 = jnp.dot(h_ref[...], w_ref[:, p * 2 * LANES:(p + 1) * 2 * LANES], preferred_element_type=F32)
            a, b = y2[:, :LANES], y2[:, LANES:]
            ss = jnp.dot((a * a + b * b).astype(BF16), ones_ref[...], preferred_element_type=F32)
            rinv = lax.rsqrt(ss * (1.0 / A_HEAD_DIM) + EPS)
            an = a * rinv * ga
            bn = b * rinv * gb
            store(p * 2 * LANES, an * cos - bn * sin)
            store(p * 2 * LANES + LANES, bn * cos + an * sin)

    @pl.when(j == 2)
    def _():
        for p in range(A_HEADS // 2):
            y2 = jnp.dot(h_ref[...], w_ref[:, p * 2 * LANES:(p + 1) * 2 * LANES], preferred_element_type=F32)
            store(p * 2 * LANES, y2[:, :LANES])
            store(p * 2 * LANES + LANES, y2[:, LANES:])


def _a_proj(x2d, nrm, w3, gains, cos, sin, ones, batch, seq, dil, tm):
    T, D = x2d.shape
    spt = seq // tm
    rows = tm // dil
    return pl.pallas_call(
        functools.partial(_a_proj_kernel, dil=dil),
        grid=(T // tm, 3),
        in_specs=[
            pl.BlockSpec((tm, D), lambda i, j: (i, 0)),
            pl.BlockSpec((1, D), lambda i, j: (0, 0)),
            pl.BlockSpec((None, D, A_WIDTH), lambda i, j: (j, 0, 0)),
            pl.BlockSpec((None, 2, LANES), lambda i, j: (jnp.minimum(j, 1), 0, 0)),
            pl.BlockSpec((tm, LANES), lambda i, j: (i % spt, 0)),
            pl.BlockSpec((tm, LANES), lambda i, j: (i % spt, 0)),
            pl.BlockSpec((LANES, LANES), lambda i, j: (0, 0)),
        ],
        out_specs=pl.BlockSpec((None, None, dil, rows, A_WIDTH), lambda i, j: (j, i // spt, 0, i % spt, 0)),
        out_shape=jax.ShapeDtypeStruct((3, batch, dil, seq // dil, A_WIDTH), BF16),
        scratch_shapes=[pltpu.VMEM((tm, D), BF16), pltpu.VMEM((D // LANES, tm, LANES), F32)],
        compiler_params=_params(("parallel", "arbitrary")),
        name=f"a_proj_d{dil}",
    )(x2d, nrm, w3, gains, cos, sin, ones)


def _dil_attn_kernel(q_ref, k_ref, kp_ref, kn_ref, v_ref, vp_ref, vn_ref,
                     o_ref, lse_ref, kx_ref, vx_ref, *, length, chunk, sub, half):
    i = pl.program_id(2)
    kx_ref[0:half] = kp_ref[...]
    kx_ref[half:half + chunk] = k_ref[...]
    kx_ref[half + chunk:] = kn_ref[...]
    vx_ref[0:half] = vp_ref[...]
    vx_ref[half:half + chunk] = v_ref[...]
    vx_ref[half + chunk:] = vn_ref[...]

    nk = sub + 2 * half
    qi = lax.broadcasted_iota(jnp.int32, (2 * sub, nk), 0) % sub
    kj = lax.broadcasted_iota(jnp.int32, (2 * sub, nk), 1)
    band = jnp.abs(qi + half - kj) <= half
    lane = lax.broadcasted_iota(jnp.int32, (sub, LANES), 1)
    lane2 = lax.broadcasted_iota(jnp.int32, (1, 2 * LANES), 1) % LANES
    first = (lane2 < LANES // 2).astype(BF16)
    second = (lane2 >= LANES // 2).astype(BF16)

    def body(j, carry):
        r0 = pl.multiple_of(j * sub, sub)
        base = i * chunk + j * sub
        lo = half - base
        hi = length + half - base
        mask = band & (kj >= lo) & (kj < hi)
        lse_tile = jnp.zeros((sub, LANES), F32)
        for p in range(A_HEADS // 2):
            cols = slice(p * 2 * LANES, (p + 1) * 2 * LANES)
            q2 = q_ref[pl.ds(r0, sub), cols]
            qq = jnp.concatenate([q2 * first, q2 * second], axis=0)
            s = lax.dot_general(qq, kx_ref[pl.ds(r0, nk), cols], NT_DIMS, preferred_element_type=F32)
            s = jnp.where(mask, s, NEG_FILL)
            m = jnp.max(s, axis=-1, keepdims=True)
            e = jnp.exp(s - m)
            den = jnp.sum(e, axis=-1, keepdims=True)
            eb = e.astype(BF16)
            lse = m + jnp.log(den)
            for t in range(2):
                h = 2 * p + t
                hc = slice(h * LANES, (h + 1) * LANES)
                rs = slice(t * sub, (t + 1) * sub)
                o = jnp.dot(eb[rs], vx_ref[pl.ds(r0, nk), hc], preferred_element_type=F32) / den[rs]
                o_ref[pl.ds(r0, sub), hc] = o.astype(BF16)
                lse_tile = jnp.where(lane == h, lse[rs], lse_tile)
        lse_ref[pl.ds(r0, sub), :] = lse_tile
        return carry

    lax.fori_loop(0, chunk // sub, body, 0)


def _dil_attn(qkv, g, chunk=512, sub=128):
    window, dil = DIL_CONFIGS[g]
    half = window // (2 * dil)
    _, batch, _, length, _ = qkv.shape
    chunk = min(chunk, length)
    assert length % chunk == 0 and chunk % sub == 0 and chunk % half == 0
    cph = chunk // half
    last = length // half - 1

    def main(which):
        return pl.BlockSpec((None, None, None, chunk, A_WIDTH), lambda b, c, i: (which, b, c, i, 0))

    def prev(which):
        return pl.BlockSpec((None, None, None, half, A_WIDTH),
                            lambda b, c, i: (which, b, c, jnp.maximum(i * cph - 1, 0), 0))

    def nxt(which):
        return pl.BlockSpec((None, None, None, half, A_WIDTH),
                            lambda b, c, i: (which, b, c, jnp.minimum((i + 1) * cph, last), 0))

    kern = functools.partial(_dil_attn_kernel, length=length, chunk=chunk, sub=sub, half=half)
    return pl.pallas_call(
        kern,
        grid=(batch, dil, length // chunk),
        in_specs=[main(0), main(1), prev(1), nxt(1), main(2), prev(2), nxt(2)],
        out_specs=[
            pl.BlockSpec((None, None, chunk, A_WIDTH), lambda b, c, i: (b, c, i, 0)),
            pl.BlockSpec((None, None, chunk, LANES), lambda b, c, i: (b, c, i, 0)),
        ],
        out_shape=[
            jax.ShapeDtypeStruct((batch, dil, length, A_WIDTH), BF16),
            jax.ShapeDtypeStruct((batch, dil, length, LANES), F32),
        ],
        scratch_shapes=[pltpu.VMEM((chunk + 2 * half, A_WIDTH), BF16),
                        pltpu.VMEM((chunk + 2 * half, A_WIDTH), BF16)],
        compiler_params=_params(("parallel", "parallel", "parallel")),
        name=f"dil_attn_g{g}",
    )(qkv, qkv, qkv, qkv, qkv, qkv, qkv)


def _merge_kernel(o0_ref, o1_ref, o2_ref, l0_ref, l1_ref, l2_ref, e_ref, out_ref, os_ref, ls_ref):
    o_refs = (o0_ref, o1_ref, o2_ref)
    l_refs = (l0_ref, l1_ref, l2_ref)
    nlb = os_ref.shape[1]
    for g in range(N_GROUPS):
        dil, rows = o_refs[g].shape[0], o_refs[g].shape[1]
        for c in range(dil):
            oc = o_refs[g][c].astype(F32)
            for kb in range(nlb):
                os_ref[g, kb, pl.ds(c, rows, stride=dil), :] = oc[:, kb * LANES:(kb + 1) * LANES]
            ls_ref[g, pl.ds(c, rows, stride=dil), :] = l_refs[g][c]
    lse = [ls_ref[g] for g in range(N_GROUPS)]
    mx = jnp.maximum(jnp.maximum(lse[0], lse[1]), lse[2])
    e = [jnp.exp(l - mx) for l in lse]
    inv = 1.0 / (e[0] + e[1] + e[2])
    acc = None
    for g in range(N_GROUPS):
        w = jnp.dot((e[g] * inv).astype(BF16), e_ref[...], preferred_element_type=F32)
        term = w * jnp.concatenate([os_ref[g, kb] for kb in range(nlb)], axis=-1)
        acc = term if acc is None else acc + term
    out_ref[...] = acc.astype(BF16)


def _merge(outs, lses, seq, tm):
    batch = outs[0].shape[0]
    spt = seq // tm
    expand = (jnp.arange(LANES)[:, None] == (jnp.arange(A_WIDTH)[None, :] // A_HEAD_DIM)).astype(BF16)

    def spec(arr):
        dil, width = arr.shape[1], arr.shape[3]
        return pl.BlockSpec((None, dil, tm // dil, width), lambda i: (i // spt, 0, i % spt, 0))

    return pl.pallas_call(
        _merge_kernel,
        grid=(batch * spt,),
        in_specs=[spec(o) for o in outs] + [spec(l) for l in lses] + [_resident((LANES, A_WIDTH))],
        out_specs=pl.BlockSpec((tm, A_WIDTH), lambda i: (i, 0)),
        out_shape=jax.ShapeDtypeStruct((batch * seq, A_WIDTH), BF16),
        scratch_shapes=[pltpu.VMEM((N_GROUPS, A_WIDTH // LANES, tm, LANES), F32),
                        pltpu.VMEM((N_GROUPS, tm, LANES), F32)],
        compiler_params=_params(("parallel",)),
        name="a_merge",
    )(*outs, *lses, expand)


def _ffn_kernel(x_ref, a_ref, wo_ref, nrm_ref, w1_ref, w2_ref, out_ref, *, ff_chunk):
    x1 = x_ref[...] + jnp.dot(a_ref[...], wo_ref[...], preferred_element_type=F32)
    h = (x1 * _rms_scale(x1, x1.shape[-1]) * nrm_ref[...]).astype(BF16)
    acc = x1
    for c in range(w1_ref.shape[1] // ff_chunk):
        cols = slice(c * ff_chunk, (c + 1) * ff_chunk)
        a = jnp.maximum(jnp.dot(h, w1_ref[:, cols], preferred_element_type=F32), 0.0)
        acc = acc + jnp.dot((a * a).astype(BF16), w2_ref[cols, :], preferred_element_type=F32)
    out_ref[...] = acc


def _ffn(x2d, a2d, w_o, nrm, w1, w2, tm, ff_chunk=1024):
    T, D = x2d.shape
    return pl.pallas_call(
        functools.partial(_ffn_kernel, ff_chunk=ff_chunk),
        grid=(T // tm,),
        in_specs=[
            pl.BlockSpec((tm, D), lambda i: (i, 0)),
            pl.BlockSpec((tm, a2d.shape[1]), lambda i: (i, 0)),
            _resident(w_o.shape),
            _resident(nrm.shape),
            _resident(w1.shape),
            _resident(w2.shape),
        ],
        out_specs=pl.BlockSpec((tm, D), lambda i: (i, 0)),
        out_shape=jax.ShapeDtypeStruct((T, D), F32),
        compiler_params=_params(("parallel",)),
        name="outproj_ffn",
    )(x2d, a2d, w_o, nrm, w1, w2)


def _b_proj_kernel(x_ref, nrm_ref, win_ref, qag_ref, kvag_ref, wqb_ref, wkn_ref, wv_ref,
                   gq_ref, gk_ref, cq_ref, ck_ref, cos_ref, sin_ref, q_ref, k_ref, v_ref):
    x = x_ref[...]
    h = (x * _rms_scale(x, x.shape[-1]) * nrm_ref[...]).astype(BF16)
    lat = jnp.dot(h, win_ref[...], preferred_element_type=F32)
    c_q = lat[:, :Q_LORA]
    c_kv = lat[:, Q_LORA:Q_LORA + KV_LORA]
    k_rope = lat[:, Q_LORA + KV_LORA:]
    cqn = (c_q * _rms_scale(c_q, Q_LORA) * qag_ref[...]).astype(BF16)
    ckvn = (c_kv * _rms_scale(c_kv, KV_LORA) * kvag_ref[...]).astype(BF16)
    cos = cos_ref[...]
    sin = sin_ref[...]
    gq = gq_ref[...]
    gk = gk_ref[...]
    ss_kr = jnp.sum(k_rope * k_rope, axis=-1, keepdims=True)

    ones_col = (lax.broadcasted_iota(jnp.int32, (x.shape[0], V_PAD - V_DIM), 1) == 0).astype(BF16)
    v_all = jnp.dot(ckvn, wv_ref[...], preferred_element_type=F32)
    for hh in range(B_HEADS):
        v_ref[:, hh * V_PAD:hh * V_PAD + V_DIM] = v_all[:, hh * V_DIM:(hh + 1) * V_DIM].astype(BF16)
        v_ref[:, hh * V_PAD + V_DIM:(hh + 1) * V_PAD] = ones_col

    def rope(t):
        return t * cos + pltpu.roll(t, LANES // 2, 1) * sin

    for hh in range(B_HEADS):
        q = jnp.dot(cqn, wqb_ref[:, hh * QK_PAD:(hh + 1) * QK_PAD], preferred_element_type=F32)
        qn, qr = q[:, :LANES], q[:, LANES:]
        rq = lax.rsqrt(jnp.sum(qn * qn + qr * qr, axis=-1, keepdims=True) * (1.0 / QK_DIM) + EPS)
        q_ref[:, hh * QK_PAD:hh * QK_PAD + LANES] = (qn * rq * gq[:, :LANES]).astype(BF16)
        q_ref[:, hh * QK_PAD + LANES:(hh + 1) * QK_PAD] = (rope(qr * rq * gq[:, LANES:]) + cq_ref[...]).astype(BF16)

    kn_all = jnp.dot(ckvn, wkn_ref[...], preferred_element_type=F32)
    for hh in range(B_HEADS):
        kn = kn_all[:, hh * LANES:(hh + 1) * LANES]
        rk = lax.rsqrt((jnp.sum(kn * kn, axis=-1, keepdims=True) + ss_kr) * (1.0 / QK_DIM) + EPS)
        k_ref[:, hh * QK_PAD:hh * QK_PAD + LANES] = (kn * rk * gk[:, :LANES]).astype(BF16)
        k_ref[:, hh * QK_PAD + LANES:(hh + 1) * QK_PAD] = (rope(k_rope * rk * gk[:, LANES:]) + ck_ref[...]).astype(BF16)


def _b_proj(x2d, nrm, w_in, qag, kvag, w_qb, w_kn, w_v, gq, gk, cq, ck, cos, sin, seq, tm):
    T, D = x2d.shape
    spt = seq // tm
    row = lambda i: (i, 0)
    consts = (nrm, w_in, qag, kvag, w_qb, w_kn, w_v, gq, gk, cq, ck)
    return pl.pallas_call(
        _b_proj_kernel,
        grid=(T // tm,),
        in_specs=[pl.BlockSpec((tm, D), row)] + [_resident(c.shape) for c in consts] + [
            pl.BlockSpec((tm, LANES), lambda i: (i % spt, 0)),
            pl.BlockSpec((tm, LANES), lambda i: (i % spt, 0)),
        ],
        out_specs=[
            pl.BlockSpec((tm, B_HEADS * QK_PAD), row),
            pl.BlockSpec((tm, B_HEADS * QK_PAD), row),
            pl.BlockSpec((tm, B_HEADS * V_PAD), row),
        ],
        out_shape=[
            jax.ShapeDtypeStruct((T, B_HEADS * QK_PAD), BF16),
            jax.ShapeDtypeStruct((T, B_HEADS * QK_PAD), BF16),
            jax.ShapeDtypeStruct((T, B_HEADS * V_PAD), BF16),
        ],
        compiler_params=_params(("parallel",)),
        name="b_proj",
    )(x2d, *consts, cos, sin)


def _mla_attn_kernel(q_ref, k_ref, v_ref, o_ref, acc_ref, *, tk, unroll):
    q = q_ref[...]
    tq = q.shape[0]
    nkv = k_ref.shape[0] // tk

    def scores(kb):
        r0 = pl.multiple_of(kb * tk, tk)
        s = lax.dot_general(q, k_ref[pl.ds(r0, tk), :], NT_DIMS, preferred_element_type=F32)
        return s, v_ref[pl.ds(r0, tk), :]

    def fast(kb, carry):
        for u in range(unroll):
            s, v = scores(kb * unroll + u)
            acc_ref[...] += jnp.dot(jnp.exp2(s).astype(BF16), v, preferred_element_type=F32)
        return carry

    acc_ref[...] = jnp.zeros_like(acc_ref)
    lax.fori_loop(0, nkv // unroll, fast, 0)
    den = acc_ref[:, V_DIM:V_DIM + 1]
    trusted = jnp.min(den) >= L_MIN

    @pl.when(trusted)
    def _():
        o_ref[...] = (acc_ref[:, :V_DIM] / den).astype(BF16)

    @pl.when(jnp.logical_not(trusted))
    def _():
        def slow(kb, m):
            s, v = scores(kb)
            m_new = jnp.maximum(m, jnp.max(s, axis=-1, keepdims=True))
            p = jnp.exp2(s - m_new).astype(BF16)
            acc_ref[...] = jnp.exp2(m - m_new) * acc_ref[...] + jnp.dot(p, v, preferred_element_type=F32)
            return m_new

        acc_ref[...] = jnp.zeros_like(acc_ref)
        lax.fori_loop(0, nkv, slow, jnp.full((tq, 1), -jnp.inf, F32))
        o_ref[...] = (acc_ref[:, :V_DIM] / acc_ref[:, V_DIM:V_DIM + 1]).astype(BF16)


def _mla_attn(q, k, v, batch, seq, tq=1024, tk=512, unroll=16):
    nq = seq // tq
    unroll = min(unroll, seq // tk)
    assert seq % (tk * unroll) == 0
    return pl.pallas_call(
        functools.partial(_mla_attn_kernel, tk=tk, unroll=unroll),
        grid=(batch, B_HEADS, nq),
        in_specs=[
            pl.BlockSpec((tq, QK_PAD), lambda b, h, i: (b * nq + i, h)),
            pl.BlockSpec((seq, QK_PAD), lambda b, h, i: (b, h)),
            pl.BlockSpec((seq, V_PAD), lambda b, h, i: (b, h)),
        ],
        out_specs=pl.BlockSpec((tq, V_DIM), lambda b, h, i: (b * nq + i, h)),
        out_shape=jax.ShapeDtypeStruct((batch * seq, B_HEADS * V_DIM), BF16),
        scratch_shapes=[pltpu.VMEM((tq, V_PAD), F32)],
        compiler_params=_params(("parallel", "parallel", "parallel")),
        name="mla_attn",
    )(q, k, v)


def _rope_angles(seq, d):
    pos = jnp.arange(seq, dtype=F32)
    freqs = ROPE_THETA ** (-jnp.arange(0, d, 2, dtype=F32) / d)
    ang = pos[:, None] * freqs[None, :]
    return jnp.cos(ang), jnp.sin(ang)


def _spread_rope(t, axis):
    a, b = jnp.split(t, 2, axis=axis)
    z = jnp.zeros_like(a)
    return jnp.concatenate([a, z, b, z], axis=axis)


def _pair_columns(w):
    d = w.shape[0]
    w = w.reshape(d, A_HEADS // 2, 2, 2, A_HEAD_DIM // 2)
    return w.transpose(0, 1, 3, 2, 4).reshape(d, A_WIDTH)


def _residue_major(table, tm, dil):
    s, w = table.shape
    return table.reshape(s // tm, tm // dil, dil, w).transpose(0, 2, 1, 3).reshape(s, w)


def kernel(x, norm_mix, norm_ffn, a_w_qkv, a_q_gain, a_k_gain, a_w_o, b_w_in, b_q_a_gain, b_w_qb,
           b_kv_a_gain, b_w_kvb, b_q_gain, b_k_gain, b_w_o, ffn_w1, ffn_w2):
    batch, seq, d_model = x.shape
    T = batch * seq
    x2d = x.reshape(T, d_model)

    tm_a = 1024
    cos, sin = _rope_angles(seq, A_HEAD_DIM)
    cos_a = jnp.concatenate([cos, cos], -1)
    sin_a = jnp.concatenate([sin, sin], -1)
    half_ones = ((jnp.arange(LANES)[:, None] < LANES // 2) == (jnp.arange(LANES)[None, :] < LANES // 2)).astype(BF16)
    w_qkv = a_w_qkv[0].reshape(d_model, 3, N_GROUPS, A_WIDTH)
    hd = A_HEAD_DIM // 2
    outs, lses = [], []
    for g, (_, dil) in enumerate(DIL_CONFIGS):
        w3 = jnp.stack([_pair_columns(w_qkv[:, 0, g]), _pair_columns(w_qkv[:, 1, g]), w_qkv[:, 2, g]]).astype(BF16)
        gq = a_q_gain[0, g] * (1.0 / math.sqrt(A_HEAD_DIM))
        gk = a_k_gain[0, g]
        gains = jnp.stack([
            jnp.stack([jnp.tile(gq[:hd], 2), jnp.tile(gq[hd:], 2)]),
            jnp.stack([jnp.tile(gk[:hd], 2), jnp.tile(gk[hd:], 2)]),
        ])
        qkv = _a_proj(x2d, norm_mix[0][None, :], w3, gains, _residue_major(cos_a, tm_a, dil),
                      _residue_major(sin_a, tm_a, dil), half_ones, batch, seq, dil, tm_a)
        o, lse = _dil_attn(qkv, g)
        outs.append(o)
        lses.append(lse)
    merged = _merge(outs, lses, seq, tm=512)
    x2d = _ffn(x2d, merged, a_w_o[0].astype(BF16), norm_ffn[0][None, :],
               ffn_w1[0].astype(BF16), ffn_w2[0].astype(BF16), tm=512)

    cos, sin = _rope_angles(seq, ROPE_DIM)
    cos_b = _spread_rope(jnp.concatenate([cos, cos], -1), -1)
    sin_b = _spread_rope(jnp.concatenate([-sin, sin], -1), -1)
    w_in = b_w_in[0]
    w_in = jnp.concatenate([w_in[:, :Q_LORA + KV_LORA], _spread_rope(w_in[:, Q_LORA + KV_LORA:], 1)], 1)
    w_qb = b_w_qb[0].reshape(Q_LORA, B_HEADS, QK_DIM)
    w_qb = jnp.concatenate([w_qb[..., :NOPE_DIM], _spread_rope(w_qb[..., NOPE_DIM:], 2)], -1)
    w_qb = w_qb.reshape(Q_LORA, B_HEADS * QK_PAD)
    w_kvb = b_w_kvb[0].reshape(KV_LORA, B_HEADS, NOPE_DIM + V_DIM)
    w_kn = w_kvb[..., :NOPE_DIM].reshape(KV_LORA, B_HEADS * NOPE_DIM)
    w_v = w_kvb[..., NOPE_DIM:].reshape(KV_LORA, B_HEADS * V_DIM)

    def head_gain(gain, scale):
        return (jnp.concatenate([gain[:NOPE_DIM], _spread_rope(gain[NOPE_DIM:], 0)]) * scale)[None, :]

    q_scale = LOG2E / math.sqrt(QK_DIM)
    bound = 1.02 * QK_DIM * q_scale * jnp.max(jnp.abs(b_q_gain[0])) * jnp.max(jnp.abs(b_k_gain[0]))
    const_lane = (jnp.arange(LANES) == CONST_LANE).astype(F32)[None, :]
    q, k, v = _b_proj(
        x2d, norm_mix[1][None, :], w_in.astype(BF16), b_q_a_gain[0][None, :], b_kv_a_gain[0][None, :],
        w_qb.astype(BF16), w_kn.astype(BF16), w_v.astype(BF16),
        head_gain(b_q_gain[0], q_scale), head_gain(b_k_gain[0], 1.0),
        -bound * const_lane, const_lane, cos_b, sin_b, seq, tm=512)
    o = _mla_attn(q, k, v, batch, seq)
    x2d = _ffn(x2d, o, b_w_o[0].astype(BF16), norm_ffn[1][None, :],
               ffn_w1[1].astype(BF16), ffn_w2[1].astype(BF16), tm=512)
    return x2d.reshape(batch, seq, d_model)
```

```python
import functools
import math

import jax
import jax.numpy as jnp
from jax import lax
from jax.experimental import pallas as pl
from jax.experimental.pallas import tpu as pltpu

EPS = 1e-6
ROPE_THETA = 10000.0
NEG_FILL = -1e30
LOG2E = math.log2(math.e)

DIL_CONFIGS = ((128, 1), (512, 4), (2048, 16))
N_GROUPS = len(DIL_CONFIGS)
A_HEADS = 8
A_HEAD_DIM = 128
A_WIDTH = A_HEADS * A_HEAD_DIM

B_HEADS = 8
Q_LORA = 256
KV_LORA = 128
NOPE_DIM = 128
ROPE_DIM = 64
V_DIM = 128
QK_DIM = NOPE_DIM + ROPE_DIM
QK_PAD = 256
Q3_PAD = 384
VT_ROWS = 144
CONST_LANE = 32
L_MIN = 2.0 ** -80

LANES = 128
VMEM_LIMIT = 48 * 1024 * 1024

BF16 = jnp.bfloat16
F32 = jnp.float32
NT_DIMS = (((1,), (1,)), ((), ()))


def _params(sem):
    return pltpu.CompilerParams(dimension_semantics=sem, vmem_limit_bytes=VMEM_LIMIT)


def _resident(shape):
    zeros = (0,) * len(shape)
    return pl.BlockSpec(shape, lambda *_: zeros, pipeline_mode=pl.Buffered(1))


def _rms_scale(x, width):
    return lax.rsqrt(jnp.sum(x * x, axis=-1, keepdims=True) * (1.0 / width) + EPS)


def _a_proj_kernel(x_ref, nrm_ref, w_ref, tab_ref, ones_ref, o_ref, h_ref, hs_ref, *, dil):
    j = pl.program_id(1)
    tm = x_ref.shape[0]
    rows = tm // dil

    @pl.when(j == 0)
    def _():
        x = x_ref[...]
        hn = x * _rms_scale(x, x.shape[-1]) * nrm_ref[...]
        if dil == 1:
            h_ref[...] = hn.astype(BF16)
        else:
            for kb in range(hs_ref.shape[0]):
                hs_ref[kb] = hn[:, kb * LANES:(kb + 1) * LANES]
            for c in range(dil):
                for kb in range(hs_ref.shape[0]):
                    h_ref[c * rows:(c + 1) * rows, kb * LANES:(kb + 1) * LANES] = (
                        hs_ref[kb, pl.ds(c, rows, stride=dil), :].astype(BF16))

    def store(col0, val):
        for c in range(dil):
            o_ref[c, :, col0:col0 + LANES] = val[c * rows:(c + 1) * rows].astype(BF16)

    y = jnp.dot(h_ref[...], w_ref[...], preferred_element_type=F32)

    @pl.when(j < 2)
    def _():
        ca, sb, cb, sa = tab_ref[0], tab_ref[1], tab_ref[2], tab_ref[3]
        for p in range(A_HEADS // 2):
            y2 = y[:, p * 2 * LANES:(p + 1) * 2 * LANES]
            ms = jnp.dot((y2 * y2).astype(BF16), ones_ref[...], preferred_element_type=F32)[:, :LANES]
            rinv = lax.rsqrt(ms + EPS)
            ar = y2[:, :LANES] * rinv
            br = y2[:, LANES:] * rinv
            store(p * 2 * LANES, ar * ca - br * sb)
            store(p * 2 * LANES + LANES, br * cb + ar * sa)

    @pl.when(j == 2)
    def _():
        for hh in range(A_HEADS):
            store(hh * LANES, y[:, hh * LANES:(hh + 1) * LANES])


def _a_proj(x2d, nrm, w3, tables, ones, batch, seq, dil, tm):
    T, D = x2d.shape
    spt = seq // tm
    rows = tm // dil
    return pl.pallas_call(
        functools.partial(_a_proj_kernel, dil=dil),
        grid=(T // tm, 3),
        in_specs=[
            pl.BlockSpec((tm, D), lambda i, j: (i, 0)),
            pl.BlockSpec((1, D), lambda i, j: (0, 0)),
            pl.BlockSpec((None, D, A_WIDTH), lambda i, j: (j, 0, 0)),
            pl.BlockSpec((None, 4, tm, LANES), lambda i, j: (jnp.minimum(j, 1), 0, i % spt, 0)),
            pl.BlockSpec((2 * LANES, 2 * LANES), lambda i, j: (0, 0)),
        ],
        out_specs=pl.BlockSpec((None, None, dil, rows, A_WIDTH), lambda i, j: (j, i // spt, 0, i % spt, 0)),
        out_shape=jax.ShapeDtypeStruct((3, batch, dil, seq // dil, A_WIDTH), BF16),
        scratch_shapes=[pltpu.VMEM((tm, D), BF16), pltpu.VMEM((D // LANES, tm, LANES), F32)],
        compiler_params=_params(("parallel", "arbitrary")),
        name=f"a_proj_d{dil}",
    )(x2d, nrm, w3, tables, ones)


def _dil_attn_kernel(q_ref, k_ref, kp_ref, kn_ref, v_ref, vp_ref, vn_ref,
                     o_ref, lse_ref, kx_ref, vx_ref, *, length, chunk, sub, half):
    i = pl.program_id(2)
    kx_ref[0:half] = kp_ref[...]
    kx_ref[half:half + chunk] = k_ref[...]
    kx_ref[half + chunk:] = kn_ref[...]
    vx_ref[0:half] = vp_ref[...]
    vx_ref[half:half + chunk] = v_ref[...]
    vx_ref[half + chunk:] = vn_ref[...]

    nk = sub + 2 * half
    qi = lax.broadcasted_iota(jnp.int32, (2 * sub, nk), 0) % sub
    kj = lax.broadcasted_iota(jnp.int32, (2 * sub, nk), 1)
    band = jnp.abs(qi + half - kj) <= half
    lane = lax.broadcasted_iota(jnp.int32, (sub, LANES), 1)
    lane2 = lax.broadcasted_iota(jnp.int32, (1, 2 * LANES), 1) % LANES
    first = (lane2 < LANES // 2).astype(BF16)
    second = (lane2 >= LANES // 2).astype(BF16)

    def body(j, carry):
        r0 = j * sub
        base = i * chunk + j * sub
        lo = half - base
        hi = length + half - base
        mask = band & (kj >= lo) & (kj < hi)
        lse_tile = jnp.zeros((sub, LANES), F32)
        for p in range(A_HEADS // 2):
            cols = slice(p * 2 * LANES, (p + 1) * 2 * LANES)
            q2 = q_ref[pl.ds(r0, sub), cols]
            qq = jnp.concatenate([q2 * first, q2 * second], axis=0)
            s = lax.dot_general(qq, kx_ref[pl.ds(r0, nk), cols], NT_DIMS, preferred_element_type=F32)
            s = jnp.where(mask, s, NEG_FILL)
            m = jnp.max(s, axis=-1, keepdims=True)
            e = jnp.exp(s - m)
            den = jnp.sum(e, axis=-1, keepdims=True)
            eb = e.astype(BF16)
            lse = m + jnp.log(den)
            for t in range(2):
                h = 2 * p + t
                hc = slice(h * LANES, (h + 1) * LANES)
                rs = slice(t * sub, (t + 1) * sub)
                o = jnp.dot(eb[rs], vx_ref[pl.ds(r0, nk), hc], preferred_element_type=F32) / den[rs]
                o_ref[pl.ds(r0, sub), hc] = o.astype(BF16)
                lse_tile = jnp.where(lane == h, lse[rs], lse_tile)
        lse_ref[pl.ds(r0, sub), :] = lse_tile
        return carry

    for j in range(chunk // sub):
        body(j, 0)


def _dil_attn(qkv, g, chunk=512, sub=128):
    window, dil = DIL_CONFIGS[g]
    half = window // (2 * dil)
    _, batch, _, length, _ = qkv.shape
    chunk = min(chunk, length)
    assert length % chunk == 0 and chunk % sub == 0 and chunk % half == 0
    cph = chunk // half
    last = length // half - 1

    def main(which):
        return pl.BlockSpec((None, None, None, chunk, A_WIDTH), lambda b, c, i: (which, b, c, i, 0))

    def prev(which):
        return pl.BlockSpec((None, None, None, half, A_WIDTH),
                            lambda b, c, i: (which, b, c, jnp.maximum(i * cph - 1, 0), 0))

    def nxt(which):
        return pl.BlockSpec((None, None, None, half, A_WIDTH),
                            lambda b, c, i: (which, b, c, jnp.minimum((i + 1) * cph, last), 0))

    kern = functools.partial(_dil_attn_kernel, length=length, chunk=chunk, sub=sub, half=half)
    return pl.pallas_call(
        kern,
        grid=(batch, dil, length // chunk),
        in_specs=[main(0), main(1), prev(1), nxt(1), main(2), prev(2), nxt(2)],
        out_specs=[
            pl.BlockSpec((None, None, chunk, A_WIDTH), lambda b, c, i: (b, c, i, 0)),
            pl.BlockSpec((None, None, chunk, LANES), lambda b, c, i: (b, c, i, 0)),
        ],
        out_shape=[
            jax.ShapeDtypeStruct((batch, dil, length, A_WIDTH), BF16),
            jax.ShapeDtypeStruct((batch, dil, length, LANES), F32),
        ],
        scratch_shapes=[pltpu.VMEM((chunk + 2 * half, A_WIDTH), BF16),
                        pltpu.VMEM((chunk + 2 * half, A_WIDTH), BF16)],
        compiler_params=_params(("parallel", "parallel", "parallel")),
        name=f"dil_attn_g{g}",
    )(qkv, qkv, qkv, qkv, qkv, qkv, qkv)


def _merge_kernel(o0_ref, o1_ref, o2_ref, l0_ref, l1_ref, l2_ref, e_ref, out_ref, os_ref, ls_ref):
    o_refs = (o0_ref, o1_ref, o2_ref)
    l_refs = (l0_ref, l1_ref, l2_ref)
    nlb = os_ref.shape[1]
    for g in range(N_GROUPS):
        dil, rows = o_refs[g].shape[0], o_refs[g].shape[1]
        for c in range(dil):
            oc = o_refs[g][c].astype(F32)
            for kb in range(nlb):
                os_ref[g, kb, pl.ds(c, rows, stride=dil), :] = oc[:, kb * LANES:(kb + 1) * LANES]
            ls_ref[g, pl.ds(c, rows, stride=dil), :] = l_refs[g][c]
    lse = [ls_ref[g] for g in range(N_GROUPS)]
    mx = jnp.maximum(jnp.maximum(lse[0], lse[1]), lse[2])
    e = [jnp.exp(l - mx) for l in lse]
    inv = 1.0 / (e[0] + e[1] + e[2])
    acc = None
    for g in range(N_GROUPS):
        w = jnp.dot((e[g] * inv).astype(BF16), e_ref[...], preferred_element_type=F32)
        term = w * jnp.concatenate([os_ref[g, kb] for kb in range(nlb)], axis=-1)
        acc = term if acc is None else acc + term
    out_ref[...] = acc.astype(BF16)


def _merge(outs, lses, seq, tm):
    batch = outs[0].shape[0]
    spt = seq // tm
    expand = (jnp.arange(LANES)[:, None] == (jnp.arange(A_WIDTH)[None, :] // A_HEAD_DIM)).astype(BF16)

    def spec(arr):
        dil, width = arr.shape[1], arr.shape[3]
        return pl.BlockSpec((None, dil, tm // dil, width), lambda i: (i // spt, 0, i % spt, 0))

    return pl.pallas_call(
        _merge_kernel,
        grid=(batch * spt,),
        in_specs=[spec(o) for o in outs] + [spec(l) for l in lses] + [_resident((LANES, A_WIDTH))],
        out_specs=pl.BlockSpec((tm, A_WIDTH), lambda i: (i, 0)),
        out_shape=jax.ShapeDtypeStruct((batch * seq, A_WIDTH), BF16),
        scratch_shapes=[pltpu.VMEM((N_GROUPS, A_WIDTH // LANES, tm, LANES), F32),
                        pltpu.VMEM((N_GROUPS, tm, LANES), F32)],
        compiler_params=_params(("parallel",)),
        name="a_merge",
    )(*outs, *lses, expand)


def _ffn_kernel(x_ref, a_ref, wo_ref, nrm_ref, w1_ref, w2_ref, out_ref, *, ff_chunk):
    x1 = x_ref[...] + jnp.dot(a_ref[...], wo_ref[...], preferred_element_type=F32)
    h = (x1 * _rms_scale(x1, x1.shape[-1]) * nrm_ref[...]).astype(BF16)
    acc = x1
    for c in range(w1_ref.shape[1] // ff_chunk):
        cols = slice(c * ff_chunk, (c + 1) * ff_chunk)
        a = jnp.maximum(jnp.dot(h, w1_ref[:, cols], preferred_element_type=F32), 0.0)
        acc = acc + jnp.dot((a * a).astype(BF16), w2_ref[cols, :], preferred_element_type=F32)
    out_ref[...] = acc


def _ffn(x2d, a2d, w_o, nrm, w1, w2, tm, ff_chunk=1024):
    T, D = x2d.shape
    return pl.pallas_call(
        functools.partial(_ffn_kernel, ff_chunk=ff_chunk),
        grid=(T // tm,),
        in_specs=[
            pl.BlockSpec((tm, D), lambda i: (i, 0)),
            pl.BlockSpec((tm, a2d.shape[1]), lambda i: (i, 0)),
            _resident(w_o.shape),
            _resident(nrm.shape),
            _resident(w1.shape),
            _resident(w2.shape),
        ],
        out_specs=pl.BlockSpec((tm, D), lambda i: (i, 0)),
        out_shape=jax.ShapeDtypeStruct((T, D), F32),
        compiler_params=_params(("parallel",)),
        name="outproj_ffn",
    )(x2d, a2d, w_o, nrm, w1, w2)


def _b_proj_kernel(x_ref, nrm_ref, win_ref, qag_ref, kvag_ref, wqb_ref, wkn_ref, wv_ref,
                   gq_ref, gk_ref, cq_ref, ck_ref, mq_ref, mk_ref, cosq_ref, cosk_ref, sin_ref,
                   q_ref, k_ref, v_ref):
    x = x_ref[...]
    h = (x * _rms_scale(x, x.shape[-1]) * nrm_ref[...]).astype(BF16)
    lat = jnp.dot(h, win_ref[...], preferred_element_type=F32)
    c_q = lat[:, :Q_LORA]
    c_kv = lat[:, Q_LORA:Q_LORA + KV_LORA]
    k_rope = lat[:, Q_LORA + KV_LORA:Q_LORA + KV_LORA + LANES]
    k_roll = lat[:, Q_LORA + KV_LORA + LANES:]
    cqn = (c_q * _rms_scale(c_q, Q_LORA) * qag_ref[...]).astype(BF16)
    ckvn = (c_kv * _rms_scale(c_kv, KV_LORA) * kvag_ref[...]).astype(BF16)
    sin = sin_ref[...]

    vt_all = lax.dot_general(wv_ref[...], ckvn, NT_DIMS, preferred_element_type=F32)
    ones_rows = (lax.broadcasted_iota(jnp.int32, (VT_ROWS - V_DIM, x.shape[0]), 0) == 0).astype(BF16)
    for hh in range(B_HEADS):
        v_ref[hh * VT_ROWS:hh * VT_ROWS + V_DIM, :] = vt_all[hh * V_DIM:(hh + 1) * V_DIM, :].astype(BF16)
        v_ref[hh * VT_ROWS + V_DIM:(hh + 1) * VT_ROWS, :] = ones_rows

    q3 = jnp.dot(cqn, wqb_ref[...], preferred_element_type=F32)
    gq_n = gq_ref[:, :LANES]
    cos_q = cosq_ref[...]
    for hh in range(B_HEADS):
        q2 = q3[:, hh * Q3_PAD:hh * Q3_PAD + QK_PAD]
        ms = jnp.dot((q2 * q2).astype(BF16), mq_ref[...], preferred_element_type=F32)[:, :LANES]
        rq = lax.rsqrt(ms * (1.0 / QK_DIM) + EPS)
        qr = q2[:, LANES:] * cos_q + q3[:, hh * Q3_PAD + QK_PAD:(hh + 1) * Q3_PAD] * sin
        q_ref[:, hh * QK_PAD:hh * QK_PAD + LANES] = (q2[:, :LANES] * rq * gq_n).astype(BF16)
        q_ref[:, hh * QK_PAD + LANES:(hh + 1) * QK_PAD] = (qr * rq + cq_ref[...]).astype(BF16)

    kn_all = jnp.dot(ckvn, wkn_ref[...], preferred_element_type=F32)
    ss_kr = jnp.sum(k_rope * k_rope, axis=-1, keepdims=True)
    kr = k_rope * cosk_ref[...] + k_roll * sin
    gk_n = gk_ref[:, :LANES]
    for p in range(B_HEADS // 2):
        kn2 = kn_all[:, p * 2 * LANES:(p + 1) * 2 * LANES]
        ms = jnp.dot((kn2 * kn2).astype(BF16), mk_ref[...], preferred_element_type=F32)
        rk2 = lax.rsqrt((ms + ss_kr) * (1.0 / QK_DIM) + EPS)
        for t in range(2):
            hh = 2 * p + t
            rk = rk2[:, t * LANES:(t + 1) * LANES]
            k_ref[:, hh * QK_PAD:hh * QK_PAD + LANES] = (kn2[:, t * LANES:(t + 1) * LANES] * rk * gk_n).astype(BF16)
            k_ref[:, hh * QK_PAD + LANES:(hh + 1) * QK_PAD] = (kr * rk + ck_ref[...]).astype(BF16)


def _b_proj(x2d, consts, tables, seq, tm):
    T, D = x2d.shape
    spt = seq // tm
    row = lambda i: (i, 0)
    pos = pl.BlockSpec((tm, LANES), lambda i: (i % spt, 0))
    return pl.pallas_call(
        _b_proj_kernel,
        grid=(T // tm,),
        in_specs=[pl.BlockSpec((tm, D), row)] + [_resident(c.shape) for c in consts] + [pos] * len(tables),
        out_specs=[
            pl.BlockSpec((tm, B_HEADS * QK_PAD), row),
            pl.BlockSpec((tm, B_HEADS * QK_PAD), row),
            pl.BlockSpec((B_HEADS * VT_ROWS, tm), lambda i: (0, i)),
        ],
        out_shape=[
            jax.ShapeDtypeStruct((T, B_HEADS * QK_PAD), BF16),
            jax.ShapeDtypeStruct((T, B_HEADS * QK_PAD), BF16),
            jax.ShapeDtypeStruct((B_HEADS * VT_ROWS, T), BF16),
        ],
        compiler_params=_params(("parallel",)),
        name="b_proj",
    )(x2d, *consts, *tables)


def _mla_attn_kernel(q_ref, k_ref, vt_ref, o_ref, acc_ref, *, tk, unroll):
    q = q_ref[...]
    tq = q.shape[0]
    nkv = k_ref.shape[0] // tk

    def scores_t(kb):
        r0 = pl.multiple_of(kb * tk, tk)
        st = lax.dot_general(k_ref[pl.ds(r0, tk), :], q, NT_DIMS, preferred_element_type=F32)
        return st, vt_ref[:, pl.ds(r0, tk)]

    def fast(kb, carry):
        for u in range(unroll):
            st, vt = scores_t(kb * unroll + u)
            acc_ref[...] += jnp.dot(vt, jnp.exp2(st).astype(BF16), preferred_element_type=F32)
        return carry

    def finish():
        inv = 1.0 / acc_ref[V_DIM:V_DIM + 1, :]
        o_ref[...] = (acc_ref[:V_DIM, :] * inv).T.astype(BF16)

    acc_ref[...] = jnp.zeros_like(acc_ref)
    lax.fori_loop(0, nkv // unroll, fast, 0)
    trusted = jnp.min(acc_ref[V_DIM:V_DIM + 1, :]) >= L_MIN

    @pl.when(trusted)
    def _():
        finish()

    @pl.when(jnp.logical_not(trusted))
    def _():
        def slow(kb, m):
            st, vt = scores_t(kb)
            m_new = jnp.maximum(m, jnp.max(st, axis=0, keepdims=True))
            pt = jnp.exp2(st - m_new).astype(BF16)
            acc_ref[...] = jnp.exp2(m - m_new) * acc_ref[...] + jnp.dot(vt, pt, preferred_element_type=F32)
            return m_new

        acc_ref[...] = jnp.zeros_like(acc_ref)
        lax.fori_loop(0, nkv, slow, jnp.full((1, tq), -jnp.inf, F32))
        finish()


def _mla_attn(q, k, vt, batch, seq, tq=1024, tk=512, unroll=16):
    nq = seq // tq
    unroll = min(unroll, seq // tk)
    assert seq % (tk * unroll) == 0
    return pl.pallas_call(
        functools.partial(_mla_attn_kernel, tk=tk, unroll=unroll),
        grid=(batch, B_HEADS, nq),
        in_specs=[
            pl.BlockSpec((tq, QK_PAD), lambda b, h, i: (b * nq + i, h)),
            pl.BlockSpec((seq, QK_PAD), lambda b, h, i: (b, h)),
            pl.BlockSpec((VT_ROWS, seq), lambda b, h, i: (h, b)),
        ],
        out_specs=pl.BlockSpec((tq, V_DIM), lambda b, h, i: (b * nq + i, h)),
        out_shape=jax.ShapeDtypeStruct((batch * seq, B_HEADS * V_DIM), BF16),
        scratch_shapes=[pltpu.VMEM((VT_ROWS, tq), F32)],
        compiler_params=_params(("parallel", "parallel", "parallel")),
        name="mla_attn",
    )(q, k, vt)


def _rope_angles(seq, d):
    pos = jnp.arange(seq, dtype=F32)
    freqs = ROPE_THETA ** (-jnp.arange(0, d, 2, dtype=F32) / d)
    ang = pos[:, None] * freqs[None, :]
    return jnp.cos(ang), jnp.sin(ang)


def _spread_rope(t, axis):
    a, b = jnp.split(t, 2, axis=axis)
    z = jnp.zeros_like(a)
    return jnp.concatenate([a, z, b, z], axis=axis)


def _pair_columns(w):
    d = w.shape[0]
    w = w.reshape(d, A_HEADS // 2, 2, 2, A_HEAD_DIM // 2)
    return w.transpose(0, 1, 3, 2, 4).reshape(d, A_WIDTH)


def _residue_major(table, tm, dil):
    s, w = table.shape
    return table.reshape(s // tm, tm // dil, dil, w).transpose(0, 2, 1, 3).reshape(s, w)


def kernel(x, norm_mix, norm_ffn, a_w_qkv, a_q_gain, a_k_gain, a_w_o, b_w_in, b_q_a_gain, b_w_qb,
           b_kv_a_gain, b_w_kvb, b_q_gain, b_k_gain, b_w_o, ffn_w1, ffn_w2):
    batch, seq, d_model = x.shape
    T = batch * seq
    x2d = x.reshape(T, d_model)

    tm_a = 1024
    cos, sin = _rope_angles(seq, A_HEAD_DIM)
    cos_a = jnp.concatenate([cos, cos], -1)
    sin_a = jnp.concatenate([sin, sin], -1)
    pair_head = (jnp.arange(2 * LANES) % LANES) // (LANES // 2)
    mean_ones = (pair_head[:, None] == pair_head[None, :]).astype(BF16) * (1.0 / A_HEAD_DIM)
    w_qkv = a_w_qkv[0].reshape(d_model, 3, N_GROUPS, A_WIDTH)
    hd = A_HEAD_DIM // 2
    outs, lses = [], []
    for g, (_, dil) in enumerate(DIL_CONFIGS):
        w3 = jnp.stack([_pair_columns(w_qkv[:, 0, g]), _pair_columns(w_qkv[:, 1, g]), w_qkv[:, 2, g]]).astype(BF16)
        cos_g = _residue_major(cos_a, tm_a, dil)
        sin_g = _residue_major(sin_a, tm_a, dil)

        def rope_tables(gain):
            ga, gb = jnp.tile(gain[:hd], 2), jnp.tile(gain[hd:], 2)
            return jnp.stack([ga * cos_g, gb * sin_g, gb * cos_g, ga * sin_g])

        tables = jnp.stack([rope_tables(a_q_gain[0, g] * (1.0 / math.sqrt(A_HEAD_DIM))),
                            rope_tables(a_k_gain[0, g])])
        qkv = _a_proj(x2d, norm_mix[0][None, :], w3, tables, mean_ones, batch, seq, dil, tm_a)
        o, lse = _dil_attn(qkv, g)
        outs.append(o)
        lses.append(lse)
    merged = _merge(outs, lses, seq, tm=512)
    x2d = _ffn(x2d, merged, a_w_o[0].astype(BF16), norm_ffn[0][None, :],
               ffn_w1[0].astype(BF16), ffn_w2[0].astype(BF16), tm=512)

    cos, sin = _rope_angles(seq, ROPE_DIM)
    cos_b = _spread_rope(jnp.concatenate([cos, cos], -1), -1)
    sin_b = _spread_rope(jnp.concatenate([-sin, sin], -1), -1)

    def head_gain(gain, scale):
        return (jnp.concatenate([gain[:NOPE_DIM], _spread_rope(gain[NOPE_DIM:], 0)]) * scale)[None, :]

    q_scale = LOG2E / math.sqrt(QK_DIM)
    gq = head_gain(b_q_gain[0], q_scale)
    gk = head_gain(b_k_gain[0], 1.0)
    bound = 1.02 * QK_DIM * q_scale * jnp.max(jnp.abs(b_q_gain[0])) * jnp.max(jnp.abs(b_k_gain[0]))
    const_lane = (jnp.arange(LANES) == CONST_LANE).astype(F32)[None, :]

    def half_roll(w, gain):
        return jnp.roll(w * gain, LANES // 2, axis=-1)

    w_in = b_w_in[0]
    k_rope_w = _spread_rope(w_in[:, Q_LORA + KV_LORA:], 1)
    w_in = jnp.concatenate([w_in[:, :Q_LORA + KV_LORA], k_rope_w, half_roll(k_rope_w, gk[:, LANES:])], 1)
    w_qb = b_w_qb[0].reshape(Q_LORA, B_HEADS, QK_DIM)
    q_rope_w = _spread_rope(w_qb[..., NOPE_DIM:], 2)
    w_qb = jnp.concatenate([w_qb[..., :NOPE_DIM], q_rope_w, half_roll(q_rope_w, gq[:, LANES:])], -1)
    w_qb = w_qb.reshape(Q_LORA, B_HEADS * Q3_PAD)
    w_kvb = b_w_kvb[0].reshape(KV_LORA, B_HEADS, NOPE_DIM + V_DIM)
    w_kn = w_kvb[..., :NOPE_DIM].reshape(KV_LORA, B_HEADS * NOPE_DIM)
    w_v = w_kvb[..., NOPE_DIM:].reshape(KV_LORA, B_HEADS * V_DIM).T
    sum_q = jnp.ones((QK_PAD, QK_PAD), BF16)
    sum_k = ((jnp.arange(2 * LANES)[:, None] // LANES) == (jnp.arange(2 * LANES)[None, :] // LANES)).astype(BF16)

    consts = (norm_mix[1][None, :], w_in.astype(BF16), b_q_a_gain[0][None, :], b_kv_a_gain[0][None, :],
              w_qb.astype(BF16), w_kn.astype(BF16), w_v.astype(BF16), gq, gk,
              -bound * const_lane, const_lane, sum_q, sum_k)
    tables = (cos_b * gq[:, LANES:], cos_b * gk[:, LANES:], sin_b)
    q, k, v = _b_proj(x2d, consts, tables, seq, tm=512)
    o = _mla_attn(q, k, v, batch, seq)
    x2d = _ffn(x2d, o, b_w_o[0].astype(BF16), norm_ffn[1][None, :],
               ffn_w1[1].astype(BF16), ffn_w2[1].astype(BF16), tm=512)
    return x2d.reshape(batch, seq, d_model)
```

```python
import functools
import math

import jax
import jax.numpy as jnp
from jax import lax
from jax.experimental import pallas as pl
from jax.experimental.pallas import tpu as pltpu

EPS = 1e-6
ROPE_THETA = 10000.0
NEG_FILL = -1e30
LOG2E = math.log2(math.e)

DIL_CONFIGS = ((128, 1), (512, 4), (2048, 16))
N_GROUPS = len(DIL_CONFIGS)
A_HEADS = 8
A_HEAD_DIM = 128
A_WIDTH = A_HEADS * A_HEAD_DIM

B_HEADS = 8
Q_LORA = 256
KV_LORA = 128
NOPE_DIM = 128
ROPE_DIM = 64
V_DIM = 128
QK_DIM = NOPE_DIM + ROPE_DIM
QK_PAD = 256
Q3_PAD = 384
VT_ROWS = 144
CONST_LANE = 32
L_MIN = 2.0 ** -80

LANES = 128
VMEM_LIMIT = 48 * 1024 * 1024

BF16 = jnp.bfloat16
F32 = jnp.float32
NT_DIMS = (((1,), (1,)), ((), ()))


def _params(sem):
    return pltpu.CompilerParams(dimension_semantics=sem, vmem_limit_bytes=VMEM_LIMIT)


def _resident(shape):
    zeros = (0,) * len(shape)
    return pl.BlockSpec(shape, lambda *_: zeros, pipeline_mode=pl.Buffered(1))


def _rms_scale(x, width):
    return lax.rsqrt(jnp.sum(x * x, axis=-1, keepdims=True) * (1.0 / width) + EPS)


def _a_proj_kernel(x_ref, nrm_ref, w_ref, tab_ref, ones_ref, o_ref, h_ref, hs_ref, *, dil):
    tm = x_ref.shape[0]
    rows = tm // dil
    x = x_ref[...]
    hn = x * _rms_scale(x, x.shape[-1]) * nrm_ref[...]
    if dil == 1:
        h_ref[...] = hn.astype(BF16)
    else:
        for kb in range(hs_ref.shape[0]):
            hs_ref[kb] = hn[:, kb * LANES:(kb + 1) * LANES]
        for c in range(dil):
            for kb in range(hs_ref.shape[0]):
                h_ref[c * rows:(c + 1) * rows, kb * LANES:(kb + 1) * LANES] = (
                    hs_ref[kb, pl.ds(c, rows, stride=dil), :].astype(BF16))

    def store(which, col0, val):
        for c in range(dil):
            o_ref[which, c, :, col0:col0 + LANES] = val[c * rows:(c + 1) * rows].astype(BF16)

    for which in range(2):
        y = jnp.dot(h_ref[...], w_ref[which], preferred_element_type=F32)
        ca, sb, cb, sa = tab_ref[which, 0], tab_ref[which, 1], tab_ref[which, 2], tab_ref[which, 3]
        for p in range(A_HEADS // 2):
            y2 = y[:, p * 2 * LANES:(p + 1) * 2 * LANES]
            ms = jnp.dot((y2 * y2).astype(BF16), ones_ref[...], preferred_element_type=F32)[:, :LANES]
            rinv = lax.rsqrt(ms + EPS)
            ar = y2[:, :LANES] * rinv
            br = y2[:, LANES:] * rinv
            store(which, p * 2 * LANES, ar * ca - br * sb)
            store(which, p * 2 * LANES + LANES, br * cb + ar * sa)

    y = jnp.dot(h_ref[...], w_ref[2], preferred_element_type=F32)
    for hh in range(A_HEADS):
        store(2, hh * LANES, y[:, hh * LANES:(hh + 1) * LANES])


def _a_proj(x2d, nrm, w3, tables, ones, batch, seq, dil, tm):
    T, D = x2d.shape
    spt = seq // tm
    rows = tm // dil
    return pl.pallas_call(
        functools.partial(_a_proj_kernel, dil=dil),
        grid=(T // tm,),
        in_specs=[
            pl.BlockSpec((tm, D), lambda i: (i, 0)),
            _resident(nrm.shape),
            _resident(w3.shape),
            pl.BlockSpec((2, 4, tm, LANES), lambda i: (0, 0, i % spt, 0)),
            _resident(ones.shape),
        ],
        out_specs=pl.BlockSpec((3, None, dil, rows, A_WIDTH), lambda i: (0, i // spt, 0, i % spt, 0)),
        out_shape=jax.ShapeDtypeStruct((3, batch, dil, seq // dil, A_WIDTH), BF16),
        scratch_shapes=[pltpu.VMEM((tm, D), BF16), pltpu.VMEM((D // LANES, tm, LANES), F32)],
        compiler_params=_params(("parallel",)),
        name=f"a_proj_d{dil}",
    )(x2d, nrm, w3, tables, ones)


def _dil_attn_kernel(bound_ref, q_ref, k_ref, kp_ref, kn_ref, v_ref, vp_ref, vn_ref,
                     o_ref, lse_ref, kx_ref, vx_ref, *, length, chunk, sub, half):
    i = pl.program_id(2)
    kx_ref[0:half] = kp_ref[...]
    kx_ref[half:half + chunk] = k_ref[...]
    kx_ref[half + chunk:] = kn_ref[...]
    vx_ref[0:half] = vp_ref[...]
    vx_ref[half:half + chunk] = v_ref[...]
    vx_ref[half + chunk:] = vn_ref[...]

    nk = sub + 2 * half
    qi = lax.broadcasted_iota(jnp.int32, (2 * sub, nk), 0) % sub
    kj = lax.broadcasted_iota(jnp.int32, (2 * sub, nk), 1)
    band = jnp.abs(qi + half - kj) <= half
    lane = lax.broadcasted_iota(jnp.int32, (sub, LANES), 1)
    lane2 = lax.broadcasted_iota(jnp.int32, (1, 2 * LANES), 1) % LANES
    first = (lane2 < LANES // 2).astype(BF16)
    second = (lane2 >= LANES // 2).astype(BF16)

    bound = bound_ref[0]

    def sub_block(j, use_bound):
        r0 = j * sub
        base = i * chunk + j * sub
        lo = half - base
        hi = length + half - base
        mask = band & (kj >= lo) & (kj < hi)
        bias = jnp.where(mask, -bound if use_bound else 0.0, NEG_FILL)
        lse_tile = jnp.zeros((sub, LANES), F32)
        for p in range(A_HEADS // 2):
            cols = slice(p * 2 * LANES, (p + 1) * 2 * LANES)
            q2 = q_ref[pl.ds(r0, sub), cols]
            qq = jnp.concatenate([q2 * first, q2 * second], axis=0)
            s = lax.dot_general(qq, kx_ref[pl.ds(r0, nk), cols], NT_DIMS, preferred_element_type=F32) + bias
            ref = bound if use_bound else jnp.max(s, axis=-1, keepdims=True)
            e = jnp.exp2(s if use_bound else s - ref)
            den = jnp.sum(e, axis=-1, keepdims=True)
            eb = e.astype(BF16)
            lse = (ref + jnp.log2(den)) * (1.0 / LOG2E)
            for t in range(2):
                h = 2 * p + t
                hc = slice(h * LANES, (h + 1) * LANES)
                rs = slice(t * sub, (t + 1) * sub)
                o = jnp.dot(eb[rs], vx_ref[pl.ds(r0, nk), hc], preferred_element_type=F32) / den[rs]
                o_ref[pl.ds(r0, sub), hc] = o.astype(BF16)
                lse_tile = jnp.where(lane == h, lse[rs], lse_tile)
        lse_ref[pl.ds(r0, sub), :] = lse_tile

    def run(use_bound):
        for j in range(chunk // sub):
            sub_block(j, use_bound)

    run(True)
    head_lanes = lax.broadcasted_iota(jnp.int32, lse_ref.shape, 1) < A_HEADS
    lse_min = jnp.min(jnp.where(head_lanes, lse_ref[...], jnp.inf), keepdims=True)
    trusted = jnp.min(lse_min * LOG2E - bound) >= math.log2(L_MIN)

    @pl.when(jnp.logical_not(trusted))
    def _():
        run(False)


def _dil_attn(qkv, bound, g, chunk=512, sub=128):
    window, dil = DIL_CONFIGS[g]
    half = window // (2 * dil)
    _, batch, _, length, _ = qkv.shape
    chunk = min(chunk, length)
    assert length % chunk == 0 and chunk % sub == 0 and chunk % half == 0
    cph = chunk // half
    last = length // half - 1

    def main(which):
        return pl.BlockSpec((None, None, None, chunk, A_WIDTH), lambda b, c, i: (which, b, c, i, 0))

    def prev(which):
        return pl.BlockSpec((None, None, None, half, A_WIDTH),
                            lambda b, c, i: (which, b, c, jnp.maximum(i * cph - 1, 0), 0))

    def nxt(which):
        return pl.BlockSpec((None, None, None, half, A_WIDTH),
                            lambda b, c, i: (which, b, c, jnp.minimum((i + 1) * cph, last), 0))

    kern = functools.partial(_dil_attn_kernel, length=length, chunk=chunk, sub=sub, half=half)
    return pl.pallas_call(
        kern,
        grid=(batch, dil, length // chunk),
        in_specs=[pl.BlockSpec(memory_space=pltpu.SMEM), main(0), main(1), prev(1), nxt(1), main(2), prev(2), nxt(2)],
        out_specs=[
            pl.BlockSpec((None, None, chunk, A_WIDTH), lambda b, c, i: (b, c, i, 0)),
            pl.BlockSpec((None, None, chunk, LANES), lambda b, c, i: (b, c, i, 0)),
        ],
        out_shape=[
            jax.ShapeDtypeStruct((batch, dil, length, A_WIDTH), BF16),
            jax.ShapeDtypeStruct((batch, dil, length, LANES), F32),
        ],
        scratch_shapes=[pltpu.VMEM((chunk + 2 * half, A_WIDTH), BF16),
                        pltpu.VMEM((chunk + 2 * half, A_WIDTH), BF16)],
        compiler_params=_params(("parallel", "parallel", "parallel")),
        name=f"dil_attn_g{g}",
    )(bound, qkv, qkv, qkv, qkv, qkv, qkv, qkv)


def _merge_kernel(o0_ref, o1_ref, o2_ref, l0_ref, l1_ref, l2_ref, e_ref, out_ref, os_ref, ls_ref):
    o_refs = (o0_ref, o1_ref, o2_ref)
    l_refs = (l0_ref, l1_ref, l2_ref)
    nlb = os_ref.shape[1]
    for g in range(N_GROUPS):
        dil, rows = o_refs[g].shape[0], o_refs[g].shape[1]
        for c in range(dil):
            oc = o_refs[g][c].astype(F32)
            for kb in range(nlb):
                os_ref[g, kb, pl.ds(c, rows, stride=dil), :] = oc[:, kb * LANES:(kb + 1) * LANES]
            ls_ref[g, pl.ds(c, rows, stride=dil), :] = l_refs[g][c]
    lse = [ls_ref[g] for g in range(N_GROUPS)]
    mx = jnp.maximum(jnp.maximum(lse[0], lse[1]), lse[2])
    e = [jnp.exp(l - mx) for l in lse]
    inv = 1.0 / (e[0] + e[1] + e[2])
    acc = None
    for g in range(N_GROUPS):
        w = jnp.dot((e[g] * inv).astype(BF16), e_ref[...], preferred_element_type=F32)
        term = w * jnp.concatenate([os_ref[g, kb] for kb in range(nlb)], axis=-1)
        acc = term if acc is None else acc + term
    out_ref[...] = acc.astype(BF16)


def _merge(outs, lses, seq, tm):
    batch = outs[0].shape[0]
    spt = seq // tm
    expand = (jnp.arange(LANES)[:, None] == (jnp.arange(A_WIDTH)[None, :] // A_HEAD_DIM)).astype(BF16)

    def spec(arr):
        dil, width = arr.shape[1], arr.shape[3]
        return pl.BlockSpec((None, dil, tm // dil, width), lambda i: (i // spt, 0, i % spt, 0))

    return pl.pallas_call(
        _merge_kernel,
        grid=(batch * spt,),
        in_specs=[spec(o) for o in outs] + [spec(l) for l in lses] + [_resident((LANES, A_WIDTH))],
        out_specs=pl.BlockSpec((tm, A_WIDTH), lambda i: (i, 0)),
        out_shape=jax.ShapeDtypeStruct((batch * seq, A_WIDTH), BF16),
        scratch_shapes=[pltpu.VMEM((N_GROUPS, A_WIDTH // LANES, tm, LANES), F32),
                        pltpu.VMEM((N_GROUPS, tm, LANES), F32)],
        compiler_params=_params(("parallel",)),
        name="a_merge",
    )(*outs, *lses, expand)


def _ffn_kernel(x_ref, a_ref, wo_ref, nrm_ref, w1_ref, w2_ref, out_ref, *, ff_chunk):
    x1 = x_ref[...] + jnp.dot(a_ref[...], wo_ref[...], preferred_element_type=F32)
    h = (x1 * _rms_scale(x1, x1.shape[-1]) * nrm_ref[...]).astype(BF16)
    acc = x1
    for c in range(w1_ref.shape[1] // ff_chunk):
        cols = slice(c * ff_chunk, (c + 1) * ff_chunk)
        a = jnp.maximum(jnp.dot(h, w1_ref[:, cols], preferred_element_type=F32), 0.0)
        acc = acc + jnp.dot((a * a).astype(BF16), w2_ref[cols, :], preferred_element_type=F32)
    out_ref[...] = acc


def _ffn(x2d, a2d, w_o, nrm, w1, w2, tm, ff_chunk=1024):
    T, D = x2d.shape
    return pl.pallas_call(
        functools.partial(_ffn_kernel, ff_chunk=ff_chunk),
        grid=(T // tm,),
        in_specs=[
            pl.BlockSpec((tm, D), lambda i: (i, 0)),
            pl.BlockSpec((tm, a2d.shape[1]), lambda i: (i, 0)),
            _resident(w_o.shape),
            _resident(nrm.shape),
            _resident(w1.shape),
            _resident(w2.shape),
        ],
        out_specs=pl.BlockSpec((tm, D), lambda i: (i, 0)),
        out_shape=jax.ShapeDtypeStruct((T, D), F32),
        compiler_params=_params(("parallel",)),
        name="outproj_ffn",
    )(x2d, a2d, w_o, nrm, w1, w2)


def _b_proj_kernel(x_ref, nrm_ref, win_ref, qag_ref, kvag_ref, wqb_ref, wkn_ref, wv_ref,
                   gq_ref, gk_ref, cq_ref, ck_ref, mq_ref, mk_ref, cosq_ref, cosk_ref, sin_ref,
                   q_ref, k_ref, v_ref):
    x = x_ref[...]
    h = (x * _rms_scale(x, x.shape[-1]) * nrm_ref[...]).astype(BF16)
    lat = jnp.dot(h, win_ref[...], preferred_element_type=F32)
    c_q = lat[:, :Q_LORA]
    c_kv = lat[:, Q_LORA:Q_LORA + KV_LORA]
    k_rope = lat[:, Q_LORA + KV_LORA:Q_LORA + KV_LORA + LANES]
    k_roll = lat[:, Q_LORA + KV_LORA + LANES:]
    cqn = (c_q * _rms_scale(c_q, Q_LORA) * qag_ref[...]).astype(BF16)
    ckvn = (c_kv * _rms_scale(c_kv, KV_LORA) * kvag_ref[...]).astype(BF16)
    sin = sin_ref[...]

    vt_all = lax.dot_general(wv_ref[...], ckvn, NT_DIMS, preferred_element_type=F32)
    ones_rows = (lax.broadcasted_iota(jnp.int32, (VT_ROWS - V_DIM, x.shape[0]), 0) == 0).astype(BF16)
    for hh in range(B_HEADS):
        v_ref[hh * VT_ROWS:hh * VT_ROWS + V_DIM, :] = vt_all[hh * V_DIM:(hh + 1) * V_DIM, :].astype(BF16)
        v_ref[hh * VT_ROWS + V_DIM:(hh + 1) * VT_ROWS, :] = ones_rows

    q3 = jnp.dot(cqn, wqb_ref[...], preferred_element_type=F32)
    gq_n = gq_ref[:, :LANES]
    cos_q = cosq_ref[...]
    for hh in range(B_HEADS):
        q2 = q3[:, hh * Q3_PAD:hh * Q3_PAD + QK_PAD]
        ms = jnp.dot((q2 * q2).astype(BF16), mq_ref[...], preferred_element_type=F32)[:, :LANES]
        rq = lax.rsqrt(ms * (1.0 / QK_DIM) + EPS)
        qr = q2[:, LANES:] * cos_q + q3[:, hh * Q3_PAD + QK_PAD:(hh + 1) * Q3_PAD] * sin
        q_ref[:, hh * QK_PAD:hh * QK_PAD + LANES] = (q2[:, :LANES] * rq * gq_n).astype(BF16)
        q_ref[:, hh * QK_PAD + LANES:(hh + 1) * QK_PAD] = (qr * rq + cq_ref[...]).astype(BF16)

    kn_all = jnp.dot(ckvn, wkn_ref[...], preferred_element_type=F32)
    ss_kr = jnp.sum(k_rope * k_rope, axis=-1, keepdims=True)
    kr = k_rope * cosk_ref[...] + k_roll * sin
    gk_n = gk_ref[:, :LANES]
    for p in range(B_HEADS // 2):
        kn2 = kn_all[:, p * 2 * LANES:(p + 1) * 2 * LANES]
        ms = jnp.dot((kn2 * kn2).astype(BF16), mk_ref[...], preferred_element_type=F32)
        rk2 = lax.rsqrt((ms + ss_kr) * (1.0 / QK_DIM) + EPS)
        for t in range(2):
            hh = 2 * p + t
            rk = rk2[:, t * LANES:(t + 1) * LANES]
            k_ref[:, hh * QK_PAD:hh * QK_PAD + LANES] = (kn2[:, t * LANES:(t + 1) * LANES] * rk * gk_n).astype(BF16)
            k_ref[:, hh * QK_PAD + LANES:(hh + 1) * QK_PAD] = (kr * rk + ck_ref[...]).astype(BF16)


def _b_proj(x2d, consts, tables, seq, tm):
    T, D = x2d.shape
    spt = seq // tm
    row = lambda i: (i, 0)
    pos = pl.BlockSpec((tm, LANES), lambda i: (i % spt, 0))
    return pl.pallas_call(
        _b_proj_kernel,
        grid=(T // tm,),
        in_specs=[pl.BlockSpec((tm, D), row)] + [_resident(c.shape) for c in consts] + [pos] * len(tables),
        out_specs=[
            pl.BlockSpec((tm, B_HEADS * QK_PAD), row),
            pl.BlockSpec((tm, B_HEADS * QK_PAD), row),
            pl.BlockSpec((B_HEADS * VT_ROWS, tm), lambda i: (0, i)),
        ],
        out_shape=[
            jax.ShapeDtypeStruct((T, B_HEADS * QK_PAD), BF16),
            jax.ShapeDtypeStruct((T, B_HEADS * QK_PAD), BF16),
            jax.ShapeDtypeStruct((B_HEADS * VT_ROWS, T), BF16),
        ],
        compiler_params=_params(("parallel",)),
        name="b_proj",
    )(x2d, *consts, *tables)


def _mla_attn_kernel(q_ref, k_ref, vt_ref, o_ref, acc_ref, *, tk, unroll):
    q = q_ref[...]
    tq = q.shape[0]
    nkv = k_ref.shape[0] // tk

    def scores_t(kb):
        r0 = pl.multiple_of(kb * tk, tk)
        st = lax.dot_general(k_ref[pl.ds(r0, tk), :], q, NT_DIMS, preferred_element_type=F32)
        return st, vt_ref[:, pl.ds(r0, tk)]

    def fast(kb, carry):
        for u in range(unroll):
            st, vt = scores_t(kb * unroll + u)
            acc_ref[...] += jnp.dot(vt, jnp.exp2(st).astype(BF16), preferred_element_type=F32)
        return carry

    def finish():
        inv = 1.0 / acc_ref[V_DIM:V_DIM + 1, :]
        o_ref[...] = (acc_ref[:V_DIM, :] * inv).T.astype(BF16)

    acc_ref[...] = jnp.zeros_like(acc_ref)
    lax.fori_loop(0, nkv // unroll, fast, 0)
    trusted = jnp.min(acc_ref[V_DIM:V_DIM + 1, :]) >= L_MIN

    @pl.when(trusted)
    def _():
        finish()

    @pl.when(jnp.logical_not(trusted))
    def _():
        def slow(kb, m):
            st, vt = scores_t(kb)
            m_new = jnp.maximum(m, jnp.max(st, axis=0, keepdims=True))
            pt = jnp.exp2(st - m_new).astype(BF16)
            acc_ref[...] = jnp.exp2(m - m_new) * acc_ref[...] + jnp.dot(vt, pt, preferred_element_type=F32)
            return m_new

        acc_ref[...] = jnp.zeros_like(acc_ref)
        lax.fori_loop(0, nkv, slow, jnp.full((1, tq), -jnp.inf, F32))
        finish()


def _mla_attn(q, k, vt, batch, seq, tq=1024, tk=1024, unroll=8):
    nq = seq // tq
    unroll = min(unroll, seq // tk)
    assert seq % (tk * unroll) == 0
    return pl.pallas_call(
        functools.partial(_mla_attn_kernel, tk=tk, unroll=unroll),
        grid=(batch, B_HEADS, nq),
        in_specs=[
            pl.BlockSpec((tq, QK_PAD), lambda b, h, i: (b * nq + i, h)),
            pl.BlockSpec((seq, QK_PAD), lambda b, h, i: (b, h)),
            pl.BlockSpec((VT_ROWS, seq), lambda b, h, i: (h, b)),
        ],
        out_specs=pl.BlockSpec((tq, V_DIM), lambda b, h, i: (b * nq + i, h)),
        out_shape=jax.ShapeDtypeStruct((batch * seq, B_HEADS * V_DIM), BF16),
        scratch_shapes=[pltpu.VMEM((VT_ROWS, tq), F32)],
        compiler_params=_params(("parallel", "parallel", "parallel")),
        name="mla_attn",
    )(q, k, vt)


def _rope_angles(seq, d):
    pos = jnp.arange(seq, dtype=F32)
    freqs = ROPE_THETA ** (-jnp.arange(0, d, 2, dtype=F32) / d)
    ang = pos[:, None] * freqs[None, :]
    return jnp.cos(ang), jnp.sin(ang)


def _spread_rope(t, axis):
    a, b = jnp.split(t, 2, axis=axis)
    z = jnp.zeros_like(a)
    return jnp.concatenate([a, z, b, z], axis=axis)


def _pair_columns(w):
    d = w.shape[0]
    w = w.reshape(d, A_HEADS // 2, 2, 2, A_HEAD_DIM // 2)
    return w.transpose(0, 1, 3, 2, 4).reshape(d, A_WIDTH)


def _residue_major(table, tm, dil):
    s, w = table.shape
    return table.reshape(s // tm, tm // dil, dil, w).transpose(0, 2, 1, 3).reshape(s, w)


def kernel(x, norm_mix, norm_ffn, a_w_qkv, a_q_gain, a_k_gain, a_w_o, b_w_in, b_q_a_gain, b_w_qb,
           b_kv_a_gain, b_w_kvb, b_q_gain, b_k_gain, b_w_o, ffn_w1, ffn_w2):
    batch, seq, d_model = x.shape
    T = batch * seq
    x2d = x.reshape(T, d_model)

    tm_a = 512
    cos, sin = _rope_angles(seq, A_HEAD_DIM)
    cos_a = jnp.concatenate([cos, cos], -1)
    sin_a = jnp.concatenate([sin, sin], -1)
    pair_head = (jnp.arange(2 * LANES) % LANES) // (LANES // 2)
    mean_ones = (pair_head[:, None] == pair_head[None, :]).astype(BF16) * (1.0 / A_HEAD_DIM)
    w_qkv = a_w_qkv[0].reshape(d_model, 3, N_GROUPS, A_WIDTH)
    hd = A_HEAD_DIM // 2
    outs, lses = [], []
    for g, (_, dil) in enumerate(DIL_CONFIGS):
        w3 = jnp.stack([_pair_columns(w_qkv[:, 0, g]), _pair_columns(w_qkv[:, 1, g]), w_qkv[:, 2, g]]).astype(BF16)
        cos_g = _residue_major(cos_a, tm_a, dil)
        sin_g = _residue_major(sin_a, tm_a, dil)

        def rope_tables(gain):
            ga, gb = jnp.tile(gain[:hd], 2), jnp.tile(gain[hd:], 2)
            return jnp.stack([ga * cos_g, gb * sin_g, gb * cos_g, ga * sin_g])

        q_scale = LOG2E / math.sqrt(A_HEAD_DIM)
        tables = jnp.stack([rope_tables(a_q_gain[0, g] * q_scale), rope_tables(a_k_gain[0, g])])
        bound = 1.02 * A_HEAD_DIM * q_scale * jnp.max(jnp.abs(a_q_gain[0, g])) * jnp.max(jnp.abs(a_k_gain[0, g]))
        qkv = _a_proj(x2d, norm_mix[0][None, :], w3, tables, mean_ones, batch, seq, dil, tm_a)
        o, lse = _dil_attn(qkv, jnp.reshape(bound, (1,)).astype(F32), g)
        outs.append(o)
        lses.append(lse)
    merged = _merge(outs, lses, seq, tm=512)
    x2d = _ffn(x2d, merged, a_w_o[0].astype(BF16), norm_ffn[0][None, :],
               ffn_w1[0].astype(BF16), ffn_w2[0].astype(BF16), tm=512)

    cos, sin = _rope_angles(seq, ROPE_DIM)
    cos_b = _spread_rope(jnp.concatenate([cos, cos], -1), -1)
    sin_b = _spread_rope(jnp.concatenate([-sin, sin], -1), -1)

    def head_gain(gain, scale):
        return (jnp.concatenate([gain[:NOPE_DIM], _spread_rope(gain[NOPE_DIM:], 0)]) * scale)[None, :]

    q_scale = LOG2E / math.sqrt(QK_DIM)
    gq = head_gain(b_q_gain[0], q_scale)
    gk = head_gain(b_k_gain[0], 1.0)
    bound = 1.02 * QK_DIM * q_scale * jnp.max(jnp.abs(b_q_gain[0])) * jnp.max(jnp.abs(b_k_gain[0]))
    const_lane = (jnp.arange(LANES) == CONST_LANE).astype(F32)[None, :]

    def half_roll(w, gain):
        return jnp.roll(w * gain, LANES // 2, axis=-1)

    w_in = b_w_in[0]
    k_rope_w = _spread_rope(w_in[:, Q_LORA + KV_LORA:], 1)
    w_in = jnp.concatenate([w_in[:, :Q_LORA + KV_LORA], k_rope_w, half_roll(k_rope_w, gk[:, LANES:])], 1)
    w_qb = b_w_qb[0].reshape(Q_LORA, B_HEADS, QK_DIM)
    q_rope_w = _spread_rope(w_qb[..., NOPE_DIM:], 2)
    w_qb = jnp.concatenate([w_qb[..., :NOPE_DIM], q_rope_w, half_roll(q_rope_w, gq[:, LANES:])], -1)
    w_qb = w_qb.reshape(Q_LORA, B_HEADS * Q3_PAD)
    w_kvb = b_w_kvb[0].reshape(KV_LORA, B_HEADS, NOPE_DIM + V_DIM)
    w_kn = w_kvb[..., :NOPE_DIM].reshape(KV_LORA, B_HEADS * NOPE_DIM)
    w_v = w_kvb[..., NOPE_DIM:].reshape(KV_LORA, B_HEADS * V_DIM).T
    sum_q = jnp.ones((QK_PAD, QK_PAD), BF16)
    sum_k = ((jnp.arange(2 * LANES)[:, None] // LANES) == (jnp.arange(2 * LANES)[None, :] // LANES)).astype(BF16)

    consts = (norm_mix[1][None, :], w_in.astype(BF16), b_q_a_gain[0][None, :], b_kv_a_gain[0][None, :],
              w_qb.astype(BF16), w_kn.astype(BF16), w_v.astype(BF16), gq, gk,
              -bound * const_lane, const_lane, sum_q, sum_k)
    tables = (cos_b * gq[:, LANES:], cos_b * gk[:, LANES:], sin_b)
    q, k, v = _b_proj(x2d, consts, tables, seq, tm=512)
    o = _mla_attn(q, k, v, batch, seq)
    x2d = _ffn(x2d, o, b_w_o[0].astype(BF16), norm_ffn[1][None, :],
               ffn_w1[1].astype(BF16), ffn_w2[1].astype(BF16), tm=512)
    return x2d.reshape(batch, seq, d_model)
```

```python
import functools
import math

import jax
import jax.numpy as jnp
from jax import lax
from jax.experimental import pallas as pl
from jax.experimental.pallas import tpu as pltpu

EPS = 1e-6
ROPE_THETA = 10000.0
NEG_FILL = -1e30
LOG2E = math.log2(math.e)

DIL_CONFIGS = ((128, 1), (512, 4), (2048, 16))
N_GROUPS = len(DIL_CONFIGS)
A_HEADS = 8
A_HEAD_DIM = 128
A_WIDTH = A_HEADS * A_HEAD_DIM

B_HEADS = 8
Q_LORA = 256
KV_LORA = 128
NOPE_DIM = 128
ROPE_DIM = 64
V_DIM = 128
QK_DIM = NOPE_DIM + ROPE_DIM
QK_PAD = 256
Q3_PAD = 384
VT_ROWS = 144
CONST_LANE = 32
L_MIN = 2.0 ** -80

LANES = 128
VMEM_LIMIT = 48 * 1024 * 1024

BF16 = jnp.bfloat16
F32 = jnp.float32
NT_DIMS = (((1,), (1,)), ((), ()))


def _params(sem):
    return pltpu.CompilerParams(dimension_semantics=sem, vmem_limit_bytes=VMEM_LIMIT)


def _resident(shape):
    zeros = (0,) * len(shape)
    return pl.BlockSpec(shape, lambda *_: zeros, pipeline_mode=pl.Buffered(1))


def _rms_scale(x, width):
    return lax.rsqrt(jnp.sum(x * x, axis=-1, keepdims=True) * (1.0 / width) + EPS)


def _a_proj_kernel(x_ref, nrm_ref, w_ref, gain_ref, tab_ref, ones_ref, o_ref, h_ref, hs_ref, *, dil):
    tm = x_ref.shape[0]
    rows = tm // dil
    x = x_ref[...]
    hn = x * _rms_scale(x, x.shape[-1]) * nrm_ref[...]
    if dil == 1:
        h_ref[...] = hn.astype(BF16)
    else:
        for kb in range(hs_ref.shape[0]):
            hs_ref[kb] = hn[:, kb * LANES:(kb + 1) * LANES]
        for c in range(dil):
            for kb in range(hs_ref.shape[0]):
                h_ref[c * rows:(c + 1) * rows, kb * LANES:(kb + 1) * LANES] = (
                    hs_ref[kb, pl.ds(c, rows, stride=dil), :].astype(BF16))

    def store(which, col0, val):
        for c in range(dil):
            o_ref[which, c, :, col0:col0 + LANES] = val[c * rows:(c + 1) * rows].astype(BF16)

    for which in range(2):
        y = jnp.dot(h_ref[...], w_ref[which], preferred_element_type=F32)
        ga, gb = gain_ref[which, 0:1, :], gain_ref[which, 1:2, :]
        cos, sin = tab_ref[0], tab_ref[1]
        ca, sb, cb, sa = ga * cos, gb * sin, gb * cos, ga * sin
        for p in range(A_HEADS // 2):
            y2 = y[:, p * 2 * LANES:(p + 1) * 2 * LANES]
            ms = jnp.dot((y2 * y2).astype(BF16), ones_ref[...], preferred_element_type=F32)[:, :LANES]
            rinv = lax.rsqrt(ms + EPS)
            ar = y2[:, :LANES] * rinv
            br = y2[:, LANES:] * rinv
            store(which, p * 2 * LANES, ar * ca - br * sb)
            store(which, p * 2 * LANES + LANES, br * cb + ar * sa)

    y = jnp.dot(h_ref[...], w_ref[2], preferred_element_type=F32)
    for hh in range(A_HEADS):
        store(2, hh * LANES, y[:, hh * LANES:(hh + 1) * LANES])


def _a_proj(x2d, nrm, w3, gains, tables, ones, batch, seq, dil, tm):
    T, D = x2d.shape
    spt = seq // tm
    rows = tm // dil
    return pl.pallas_call(
        functools.partial(_a_proj_kernel, dil=dil),
        grid=(T // tm,),
        in_specs=[
            pl.BlockSpec((tm, D), lambda i: (i, 0)),
            _resident(nrm.shape),
            _resident(w3.shape),
            _resident(gains.shape),
            pl.BlockSpec((2, tm, LANES), lambda i: (0, i % spt, 0)),
            _resident(ones.shape),
        ],
        out_specs=pl.BlockSpec((3, None, dil, rows, A_WIDTH), lambda i: (0, i // spt, 0, i % spt, 0)),
        out_shape=jax.ShapeDtypeStruct((3, batch, dil, seq // dil, A_WIDTH), BF16),
        scratch_shapes=[pltpu.VMEM((tm, D), BF16), pltpu.VMEM((D // LANES, tm, LANES), F32)],
        compiler_params=_params(("parallel",)),
        name=f"a_proj_d{dil}",
    )(x2d, nrm, w3, gains, tables, ones)


def _dil_attn_kernel(bound_ref, q_ref, k_ref, kp_ref, kn_ref, v_ref, vp_ref, vn_ref, hot_ref,
                     o_ref, st_ref, kx_ref, vx_ref, *, length, chunk, sub, half):
    i = pl.program_id(2)
    kx_ref[0:half] = kp_ref[...]
    kx_ref[half:half + chunk] = k_ref[...]
    kx_ref[half + chunk:] = kn_ref[...]
    vx_ref[0:half] = vp_ref[...]
    vx_ref[half:half + chunk] = v_ref[...]
    vx_ref[half + chunk:] = vn_ref[...]

    nk = sub + 2 * half
    qi = lax.broadcasted_iota(jnp.int32, (2 * sub, nk), 0) % sub
    kj = lax.broadcasted_iota(jnp.int32, (2 * sub, nk), 1)
    band = jnp.abs(qi + half - kj) <= half
    lane = lax.broadcasted_iota(jnp.int32, (sub, LANES), 1)
    lane2 = lax.broadcasted_iota(jnp.int32, (1, 2 * LANES), 1) % LANES
    first = (lane2 < LANES // 2).astype(BF16)
    second = (lane2 >= LANES // 2).astype(BF16)

    bound = bound_ref[0]

    def sub_block(j, use_bound):
        r0 = j * sub
        base = i * chunk + j * sub
        lo = half - base
        hi = length + half - base
        mask = band & (kj >= lo) & (kj < hi)
        bias = jnp.where(mask, -bound if use_bound else 0.0, NEG_FILL)
        stats = jnp.zeros((sub, LANES), F32)
        for p in range(A_HEADS // 2):
            cols = slice(p * 2 * LANES, (p + 1) * 2 * LANES)
            q2 = q_ref[pl.ds(r0, sub), cols]
            qq = jnp.concatenate([q2 * first, q2 * second], axis=0)
            s = lax.dot_general(qq, kx_ref[pl.ds(r0, nk), cols], NT_DIMS, preferred_element_type=F32) + bias
            if use_bound:
                eb = jnp.exp2(s).astype(BF16)
            else:
                ref = jnp.max(s, axis=-1, keepdims=True)
                e = jnp.exp2(s - ref)
                den = jnp.sum(e, axis=-1, keepdims=True)
                eb = e.astype(BF16)
                lse = (ref + jnp.log2(den)) * (1.0 / LOG2E)
            for t in range(2):
                h = 2 * p + t
                hc = slice(h * LANES, (h + 1) * LANES)
                rs = slice(t * sub, (t + 1) * sub)
                v = vx_ref[pl.ds(r0, nk), hc]
                if use_bound:
                    r = jnp.dot(eb[rs], jnp.concatenate([v, hot_ref[h]], axis=1), preferred_element_type=F32)
                    o_ref[pl.ds(r0, sub), hc] = r[:, :LANES].astype(BF16)
                    stats = stats + r[:, LANES:]
                else:
                    o = jnp.dot(eb[rs], v, preferred_element_type=F32) / den[rs]
                    o_ref[pl.ds(r0, sub), hc] = o.astype(BF16)
                    stats = jnp.where(lane == h, lse[rs], stats)
        if use_bound:
            l2 = jnp.log2(stats)
            stats = jnp.where(lane < A_HEADS, (bound + l2) * (1.0 / LOG2E), jnp.where(lane < 2 * A_HEADS, l2, 0.0))
        st_ref[pl.ds(r0, sub), :] = stats

    def run(use_bound):
        for j in range(chunk // sub):
            sub_block(j, use_bound)

    run(True)
    head_lanes = lax.broadcasted_iota(jnp.int32, st_ref.shape, 1) < A_HEADS
    lse_min = jnp.min(jnp.where(head_lanes, st_ref[...], jnp.inf), keepdims=True)
    trusted = jnp.min(lse_min * LOG2E - bound) >= math.log2(L_MIN)

    @pl.when(jnp.logical_not(trusted))
    def _():
        run(False)


def _dil_attn(qkv, bound, g, chunk=512, sub=128):
    window, dil = DIL_CONFIGS[g]
    half = window // (2 * dil)
    _, batch, _, length, _ = qkv.shape
    chunk = min(chunk, length)
    assert length % chunk == 0 and chunk % sub == 0 and chunk % half == 0
    cph = chunk // half
    last = length // half - 1

    def main(which):
        return pl.BlockSpec((None, None, None, chunk, A_WIDTH), lambda b, c, i: (which, b, c, i, 0))

    def prev(which):
        return pl.BlockSpec((None, None, None, half, A_WIDTH),
                            lambda b, c, i: (which, b, c, jnp.maximum(i * cph - 1, 0), 0))

    def nxt(which):
        return pl.BlockSpec((None, None, None, half, A_WIDTH),
                            lambda b, c, i: (which, b, c, jnp.minimum((i + 1) * cph, last), 0))

    col = jnp.arange(LANES)[None, None, :]
    head = jnp.arange(A_HEADS)[:, None, None]
    hot = jnp.broadcast_to((col == head) | (col == head + A_HEADS), (A_HEADS, sub + 2 * half, LANES)).astype(BF16)
    kern = functools.partial(_dil_attn_kernel, length=length, chunk=chunk, sub=sub, half=half)
    return pl.pallas_call(
        kern,
        grid=(batch, dil, length // chunk),
        in_specs=[pl.BlockSpec(memory_space=pltpu.SMEM), main(0), main(1), prev(1), nxt(1), main(2), prev(2), nxt(2),
                  _resident(hot.shape)],
        out_specs=[
            pl.BlockSpec((None, None, chunk, A_WIDTH), lambda b, c, i: (b, c, i, 0)),
            pl.BlockSpec((None, None, chunk, LANES), lambda b, c, i: (b, c, i, 0)),
        ],
        out_shape=[
            jax.ShapeDtypeStruct((batch, dil, length, A_WIDTH), BF16),
            jax.ShapeDtypeStruct((batch, dil, length, LANES), F32),
        ],
        scratch_shapes=[pltpu.VMEM((chunk + 2 * half, A_WIDTH), BF16),
                        pltpu.VMEM((chunk + 2 * half, A_WIDTH), BF16)],
        compiler_params=_params(("parallel", "parallel", "parallel")),
        name=f"dil_attn_g{g}",
    )(bound, qkv, qkv, qkv, qkv, qkv, qkv, qkv, hot)


def _merge_kernel(o0_ref, o1_ref, o2_ref, l0_ref, l1_ref, l2_ref, e_ref, out_ref, os_ref, ls_ref):
    o_refs = (o0_ref, o1_ref, o2_ref)
    l_refs = (l0_ref, l1_ref, l2_ref)
    nlb = os_ref.shape[1]
    for g in range(N_GROUPS):
        dil, rows = o_refs[g].shape[0], o_refs[g].shape[1]
        for c in range(dil):
            oc = o_refs[g][c].astype(F32)
            for kb in range(nlb):
                os_ref[g, kb, pl.ds(c, rows, stride=dil), :] = oc[:, kb * LANES:(kb + 1) * LANES]
            ls_ref[g, pl.ds(c, rows, stride=dil), :] = l_refs[g][c]
    stats = [ls_ref[g] for g in range(N_GROUPS)]
    mx = jnp.maximum(jnp.maximum(stats[0], stats[1]), stats[2])
    e = [jnp.exp(st - mx) for st in stats]
    inv = 1.0 / (e[0] + e[1] + e[2])
    acc = None
    for g in range(N_GROUPS):
        pending = jnp.exp2(-pltpu.roll(stats[g], LANES - A_HEADS, 1))
        w = jnp.dot((e[g] * inv * pending).astype(BF16), e_ref[...], preferred_element_type=F32)
        term = w * jnp.concatenate([os_ref[g, kb] for kb in range(nlb)], axis=-1)
        acc = term if acc is None else acc + term
    out_ref[...] = acc.astype(BF16)


def _merge(outs, lses, seq, tm):
    batch = outs[0].shape[0]
    spt = seq // tm
    expand = (jnp.arange(LANES)[:, None] == (jnp.arange(A_WIDTH)[None, :] // A_HEAD_DIM)).astype(BF16)

    def spec(arr):
        dil, width = arr.shape[1], arr.shape[3]
        return pl.BlockSpec((None, dil, tm // dil, width), lambda i: (i // spt, 0, i % spt, 0))

    return pl.pallas_call(
        _merge_kernel,
        grid=(batch * spt,),
        in_specs=[spec(o) for o in outs] + [spec(l) for l in lses] + [_resident((LANES, A_WIDTH))],
        out_specs=pl.BlockSpec((tm, A_WIDTH), lambda i: (i, 0)),
        out_shape=jax.ShapeDtypeStruct((batch * seq, A_WIDTH), BF16),
        scratch_shapes=[pltpu.VMEM((N_GROUPS, A_WIDTH // LANES, tm, LANES), F32),
                        pltpu.VMEM((N_GROUPS, tm, LANES), F32)],
        compiler_params=_params(("parallel",)),
        name="a_merge",
    )(*outs, *lses, expand)


def _ffn_kernel(x_ref, a_ref, wo_ref, nrm_ref, w1_ref, w2_ref, out_ref, *, ff_chunk):
    x1 = x_ref[...] + jnp.dot(a_ref[...], wo_ref[...], preferred_element_type=F32)
    h = (x1 * _rms_scale(x1, x1.shape[-1]) * nrm_ref[...]).astype(BF16)
    acc = x1
    for c in range(w1_ref.shape[1] // ff_chunk):
        cols = slice(c * ff_chunk, (c + 1) * ff_chunk)
        a = jnp.maximum(jnp.dot(h, w1_ref[:, cols], preferred_element_type=F32), 0.0)
        acc = acc + jnp.dot((a * a).astype(BF16), w2_ref[cols, :], preferred_element_type=F32)
    out_ref[...] = acc


def _ffn(x2d, a2d, w_o, nrm, w1, w2, tm, ff_chunk=1024):
    T, D = x2d.shape
    return pl.pallas_call(
        functools.partial(_ffn_kernel, ff_chunk=ff_chunk),
        grid=(T // tm,),
        in_specs=[
            pl.BlockSpec((tm, D), lambda i: (i, 0)),
            pl.BlockSpec((tm, a2d.shape[1]), lambda i: (i, 0)),
            _resident(w_o.shape),
            _resident(nrm.shape),
            _resident(w1.shape),
            _resident(w2.shape),
        ],
        out_specs=pl.BlockSpec((tm, D), lambda i: (i, 0)),
        out_shape=jax.ShapeDtypeStruct((T, D), F32),
        compiler_params=_params(("parallel",)),
        name="outproj_ffn",
    )(x2d, a2d, w_o, nrm, w1, w2)


def _b_proj_kernel(x_ref, nrm_ref, win_ref, qag_ref, kvag_ref, wqb_ref, wkn_ref, wv_ref,
                   gq_ref, gk_ref, cq_ref, ck_ref, mq_ref, mk_ref, cosq_ref, cosk_ref, sin_ref,
                   q_ref, k_ref, v_ref):
    x = x_ref[...]
    h = (x * _rms_scale(x, x.shape[-1]) * nrm_ref[...]).astype(BF16)
    lat = jnp.dot(h, win_ref[...], preferred_element_type=F32)
    c_q = lat[:, :Q_LORA]
    c_kv = lat[:, Q_LORA:Q_LORA + KV_LORA]
    k_rope = lat[:, Q_LORA + KV_LORA:Q_LORA + KV_LORA + LANES]
    k_roll = lat[:, Q_LORA + KV_LORA + LANES:]
    cqn = (c_q * _rms_scale(c_q, Q_LORA) * qag_ref[...]).astype(BF16)
    ckvn = (c_kv * _rms_scale(c_kv, KV_LORA) * kvag_ref[...]).astype(BF16)
    sin = sin_ref[...]

    vt_all = lax.dot_general(wv_ref[...], ckvn, NT_DIMS, preferred_element_type=F32)
    ones_rows = (lax.broadcasted_iota(jnp.int32, (VT_ROWS - V_DIM, x.shape[0]), 0) == 0).astype(BF16)
    for hh in range(B_HEADS):
        v_ref[hh * VT_ROWS:hh * VT_ROWS + V_DIM, :] = vt_all[hh * V_DIM:(hh + 1) * V_DIM, :].astype(BF16)
        v_ref[hh * VT_ROWS + V_DIM:(hh + 1) * VT_ROWS, :] = ones_rows

    q3 = jnp.dot(cqn, wqb_ref[...], preferred_element_type=F32)
    gq_n = gq_ref[:, :LANES]
    cos_q = cosq_ref[...]
    for hh in range(B_HEADS):
        q2 = q3[:, hh * Q3_PAD:hh * Q3_PAD + QK_PAD]
        ms = jnp.dot((q2 * q2).astype(BF16), mq_ref[...], preferred_element_type=F32)[:, :LANES]
        rq = lax.rsqrt(ms * (1.0 / QK_DIM) + EPS)
        qr = q2[:, LANES:] * cos_q + q3[:, hh * Q3_PAD + QK_PAD:(hh + 1) * Q3_PAD] * sin
        q_ref[:, hh * QK_PAD:hh * QK_PAD + LANES] = (q2[:, :LANES] * rq * gq_n).astype(BF16)
        q_ref[:, hh * QK_PAD + LANES:(hh + 1) * QK_PAD] = (qr * rq + cq_ref[...]).astype(BF16)

    kn_all = jnp.dot(ckvn, wkn_ref[...], preferred_element_type=F32)
    ss_kr = jnp.sum(k_rope * k_rope, axis=-1, keepdims=True)
    kr = k_rope * cosk_ref[...] + k_roll * sin
    gk_n = gk_ref[:, :LANES]
    for p in range(B_HEADS // 2):
        kn2 = kn_all[:, p * 2 * LANES:(p + 1) * 2 * LANES]
        ms = jnp.dot((kn2 * kn2).astype(BF16), mk_ref[...], preferred_element_type=F32)
        rk2 = lax.rsqrt((ms + ss_kr) * (1.0 / QK_DIM) + EPS)
        for t in range(2):
            hh = 2 * p + t
            rk = rk2[:, t * LANES:(t + 1) * LANES]
            k_ref[:, hh * QK_PAD:hh * QK_PAD + LANES] = (kn2[:, t * LANES:(t + 1) * LANES] * rk * gk_n).astype(BF16)
            k_ref[:, hh * QK_PAD + LANES:(hh + 1) * QK_PAD] = (kr * rk + ck_ref[...]).astype(BF16)


def _b_proj(x2d, consts, tables, seq, tm):
    T, D = x2d.shape
    spt = seq // tm
    row = lambda i: (i, 0)
    pos = pl.BlockSpec((tm, LANES), lambda i: (i % spt, 0))
    return pl.pallas_call(
        _b_proj_kernel,
        grid=(T // tm,),
        in_specs=[pl.BlockSpec((tm, D), row)] + [_resident(c.shape) for c in consts] + [pos] * len(tables),
        out_specs=[
            pl.BlockSpec((tm, B_HEADS * QK_PAD), row),
            pl.BlockSpec((tm, B_HEADS * QK_PAD), row),
            pl.BlockSpec((B_HEADS * VT_ROWS, tm), lambda i: (0, i)),
        ],
        out_shape=[
            jax.ShapeDtypeStruct((T, B_HEADS * QK_PAD), BF16),
            jax.ShapeDtypeStruct((T, B_HEADS * QK_PAD), BF16),
            jax.ShapeDtypeStruct((B_HEADS * VT_ROWS, T), BF16),
        ],
        compiler_params=_params(("parallel",)),
        name="b_proj",
    )(x2d, *consts, *tables)


def _mla_attn_kernel(q_ref, k_ref, vt_ref, o_ref, acc_ref, *, tk, unroll):
    q = q_ref[...]
    tq = q.shape[0]
    nkv = k_ref.shape[0] // tk

    def scores_t(kb):
        r0 = pl.multiple_of(kb * tk, tk)
        st = lax.dot_general(k_ref[pl.ds(r0, tk), :], q, NT_DIMS, preferred_element_type=F32)
        return st, vt_ref[:, pl.ds(r0, tk)]

    def fast(kb, carry):
        for u in range(unroll):
            st, vt = scores_t(kb * unroll + u)
            acc_ref[...] += jnp.dot(vt, jnp.exp2(st).astype(BF16), preferred_element_type=F32)
        return carry

    def finish():
        inv = 1.0 / acc_ref[V_DIM:V_DIM + 1, :]
        o_ref[...] = (acc_ref[:V_DIM, :] * inv).T.astype(BF16)

    acc_ref[...] = jnp.zeros_like(acc_ref)
    lax.fori_loop(0, nkv // unroll, fast, 0)
    trusted = jnp.min(acc_ref[V_DIM:V_DIM + 1, :]) >= L_MIN

    @pl.when(trusted)
    def _():
        finish()

    @pl.when(jnp.logical_not(trusted))
    def _():
        def slow(kb, m):
            st, vt = scores_t(kb)
            m_new = jnp.maximum(m, jnp.max(st, axis=0, keepdims=True))
            pt = jnp.exp2(st - m_new).astype(BF16)
            acc_ref[...] = jnp.exp2(m - m_new) * acc_ref[...] + jnp.dot(vt, pt, preferred_element_type=F32)
            return m_new

        acc_ref[...] = jnp.zeros_like(acc_ref)
        lax.fori_loop(0, nkv, slow, jnp.full((1, tq), -jnp.inf, F32))
        finish()


def _mla_attn(q, k, vt, batch, seq, tq=1024, tk=1024, unroll=8):
    nq = seq // tq
    unroll = min(unroll, seq // tk)
    assert seq % (tk * unroll) == 0
    return pl.pallas_call(
        functools.partial(_mla_attn_kernel, tk=tk, unroll=unroll),
        grid=(batch, B_HEADS, nq),
        in_specs=[
            pl.BlockSpec((tq, QK_PAD), lambda b, h, i: (b * nq + i, h)),
            pl.BlockSpec((seq, QK_PAD), lambda b, h, i: (b, h)),
            pl.BlockSpec((VT_ROWS, seq), lambda b, h, i: (h, b)),
        ],
        out_specs=pl.BlockSpec((tq, V_DIM), lambda b, h, i: (b * nq + i, h)),
        out_shape=jax.ShapeDtypeStruct((batch * seq, B_HEADS * V_DIM), BF16),
        scratch_shapes=[pltpu.VMEM((VT_ROWS, tq), F32)],
        compiler_params=_params(("parallel", "parallel", "parallel")),
        name="mla_attn",
    )(q, k, vt)


def _rope_angles(seq, d):
    pos = jnp.arange(seq, dtype=F32)
    freqs = ROPE_THETA ** (-jnp.arange(0, d, 2, dtype=F32) / d)
    ang = pos[:, None] * freqs[None, :]
    return jnp.cos(ang), jnp.sin(ang)


def _spread_rope(t, axis):
    a, b = jnp.split(t, 2, axis=axis)
    z = jnp.zeros_like(a)
    return jnp.concatenate([a, z, b, z], axis=axis)


def _pair_columns(w):
    d = w.shape[0]
    w = w.reshape(d, A_HEADS // 2, 2, 2, A_HEAD_DIM // 2)
    return w.transpose(0, 1, 3, 2, 4).reshape(d, A_WIDTH)


def _residue_major(table, tm, dil):
    s, w = table.shape
    return table.reshape(s // tm, tm // dil, dil, w).transpose(0, 2, 1, 3).reshape(s, w)


def kernel(x, norm_mix, norm_ffn, a_w_qkv, a_q_gain, a_k_gain, a_w_o, b_w_in, b_q_a_gain, b_w_qb,
           b_kv_a_gain, b_w_kvb, b_q_gain, b_k_gain, b_w_o, ffn_w1, ffn_w2):
    batch, seq, d_model = x.shape
    T = batch * seq
    x2d = x.reshape(T, d_model)

    tm_a = 512
    cos, sin = _rope_angles(seq, A_HEAD_DIM)
    cos_a = jnp.concatenate([cos, cos], -1)
    sin_a = jnp.concatenate([sin, sin], -1)
    pair_head = (jnp.arange(2 * LANES) % LANES) // (LANES // 2)
    mean_ones = (pair_head[:, None] == pair_head[None, :]).astype(BF16) * (1.0 / A_HEAD_DIM)
    w_qkv = a_w_qkv[0].reshape(d_model, 3, N_GROUPS, A_WIDTH)
    hd = A_HEAD_DIM // 2
    outs, lses = [], []
    for g, (_, dil) in enumerate(DIL_CONFIGS):
        w3 = jnp.stack([_pair_columns(w_qkv[:, 0, g]), _pair_columns(w_qkv[:, 1, g]), w_qkv[:, 2, g]]).astype(BF16)
        tables = jnp.stack([_residue_major(cos_a, tm_a, dil), _residue_major(sin_a, tm_a, dil)])

        def pair_gain(gain):
            return jnp.stack([jnp.tile(gain[:hd], 2), jnp.tile(gain[hd:], 2)])

        q_scale = LOG2E / math.sqrt(A_HEAD_DIM)
        gains = jnp.stack([pair_gain(a_q_gain[0, g] * q_scale), pair_gain(a_k_gain[0, g])])
        bound = 1.02 * A_HEAD_DIM * q_scale * jnp.max(jnp.abs(a_q_gain[0, g])) * jnp.max(jnp.abs(a_k_gain[0, g]))
        qkv = _a_proj(x2d, norm_mix[0][None, :], w3, gains, tables, mean_ones, batch, seq, dil, tm_a)
        o, lse = _dil_attn(qkv, jnp.reshape(bound, (1,)).astype(F32), g)
        outs.append(o)
        lses.append(lse)
    merged = _merge(outs, lses, seq, tm=512)
    x2d = _ffn(x2d, merged, a_w_o[0].astype(BF16), norm_ffn[0][None, :],
               ffn_w1[0].astype(BF16), ffn_w2[0].astype(BF16), tm=512)

    assert A_HEAD_DIM == 2 * ROPE_DIM
    cos, sin = cos[:, 0::2], sin[:, 0::2]
    cos_b = _spread_rope(jnp.concatenate([cos, cos], -1), -1)
    sin_b = _spread_rope(jnp.concatenate([-sin, sin], -1), -1)

    def head_gain(gain, scale):
        return (jnp.concatenate([gain[:NOPE_DIM], _spread_rope(gain[NOPE_DIM:], 0)]) * scale)[None, :]

    q_scale = LOG2E / math.sqrt(QK_DIM)
    gq = head_gain(b_q_gain[0], q_scale)
    gk = head_gain(b_k_gain[0], 1.0)
    bound = 1.02 * QK_DIM * q_scale * jnp.max(jnp.abs(b_q_gain[0])) * jnp.max(jnp.abs(b_k_gain[0]))
    const_lane = (jnp.arange(LANES) == CONST_LANE).astype(F32)[None, :]

    def half_roll(w, gain):
        return jnp.roll(w * gain, LANES // 2, axis=-1)

    w_in = b_w_in[0]
    k_rope_w = _spread_rope(w_in[:, Q_LORA + KV_LORA:], 1)
    w_in = jnp.concatenate([w_in[:, :Q_LORA + KV_LORA], k_rope_w, half_roll(k_rope_w, gk[:, LANES:])], 1)
    w_qb = b_w_qb[0].reshape(Q_LORA, B_HEADS, QK_DIM)
    q_rope_w = _spread_rope(w_qb[..., NOPE_DIM:], 2)
    w_qb = jnp.concatenate([w_qb[..., :NOPE_DIM], q_rope_w, half_roll(q_rope_w, gq[:, LANES:])], -1)
    w_qb = w_qb.reshape(Q_LORA, B_HEADS * Q3_PAD)
    w_kvb = b_w_kvb[0].reshape(KV_LORA, B_HEADS, NOPE_DIM + V_DIM)
    w_kn = w_kvb[..., :NOPE_DIM].reshape(KV_LORA, B_HEADS * NOPE_DIM)
    w_v = w_kvb[..., NOPE_DIM:].reshape(KV_LORA, B_HEADS * V_DIM).T
    sum_q = jnp.ones((QK_PAD, QK_PAD), BF16)
    sum_k = ((jnp.arange(2 * LANES)[:, None] // LANES) == (jnp.arange(2 * LANES)[None, :] // LANES)).astype(BF16)

    consts = (norm_mix[1][None, :], w_in.astype(BF16), b_q_a_gain[0][None, :], b_kv_a_gain[0][None, :],
              w_qb.astype(BF16), w_kn.astype(BF16), w_v.astype(BF16), gq, gk,
              -bound * const_lane, const_lane, sum_q, sum_k)
    tables = (cos_b * gq[:, LANES:], cos_b * gk[:, LANES:], sin_b)
    q, k, v = _b_proj(x2d, consts, tables, seq, tm=512)
    o = _mla_attn(q, k, v, batch, seq)
    x2d = _ffn(x2d, o, b_w_o[0].astype(BF16), norm_ffn[1][None, :],
               ffn_w1[1].astype(BF16), ffn_w2[1].astype(BF16), tm=512)
    return x2d.reshape(batch, seq, d_model)
```

```python
import functools
import math

import jax
import jax.numpy as jnp
from jax import lax
from jax.experimental import pallas as pl
from jax.experimental.pallas import tpu as pltpu

EPS = 1e-6
ROPE_THETA = 10000.0
NEG_FILL = -1e30
LOG2E = math.log2(math.e)

DIL_CONFIGS = ((128, 1), (512, 4), (2048, 16))
N_GROUPS = len(DIL_CONFIGS)
A_HEADS = 8
A_HEAD_DIM = 128
A_WIDTH = A_HEADS * A_HEAD_DIM

B_HEADS = 8
Q_LORA = 256
KV_LORA = 128
NOPE_DIM = 128
ROPE_DIM = 64
V_DIM = 128
QK_DIM = NOPE_DIM + ROPE_DIM
QK_PAD = 256
Q3_PAD = 384
VT_ROWS = 144
CONST_LANE = 32
L_MIN = 2.0 ** -80

LANES = 128
VMEM_LIMIT = 48 * 1024 * 1024

BF16 = jnp.bfloat16
F32 = jnp.float32
NT_DIMS = (((1,), (1,)), ((), ()))


def _params(sem):
    return pltpu.CompilerParams(dimension_semantics=sem, vmem_limit_bytes=VMEM_LIMIT)


def _resident(shape):
    zeros = (0,) * len(shape)
    return pl.BlockSpec(shape, lambda *_: zeros, pipeline_mode=pl.Buffered(1))


def _rms_scale(x, width):
    return lax.rsqrt(jnp.sum(x * x, axis=-1, keepdims=True) * (1.0 / width) + EPS)


def _a_proj_kernel(x_ref, nrm_ref, w_ref, gain_ref, tab_ref, ones_ref, o_ref, h_ref, hs_ref, *, dil):
    tm = x_ref.shape[0]
    rows = tm // dil
    x = x_ref[...]
    hn = x * _rms_scale(x, x.shape[-1]) * nrm_ref[...]
    if dil == 1:
        h_ref[...] = hn.astype(BF16)
    else:
        for kb in range(hs_ref.shape[0]):
            hs_ref[kb] = hn[:, kb * LANES:(kb + 1) * LANES]
        for c in range(dil):
            for kb in range(hs_ref.shape[0]):
                h_ref[c * rows:(c + 1) * rows, kb * LANES:(kb + 1) * LANES] = (
                    hs_ref[kb, pl.ds(c, rows, stride=dil), :].astype(BF16))

    def store(which, col0, val):
        for c in range(dil):
            o_ref[which, c, :, col0:col0 + LANES] = val[c * rows:(c + 1) * rows].astype(BF16)

    for which in range(2):
        y = jnp.dot(h_ref[...], w_ref[which], preferred_element_type=F32)
        ga, gb = gain_ref[which, 0:1, :], gain_ref[which, 1:2, :]
        cos, sin = tab_ref[0], tab_ref[1]
        ca, sb, cb, sa = ga * cos, gb * sin, gb * cos, ga * sin
        for p in range(A_HEADS // 2):
            y2 = y[:, p * 2 * LANES:(p + 1) * 2 * LANES]
            ms = jnp.dot((y2 * y2).astype(BF16), ones_ref[...], preferred_element_type=F32)[:, :LANES]
            rinv = lax.rsqrt(ms + EPS)
            ar = y2[:, :LANES] * rinv
            br = y2[:, LANES:] * rinv
            store(which, p * 2 * LANES, ar * ca - br * sb)
            store(which, p * 2 * LANES + LANES, br * cb + ar * sa)

    y = jnp.dot(h_ref[...], w_ref[2], preferred_element_type=F32)
    for hh in range(A_HEADS):
        store(2, hh * LANES, y[:, hh * LANES:(hh + 1) * LANES])


def _a_proj(x2d, nrm, w3, gains, tables, ones, batch, seq, dil, tm):
    T, D = x2d.shape
    spt = seq // tm
    rows = tm // dil
    return pl.pallas_call(
        functools.partial(_a_proj_kernel, dil=dil),
        grid=(T // tm,),
        in_specs=[
            pl.BlockSpec((tm, D), lambda i: (i, 0)),
            _resident(nrm.shape),
            _resident(w3.shape),
            _resident(gains.shape),
            pl.BlockSpec((2, tm, LANES), lambda i: (0, i % spt, 0)),
            _resident(ones.shape),
        ],
        out_specs=pl.BlockSpec((3, None, dil, rows, A_WIDTH), lambda i: (0, i // spt, 0, i % spt, 0)),
        out_shape=jax.ShapeDtypeStruct((3, batch, dil, seq // dil, A_WIDTH), BF16),
        scratch_shapes=[pltpu.VMEM((tm, D), BF16), pltpu.VMEM((D // LANES, tm, LANES), F32)],
        compiler_params=_params(("parallel",)),
        name=f"a_proj_d{dil}",
    )(x2d, nrm, w3, gains, tables, ones)


def _dil_attn_kernel(bound_ref, q_ref, k_ref, kp_ref, kn_ref, v_ref, vp_ref, vn_ref, hot_ref,
                     o_ref, st_ref, kx_ref, vx_ref, *, length, chunk, sub, half):
    i = pl.program_id(2)
    kx_ref[0:half] = kp_ref[...]
    kx_ref[half:half + chunk] = k_ref[...]
    kx_ref[half + chunk:] = kn_ref[...]
    vx_ref[0:half] = vp_ref[...]
    vx_ref[half:half + chunk] = v_ref[...]
    vx_ref[half + chunk:] = vn_ref[...]

    nk = sub + 2 * half
    qi = lax.broadcasted_iota(jnp.int32, (2 * sub, nk), 0) % sub
    kj = lax.broadcasted_iota(jnp.int32, (2 * sub, nk), 1)
    band = jnp.abs(qi + half - kj) <= half
    lane = lax.broadcasted_iota(jnp.int32, (sub, LANES), 1)
    lane2 = lax.broadcasted_iota(jnp.int32, (1, 2 * LANES), 1) % LANES
    first = (lane2 < LANES // 2).astype(BF16)
    second = (lane2 >= LANES // 2).astype(BF16)

    bound = bound_ref[0]

    def sub_block(j, use_bound):
        r0 = j * sub
        base = i * chunk + j * sub
        lo = half - base
        hi = length + half - base
        mask = band & (kj >= lo) & (kj < hi)
        bias = jnp.where(mask, -bound if use_bound else 0.0, NEG_FILL)
        stats = jnp.zeros((sub, LANES), F32)
        for p in range(A_HEADS // 2):
            cols = slice(p * 2 * LANES, (p + 1) * 2 * LANES)
            q2 = q_ref[pl.ds(r0, sub), cols]
            qq = jnp.concatenate([q2 * first, q2 * second], axis=0)
            s = lax.dot_general(qq, kx_ref[pl.ds(r0, nk), cols], NT_DIMS, preferred_element_type=F32) + bias
            if use_bound:
                eb = jnp.exp2(s).astype(BF16)
            else:
                ref = jnp.max(s, axis=-1, keepdims=True)
                e = jnp.exp2(s - ref)
                den = jnp.sum(e, axis=-1, keepdims=True)
                eb = e.astype(BF16)
                lse = (ref + jnp.log2(den)) * (1.0 / LOG2E)
            for t in range(2):
                h = 2 * p + t
                hc = slice(h * LANES, (h + 1) * LANES)
                rs = slice(t * sub, (t + 1) * sub)
                v = vx_ref[pl.ds(r0, nk), hc]
                if use_bound:
                    r = jnp.dot(eb[rs], jnp.concatenate([v, hot_ref[h]], axis=1), preferred_element_type=F32)
                    o_ref[pl.ds(r0, sub), hc] = r[:, :LANES].astype(BF16)
                    stats = stats + r[:, LANES:]
                else:
                    o = jnp.dot(eb[rs], v, preferred_element_type=F32) / den[rs]
                    o_ref[pl.ds(r0, sub), hc] = o.astype(BF16)
                    stats = jnp.where(lane == h, lse[rs], stats)
        if use_bound:
            l2 = jnp.log2(stats)
            stats = jnp.where(lane < A_HEADS, (bound + l2) * (1.0 / LOG2E), jnp.where(lane < 2 * A_HEADS, l2, 0.0))
        st_ref[pl.ds(r0, sub), :] = stats

    def run(use_bound):
        for j in range(chunk // sub):
            sub_block(j, use_bound)

    run(True)
    head_lanes = lax.broadcasted_iota(jnp.int32, st_ref.shape, 1) < A_HEADS
    lse_min = jnp.min(jnp.where(head_lanes, st_ref[...], jnp.inf), keepdims=True)
    trusted = jnp.min(lse_min * LOG2E - bound) >= math.log2(L_MIN)

    @pl.when(jnp.logical_not(trusted))
    def _():
        run(False)


def _dil_attn(qkv, bound, g, chunk=512, sub=128):
    window, dil = DIL_CONFIGS[g]
    half = window // (2 * dil)
    _, batch, _, length, _ = qkv.shape
    chunk = min(chunk, length)
    assert length % chunk == 0 and chunk % sub == 0 and chunk % half == 0
    cph = chunk // half
    last = length // half - 1

    def main(which):
        return pl.BlockSpec((None, None, None, chunk, A_WIDTH), lambda b, c, i: (which, b, c, i, 0))

    def prev(which):
        return pl.BlockSpec((None, None, None, half, A_WIDTH),
                            lambda b, c, i: (which, b, c, jnp.maximum(i * cph - 1, 0), 0))

    def nxt(which):
        return pl.BlockSpec((None, None, None, half, A_WIDTH),
                            lambda b, c, i: (which, b, c, jnp.minimum((i + 1) * cph, last), 0))

    col = jnp.arange(LANES)[None, None, :]
    head = jnp.arange(A_HEADS)[:, None, None]
    hot = jnp.broadcast_to((col == head) | (col == head + A_HEADS), (A_HEADS, sub + 2 * half, LANES)).astype(BF16)
    kern = functools.partial(_dil_attn_kernel, length=length, chunk=chunk, sub=sub, half=half)
    return pl.pallas_call(
        kern,
        grid=(batch, dil, length // chunk),
        in_specs=[pl.BlockSpec(memory_space=pltpu.SMEM), main(0), main(1), prev(1), nxt(1), main(2), prev(2), nxt(2),
                  _resident(hot.shape)],
        out_specs=[
            pl.BlockSpec((None, None, chunk, A_WIDTH), lambda b, c, i: (b, c, i, 0)),
            pl.BlockSpec((None, None, chunk, LANES), lambda b, c, i: (b, c, i, 0)),
        ],
        out_shape=[
            jax.ShapeDtypeStruct((batch, dil, length, A_WIDTH), BF16),
            jax.ShapeDtypeStruct((batch, dil, length, LANES), F32),
        ],
        scratch_shapes=[pltpu.VMEM((chunk + 2 * half, A_WIDTH), BF16),
                        pltpu.VMEM((chunk + 2 * half, A_WIDTH), BF16)],
        compiler_params=_params(("parallel", "parallel", "parallel")),
        name=f"dil_attn_g{g}",
    )(bound, qkv, qkv, qkv, qkv, qkv, qkv, qkv, hot)


def _merge_tile(o_refs, l_refs, e_ref, os_ref, ls_ref):
    nlb = os_ref.shape[1]
    natural = [o_refs[g].shape[0] == 1 for g in range(N_GROUPS)]
    for g in range(N_GROUPS):
        if natural[g]:
            continue
        dil, rows = o_refs[g].shape[0], o_refs[g].shape[1]
        for c in range(dil):
            oc = o_refs[g][c].astype(F32)
            for kb in range(nlb):
                os_ref[g, kb, pl.ds(c, rows, stride=dil), :] = oc[:, kb * LANES:(kb + 1) * LANES]
            ls_ref[g, pl.ds(c, rows, stride=dil), :] = l_refs[g][c]
    stats = [l_refs[g][0] if natural[g] else ls_ref[g] for g in range(N_GROUPS)]
    mx = jnp.maximum(jnp.maximum(stats[0], stats[1]), stats[2])
    e = [jnp.exp(st - mx) for st in stats]
    inv = 1.0 / (e[0] + e[1] + e[2])
    acc = None
    for g in range(N_GROUPS):
        pending = jnp.exp2(-pltpu.roll(stats[g], LANES - A_HEADS, 1))
        w = jnp.dot((e[g] * inv * pending).astype(BF16), e_ref[...], preferred_element_type=F32)
        o_nat = (o_refs[g][0].astype(F32) if natural[g]
                 else jnp.concatenate([os_ref[g, kb] for kb in range(nlb)], axis=-1))
        term = w * o_nat
        acc = term if acc is None else acc + term
    return acc.astype(BF16)


def _ffn_tile(x, a, wo_ref, nrm_ref, w1_ref, w2_ref, ff_chunk):
    x1 = x + jnp.dot(a, wo_ref[...], preferred_element_type=F32)
    h = (x1 * _rms_scale(x1, x1.shape[-1]) * nrm_ref[...]).astype(BF16)
    acc = x1
    for c in range(w1_ref.shape[1] // ff_chunk):
        cols = slice(c * ff_chunk, (c + 1) * ff_chunk)
        a = jnp.maximum(jnp.dot(h, w1_ref[:, cols], preferred_element_type=F32), 0.0)
        acc = acc + jnp.dot((a * a).astype(BF16), w2_ref[cols, :], preferred_element_type=F32)
    return acc


def _ffn_kernel(x_ref, a_ref, wo_ref, nrm_ref, w1_ref, w2_ref, out_ref, *, ff_chunk):
    out_ref[...] = _ffn_tile(x_ref[...], a_ref[...], wo_ref, nrm_ref, w1_ref, w2_ref, ff_chunk)


def _ffn(x2d, a2d, w_o, nrm, w1, w2, tm, ff_chunk=1024):
    T, D = x2d.shape
    return pl.pallas_call(
        functools.partial(_ffn_kernel, ff_chunk=ff_chunk),
        grid=(T // tm,),
        in_specs=[
            pl.BlockSpec((tm, D), lambda i: (i, 0)),
            pl.BlockSpec((tm, a2d.shape[1]), lambda i: (i, 0)),
            _resident(w_o.shape),
            _resident(nrm.shape),
            _resident(w1.shape),
            _resident(w2.shape),
        ],
        out_specs=pl.BlockSpec((tm, D), lambda i: (i, 0)),
        out_shape=jax.ShapeDtypeStruct((T, D), F32),
        compiler_params=_params(("parallel",)),
        name="outproj_ffn",
    )(x2d, a2d, w_o, nrm, w1, w2)


def _merge_kernel(o0_ref, o1_ref, o2_ref, l0_ref, l1_ref, l2_ref, e_ref, out_ref, os_ref, ls_ref):
    out_ref[...] = _merge_tile((o0_ref, o1_ref, o2_ref), (l0_ref, l1_ref, l2_ref), e_ref, os_ref, ls_ref)


def _merge(outs, stats, seq, tm):
    batch = outs[0].shape[0]
    spt = seq // tm
    expand = (jnp.arange(LANES)[:, None] == (jnp.arange(A_WIDTH)[None, :] // A_HEAD_DIM)).astype(BF16)

    def spec(arr):
        dil, width = arr.shape[1], arr.shape[3]
        return pl.BlockSpec((None, dil, tm // dil, width), lambda i: (i // spt, 0, i % spt, 0))

    return pl.pallas_call(
        _merge_kernel,
        grid=(batch * spt,),
        in_specs=[spec(o) for o in outs] + [spec(l) for l in stats] + [_resident(expand.shape)],
        out_specs=pl.BlockSpec((tm, A_WIDTH), lambda i: (i, 0)),
        out_shape=jax.ShapeDtypeStruct((batch * seq, A_WIDTH), BF16),
        scratch_shapes=[pltpu.VMEM((N_GROUPS, A_WIDTH // LANES, tm, LANES), F32),
                        pltpu.VMEM((N_GROUPS, tm, LANES), F32)],
        compiler_params=_params(("parallel",)),
        name="a_merge",
    )(*outs, *stats, expand)


def _b_proj_kernel(x_ref, nrm_ref, win_ref, qag_ref, kvag_ref, wqb_ref, wkn_ref, wv_ref,
                   gq_ref, gk_ref, cq_ref, ck_ref, mq_ref, mk_ref, cosq_ref, cosk_ref, sin_ref,
                   q_ref, k_ref, v_ref):
    x = x_ref[...]
    h = (x * _rms_scale(x, x.shape[-1]) * nrm_ref[...]).astype(BF16)
    lat = jnp.dot(h, win_ref[...], preferred_element_type=F32)
    c_q = lat[:, :Q_LORA]
    c_kv = lat[:, Q_LORA:Q_LORA + KV_LORA]
    k_rope = lat[:, Q_LORA + KV_LORA:Q_LORA + KV_LORA + LANES]
    k_roll = lat[:, Q_LORA + KV_LORA + LANES:]
    cqn = (c_q * _rms_scale(c_q, Q_LORA) * qag_ref[...]).astype(BF16)
    ckvn = (c_kv * _rms_scale(c_kv, KV_LORA) * kvag_ref[...]).astype(BF16)
    sin = sin_ref[...]

    vt_all = lax.dot_general(wv_ref[...], ckvn, NT_DIMS, preferred_element_type=F32)
    ones_rows = (lax.broadcasted_iota(jnp.int32, (VT_ROWS - V_DIM, x.shape[0]), 0) == 0).astype(BF16)
    for hh in range(B_HEADS):
        v_ref[hh * VT_ROWS:hh * VT_ROWS + V_DIM, :] = vt_all[hh * V_DIM:(hh + 1) * V_DIM, :].astype(BF16)
        v_ref[hh * VT_ROWS + V_DIM:(hh + 1) * VT_ROWS, :] = ones_rows

    q3 = jnp.dot(cqn, wqb_ref[...], preferred_element_type=F32)
    gq_n = gq_ref[:, :LANES]
    cos_q = cosq_ref[...]
    for hh in range(B_HEADS):
        q2 = q3[:, hh * Q3_PAD:hh * Q3_PAD + QK_PAD]
        ms = jnp.dot((q2 * q2).astype(BF16), mq_ref[...], preferred_element_type=F32)[:, :LANES]
        rq = lax.rsqrt(ms * (1.0 / QK_DIM) + EPS)
        qr = q2[:, LANES:] * cos_q + q3[:, hh * Q3_PAD + QK_PAD:(hh + 1) * Q3_PAD] * sin
        q_ref[:, hh * QK_PAD:hh * QK_PAD + LANES] = (q2[:, :LANES] * rq * gq_n).astype(BF16)
        q_ref[:, hh * QK_PAD + LANES:(hh + 1) * QK_PAD] = (qr * rq + cq_ref[...]).astype(BF16)

    kn_all = jnp.dot(ckvn, wkn_ref[...], preferred_element_type=F32)
    ss_kr = jnp.sum(k_rope * k_rope, axis=-1, keepdims=True)
    kr = k_rope * cosk_ref[...] + k_roll * sin
    gk_n = gk_ref[:, :LANES]
    for p in range(B_HEADS // 2):
        kn2 = kn_all[:, p * 2 * LANES:(p + 1) * 2 * LANES]
        ms = jnp.dot((kn2 * kn2).astype(BF16), mk_ref[...], preferred_element_type=F32)
        rk2 = lax.rsqrt((ms + ss_kr) * (1.0 / QK_DIM) + EPS)
        for t in range(2):
            hh = 2 * p + t
            rk = rk2[:, t * LANES:(t + 1) * LANES]
            k_ref[:, hh * QK_PAD:hh * QK_PAD + LANES] = (kn2[:, t * LANES:(t + 1) * LANES] * rk * gk_n).astype(BF16)
            k_ref[:, hh * QK_PAD + LANES:(hh + 1) * QK_PAD] = (kr * rk + ck_ref[...]).astype(BF16)


def _b_proj(x2d, consts, tables, seq, tm):
    T, D = x2d.shape
    spt = seq // tm
    row = lambda i: (i, 0)
    pos = pl.BlockSpec((tm, LANES), lambda i: (i % spt, 0))
    return pl.pallas_call(
        _b_proj_kernel,
        grid=(T // tm,),
        in_specs=[pl.BlockSpec((tm, D), row)] + [_resident(c.shape) for c in consts] + [pos] * len(tables),
        out_specs=[
            pl.BlockSpec((tm, B_HEADS * QK_PAD), row),
            pl.BlockSpec((tm, B_HEADS * QK_PAD), row),
            pl.BlockSpec((B_HEADS * VT_ROWS, tm), lambda i: (0, i)),
        ],
        out_shape=[
            jax.ShapeDtypeStruct((T, B_HEADS * QK_PAD), BF16),
            jax.ShapeDtypeStruct((T, B_HEADS * QK_PAD), BF16),
            jax.ShapeDtypeStruct((B_HEADS * VT_ROWS, T), BF16),
        ],
        compiler_params=_params(("parallel",)),
        name="b_proj",
    )(x2d, *consts, *tables)


def _mla_attn_kernel(q_ref, k_ref, vt_ref, o_ref, acc_ref, *, tk, unroll):
    q = q_ref[...]
    tq = q.shape[0]
    nkv = k_ref.shape[0] // tk

    def scores_t(kb):
        r0 = pl.multiple_of(kb * tk, tk)
        st = lax.dot_general(k_ref[pl.ds(r0, tk), :], q, NT_DIMS, preferred_element_type=F32)
        return st, vt_ref[:, pl.ds(r0, tk)]

    def fast(kb, carry):
        for u in range(unroll):
            st, vt = scores_t(kb * unroll + u)
            acc_ref[...] += jnp.dot(vt, jnp.exp2(st).astype(BF16), preferred_element_type=F32)
        return carry

    def finish():
        inv = 1.0 / acc_ref[V_DIM:V_DIM + 1, :]
        o_ref[...] = (acc_ref[:V_DIM, :] * inv).T.astype(BF16)

    acc_ref[...] = jnp.zeros_like(acc_ref)
    lax.fori_loop(0, nkv // unroll, fast, 0)
    trusted = jnp.min(acc_ref[V_DIM:V_DIM + 1, :]) >= L_MIN

    @pl.when(trusted)
    def _():
        finish()

    @pl.when(jnp.logical_not(trusted))
    def _():
        def slow(kb, m):
            st, vt = scores_t(kb)
            m_new = jnp.maximum(m, jnp.max(st, axis=0, keepdims=True))
            pt = jnp.exp2(st - m_new).astype(BF16)
            acc_ref[...] = jnp.exp2(m - m_new) * acc_ref[...] + jnp.dot(vt, pt, preferred_element_type=F32)
            return m_new

        acc_ref[...] = jnp.zeros_like(acc_ref)
        lax.fori_loop(0, nkv, slow, jnp.full((1, tq), -jnp.inf, F32))
        finish()


def _mla_attn(q, k, vt, batch, seq, tq=2048, tk=512, unroll=16):
    nq = seq // tq
    unroll = min(unroll, seq // tk)
    assert seq % (tk * unroll) == 0
    return pl.pallas_call(
        functools.partial(_mla_attn_kernel, tk=tk, unroll=unroll),
        grid=(batch, B_HEADS, nq),
        in_specs=[
            pl.BlockSpec((tq, QK_PAD), lambda b, h, i: (b * nq + i, h)),
            pl.BlockSpec((seq, QK_PAD), lambda b, h, i: (b, h)),
            pl.BlockSpec((VT_ROWS, seq), lambda b, h, i: (h, b)),
        ],
        out_specs=pl.BlockSpec((tq, V_DIM), lambda b, h, i: (b * nq + i, h)),
        out_shape=jax.ShapeDtypeStruct((batch * seq, B_HEADS * V_DIM), BF16),
        scratch_shapes=[pltpu.VMEM((VT_ROWS, tq), F32)],
        compiler_params=_params(("parallel", "parallel", "parallel")),
        name="mla_attn",
    )(q, k, vt)


def _rope_angles(seq, d):
    pos = jnp.arange(seq, dtype=F32)
    freqs = ROPE_THETA ** (-jnp.arange(0, d, 2, dtype=F32) / d)
    ang = pos[:, None] * freqs[None, :]
    return jnp.cos(ang), jnp.sin(ang)


def _spread_rope(t, axis):
    a, b = jnp.split(t, 2, axis=axis)
    z = jnp.zeros_like(a)
    return jnp.concatenate([a, z, b, z], axis=axis)


def _pair_columns(w):
    d = w.shape[0]
    w = w.reshape(d, A_HEADS // 2, 2, 2, A_HEAD_DIM // 2)
    return w.transpose(0, 1, 3, 2, 4).reshape(d, A_WIDTH)


def _residue_major(table, tm, dil):
    s, w = table.shape
    return table.reshape(s // tm, tm // dil, dil, w).transpose(0, 2, 1, 3).reshape(s, w)


def kernel(x, norm_mix, norm_ffn, a_w_qkv, a_q_gain, a_k_gain, a_w_o, b_w_in, b_q_a_gain, b_w_qb,
           b_kv_a_gain, b_w_kvb, b_q_gain, b_k_gain, b_w_o, ffn_w1, ffn_w2):
    batch, seq, d_model = x.shape
    T = batch * seq
    x2d = x.reshape(T, d_model)

    tm_a = 512
    cos, sin = _rope_angles(seq, A_HEAD_DIM)
    cos_a = jnp.concatenate([cos, cos], -1)
    sin_a = jnp.concatenate([sin, sin], -1)
    pair_head = (jnp.arange(2 * LANES) % LANES) // (LANES // 2)
    mean_ones = (pair_head[:, None] == pair_head[None, :]).astype(BF16) * (1.0 / A_HEAD_DIM)
    w_qkv = a_w_qkv[0].reshape(d_model, 3, N_GROUPS, A_WIDTH)
    hd = A_HEAD_DIM // 2
    outs, lses = [], []
    for g, (_, dil) in enumerate(DIL_CONFIGS):
        w3 = jnp.stack([_pair_columns(w_qkv[:, 0, g]), _pair_columns(w_qkv[:, 1, g]), w_qkv[:, 2, g]]).astype(BF16)
        tables = jnp.stack([_residue_major(cos_a, tm_a, dil), _residue_major(sin_a, tm_a, dil)])

        def pair_gain(gain):
            return jnp.stack([jnp.tile(gain[:hd], 2), jnp.tile(gain[hd:], 2)])

        q_scale = LOG2E / math.sqrt(A_HEAD_DIM)
        gains = jnp.stack([pair_gain(a_q_gain[0, g] * q_scale), pair_gain(a_k_gain[0, g])])
        bound = 1.02 * A_HEAD_DIM * q_scale * jnp.max(jnp.abs(a_q_gain[0, g])) * jnp.max(jnp.abs(a_k_gain[0, g]))
        qkv = _a_proj(x2d, norm_mix[0][None, :], w3, gains, tables, mean_ones, batch, seq, dil, tm_a)
        o, lse = _dil_attn(qkv, jnp.reshape(bound, (1,)).astype(F32), g)
        outs.append(o)
        lses.append(lse)
    merged = _merge(outs, lses, seq, tm=512)
    x2d = _ffn(x2d, merged, a_w_o[0].astype(BF16), norm_ffn[0][None, :],
               ffn_w1[0].astype(BF16), ffn_w2[0].astype(BF16), tm=512)

    cos, sin = _rope_angles(seq, ROPE_DIM)
    cos_b = _spread_rope(jnp.concatenate([cos, cos], -1), -1)
    sin_b = _spread_rope(jnp.concatenate([-sin, sin], -1), -1)

    def head_gain(gain, scale):
        return (jnp.concatenate([gain[:NOPE_DIM], _spread_rope(gain[NOPE_DIM:], 0)]) * scale)[None, :]

    q_scale = LOG2E / math.sqrt(QK_DIM)
    gq = head_gain(b_q_gain[0], q_scale)
    gk = head_gain(b_k_gain[0], 1.0)
    bound = 1.02 * QK_DIM * q_scale * jnp.max(jnp.abs(b_q_gain[0])) * jnp.max(jnp.abs(b_k_gain[0]))
    const_lane = (jnp.arange(LANES) == CONST_LANE).astype(F32)[None, :]

    def half_roll(w, gain):
        return jnp.roll(w * gain, LANES // 2, axis=-1)

    w_in = b_w_in[0]
    k_rope_w = _spread_rope(w_in[:, Q_LORA + KV_LORA:], 1)
    w_in = jnp.concatenate([w_in[:, :Q_LORA + KV_LORA], k_rope_w, half_roll(k_rope_w, gk[:, LANES:])], 1)
    w_qb = b_w_qb[0].reshape(Q_LORA, B_HEADS, QK_DIM)
    q_rope_w = _spread_rope(w_qb[..., NOPE_DIM:], 2)
    w_qb = jnp.concatenate([w_qb[..., :NOPE_DIM], q_rope_w, half_roll(q_rope_w, gq[:, LANES:])], -1)
    w_qb = w_qb.reshape(Q_LORA, B_HEADS * Q3_PAD)
    w_kvb = b_w_kvb[0].reshape(KV_LORA, B_HEADS, NOPE_DIM + V_DIM)
    w_kn = w_kvb[..., :NOPE_DIM].reshape(KV_LORA, B_HEADS * NOPE_DIM)
    w_v = w_kvb[..., NOPE_DIM:].reshape(KV_LORA, B_HEADS * V_DIM).T
    sum_q = jnp.ones((QK_PAD, QK_PAD), BF16)
    sum_k = ((jnp.arange(2 * LANES)[:, None] // LANES) == (jnp.arange(2 * LANES)[None, :] // LANES)).astype(BF16)

    consts = (norm_mix[1][None, :], w_in.astype(BF16), b_q_a_gain[0][None, :], b_kv_a_gain[0][None, :],
              w_qb.astype(BF16), w_kn.astype(BF16), w_v.astype(BF16), gq, gk,
              -bound * const_lane, const_lane, sum_q, sum_k)
    tables = (cos_b * gq[:, LANES:], cos_b * gk[:, LANES:], sin_b)
    q, k, v = _b_proj(x2d, consts, tables, seq, tm=512)
    o = _mla_attn(q, k, v, batch, seq)
    x2d = _ffn(x2d, o, b_w_o[0].astype(BF16), norm_ffn[1][None, :],
               ffn_w1[1].astype(BF16), ffn_w2[1].astype(BF16), tm=512)
    return x2d.reshape(batch, seq, d_model)
```

```python
import functools
import math

import jax
import jax.numpy as jnp
from jax import lax
from jax.experimental import pallas as pl
from jax.experimental.pallas import tpu as pltpu

EPS = 1e-6
ROPE_THETA = 10000.0
NEG_FILL = -1e30
LOG2E = math.log2(math.e)

DIL_CONFIGS = ((128, 1), (512, 4), (2048, 16))
N_GROUPS = len(DIL_CONFIGS)
A_HEADS = 8
A_HEAD_DIM = 128
A_WIDTH = A_HEADS * A_HEAD_DIM

B_HEADS = 8
Q_LORA = 256
KV_LORA = 128
NOPE_DIM = 128
ROPE_DIM = 64
V_DIM = 128
QK_DIM = NOPE_DIM + ROPE_DIM
QK_PAD = 256
Q3_PAD = 384
VT_ROWS = 144
CONST_LANE = 32
L_MIN = 2.0 ** -80

LANES = 128
VMEM_LIMIT = 48 * 1024 * 1024

BF16 = jnp.bfloat16
F32 = jnp.float32
NT_DIMS = (((1,), (1,)), ((), ()))


def _params(sem):
    return pltpu.CompilerParams(dimension_semantics=sem, vmem_limit_bytes=VMEM_LIMIT)


def _resident(shape):
    zeros = (0,) * len(shape)
    return pl.BlockSpec(shape, lambda *_: zeros, pipeline_mode=pl.Buffered(1))


def _rms_scale(x, width):
    return lax.rsqrt(jnp.sum(x * x, axis=-1, keepdims=True) * (1.0 / width) + EPS)


def _a_proj_kernel(x_ref, nrm_ref, w_ref, gain_ref, tab_ref, o_ref, h_ref, hs_ref, *, dil):
    tm = x_ref.shape[0]
    rows = tm // dil
    x = x_ref[...]
    hn = x * _rms_scale(x, x.shape[-1]) * nrm_ref[...]
    if dil == 1:
        h_ref[...] = hn.astype(BF16)
    else:
        for kb in range(hs_ref.shape[0]):
            hs_ref[kb] = hn[:, kb * LANES:(kb + 1) * LANES]
        for c in range(dil):
            for kb in range(hs_ref.shape[0]):
                h_ref[c * rows:(c + 1) * rows, kb * LANES:(kb + 1) * LANES] = (
                    hs_ref[kb, pl.ds(c, rows, stride=dil), :].astype(BF16))

    def store(which, col0, val):
        for c in range(dil):
            o_ref[which, c, :, col0:col0 + LANES] = val[c * rows:(c + 1) * rows].astype(BF16)

    for which in range(2):
        y = jnp.dot(h_ref[...], w_ref[which], preferred_element_type=F32)
        ga, gb = gain_ref[which, 0:1, :], gain_ref[which, 1:2, :]
        cos, sin = tab_ref[0], tab_ref[1]
        ca, sb, cb, sa = ga * cos, gb * sin, gb * cos, ga * sin
        for p in range(A_HEADS // 2):
            y2 = y[:, p * 2 * LANES:(p + 1) * 2 * LANES]
            t = y2[:, :LANES] * y2[:, :LANES] + y2[:, LANES:] * y2[:, LANES:]
            lo_lane = lax.broadcasted_iota(jnp.int32, t.shape, 1) < LANES // 2
            s_lo = jnp.sum(jnp.where(lo_lane, t, 0.0), axis=-1, keepdims=True)
            s_all = jnp.sum(t, axis=-1, keepdims=True)
            ms = jnp.where(lo_lane, s_lo, s_all - s_lo) * (1.0 / A_HEAD_DIM)
            rinv = lax.rsqrt(ms + EPS)
            ar = y2[:, :LANES] * rinv
            br = y2[:, LANES:] * rinv
            store(which, p * 2 * LANES, ar * ca - br * sb)
            store(which, p * 2 * LANES + LANES, br * cb + ar * sa)

    y = jnp.dot(h_ref[...], w_ref[2], preferred_element_type=F32)
    for hh in range(A_HEADS):
        store(2, hh * LANES, y[:, hh * LANES:(hh + 1) * LANES])


def _a_proj(x2d, nrm, w3, gains, tables, batch, seq, dil, tm):
    T, D = x2d.shape
    spt = seq // tm
    rows = tm // dil
    return pl.pallas_call(
        functools.partial(_a_proj_kernel, dil=dil),
        grid=(T // tm,),
        in_specs=[
            pl.BlockSpec((tm, D), lambda i: (i, 0)),
            _resident(nrm.shape),
            _resident(w3.shape),
            _resident(gains.shape),
            pl.BlockSpec((2, tm, LANES), lambda i: (0, i % spt, 0)),
        ],
        out_specs=pl.BlockSpec((3, None, dil, rows, A_WIDTH), lambda i: (0, i // spt, 0, i % spt, 0)),
        out_shape=jax.ShapeDtypeStruct((3, batch, dil, seq // dil, A_WIDTH), BF16),
        scratch_shapes=[pltpu.VMEM((tm, D), BF16), pltpu.VMEM((D // LANES, tm, LANES), F32)],
        compiler_params=_params(("parallel",)),
        name=f"a_proj_d{dil}",
    )(x2d, nrm, w3, gains, tables)


def _dil_attn_kernel(bound_ref, q_ref, k_ref, kp_ref, kn_ref, v_ref, vp_ref, vn_ref, hot_ref,
                     o_ref, st_ref, *, length, chunk, sub, half):
    i = pl.program_id(2)
    nk = sub + 2 * half

    def window(ref, prev_ref, next_ref, j, cols):
        lo, hi = j * sub - half, (j + 1) * sub + half
        parts = [prev_ref[:, cols]] if lo < 0 else []
        parts.append(ref[max(lo, 0):min(hi, chunk), cols])
        if hi > chunk:
            parts.append(next_ref[:, cols])
        return parts[0] if len(parts) == 1 else jnp.concatenate(parts, axis=0)

    qi = lax.broadcasted_iota(jnp.int32, (2 * sub, nk), 0) % sub
    kj = lax.broadcasted_iota(jnp.int32, (2 * sub, nk), 1)
    band = jnp.abs(qi + half - kj) <= half
    lane = lax.broadcasted_iota(jnp.int32, (sub, LANES), 1)
    lane2 = lax.broadcasted_iota(jnp.int32, (1, 2 * LANES), 1) % LANES
    first = (lane2 < LANES // 2).astype(BF16)
    second = (lane2 >= LANES // 2).astype(BF16)

    bound = bound_ref[0]

    def sub_block(j, use_bound):
        r0 = j * sub
        base = i * chunk + j * sub
        lo = half - base
        hi = length + half - base
        mask = band & (kj >= lo) & (kj < hi)
        bias = jnp.where(mask, -bound if use_bound else 0.0, NEG_FILL)
        stats = jnp.zeros((sub, LANES), F32)
        for p in range(A_HEADS // 2):
            cols = slice(p * 2 * LANES, (p + 1) * 2 * LANES)
            q2 = q_ref[pl.ds(r0, sub), cols]
            qq = jnp.concatenate([q2 * first, q2 * second], axis=0)
            s = lax.dot_general(qq, window(k_ref, kp_ref, kn_ref, j, cols), NT_DIMS,
                                preferred_element_type=F32) + bias
            if use_bound:
                eb = jnp.exp2(s).astype(BF16)
            else:
                ref = jnp.max(s, axis=-1, keepdims=True)
                e = jnp.exp2(s - ref)
                den = jnp.sum(e, axis=-1, keepdims=True)
                eb = e.astype(BF16)
                lse = (ref + jnp.log2(den)) * (1.0 / LOG2E)
            for t in range(2):
                h = 2 * p + t
                hc = slice(h * LANES, (h + 1) * LANES)
                rs = slice(t * sub, (t + 1) * sub)
                v = window(v_ref, vp_ref, vn_ref, j, hc)
                if use_bound:
                    r = jnp.dot(eb[rs], jnp.concatenate([v, hot_ref[h]], axis=1), preferred_element_type=F32)
                    o_ref[pl.ds(r0, sub), hc] = r[:, :LANES].astype(BF16)
                    stats = stats + r[:, LANES:]
                else:
                    o = jnp.dot(eb[rs], v, preferred_element_type=F32) / den[rs]
                    o_ref[pl.ds(r0, sub), hc] = o.astype(BF16)
                    stats = jnp.where(lane == h, lse[rs], stats)
        if use_bound:
            l2 = jnp.log2(stats)
            stats = jnp.where(lane < A_HEADS, (bound + l2) * (1.0 / LOG2E), jnp.where(lane < 2 * A_HEADS, l2, 0.0))
        st_ref[pl.ds(r0, sub), :] = stats

    def run(use_bound):
        for j in range(chunk // sub):
            sub_block(j, use_bound)

    run(True)
    head_lanes = lax.broadcasted_iota(jnp.int32, st_ref.shape, 1) < A_HEADS
    lse_min = jnp.min(jnp.where(head_lanes, st_ref[...], jnp.inf), keepdims=True)
    trusted = jnp.min(lse_min * LOG2E - bound) >= math.log2(L_MIN)

    @pl.when(jnp.logical_not(trusted))
    def _():
        run(False)


def _dil_attn(qkv, bound, g, chunk=512, sub=128):
    window, dil = DIL_CONFIGS[g]
    half = window // (2 * dil)
    _, batch, _, length, _ = qkv.shape
    chunk = min(chunk, length)
    assert length % chunk == 0 and chunk % sub == 0 and chunk % half == 0
    cph = chunk // half
    last = length // half - 1

    def main(which):
        return pl.BlockSpec((None, None, None, chunk, A_WIDTH), lambda b, c, i: (which, b, c, i, 0))

    def prev(which):
        return pl.BlockSpec((None, None, None, half, A_WIDTH),
                            lambda b, c, i: (which, b, c, jnp.maximum(i * cph - 1, 0), 0))

    def nxt(which):
        return pl.BlockSpec((None, None, None, half, A_WIDTH),
                            lambda b, c, i: (which, b, c, jnp.minimum((i + 1) * cph, last), 0))

    col = jnp.arange(LANES)[None, None, :]
    head = jnp.arange(A_HEADS)[:, None, None]
    hot = jnp.broadcast_to((col == head) | (col == head + A_HEADS), (A_HEADS, sub + 2 * half, LANES)).astype(BF16)
    kern = functools.partial(_dil_attn_kernel, length=length, chunk=chunk, sub=sub, half=half)
    return pl.pallas_call(
        kern,
        grid=(batch, dil, length // chunk),
        in_specs=[pl.BlockSpec(memory_space=pltpu.SMEM), main(0), main(1), prev(1), nxt(1), main(2), prev(2), nxt(2),
                  _resident(hot.shape)],
        out_specs=[
            pl.BlockSpec((None, None, chunk, A_WIDTH), lambda b, c, i: (b, c, i, 0)),
            pl.BlockSpec((None, None, chunk, LANES), lambda b, c, i: (b, c, i, 0)),
        ],
        out_shape=[
            jax.ShapeDtypeStruct((batch, dil, length, A_WIDTH), BF16),
            jax.ShapeDtypeStruct((batch, dil, length, LANES), F32),
        ],
        compiler_params=_params(("parallel", "parallel", "parallel")),
        name=f"dil_attn_g{g}",
    )(bound, qkv, qkv, qkv, qkv, qkv, qkv, qkv, hot)


def _merge_tile(o_refs, l_refs, e_ref, os_ref, ls_ref):
    nlb = os_ref.shape[1]
    natural = [o_refs[g].shape[0] == 1 for g in range(N_GROUPS)]
    for g in range(N_GROUPS):
        if natural[g]:
            continue
        dil, rows = o_refs[g].shape[0], o_refs[g].shape[1]
        for c in range(dil):
            oc = o_refs[g][c].astype(F32)
            for kb in range(nlb):
                os_ref[g, kb, pl.ds(c, rows, stride=dil), :] = oc[:, kb * LANES:(kb + 1) * LANES]
            ls_ref[g, pl.ds(c, rows, stride=dil), :] = l_refs[g][c]
    stats = [l_refs[g][0] if natural[g] else ls_ref[g] for g in range(N_GROUPS)]
    mx = jnp.maximum(jnp.maximum(stats[0], stats[1]), stats[2])
    e = [jnp.exp(st - mx) for st in stats]
    inv = 1.0 / (e[0] + e[1] + e[2])
    acc = None
    for g in range(N_GROUPS):
        pending = jnp.exp2(-pltpu.roll(stats[g], LANES - A_HEADS, 1))
        w = jnp.dot((e[g] * inv * pending).astype(BF16), e_ref[...], preferred_element_type=F32)
        o_nat = (o_refs[g][0].astype(F32) if natural[g]
                 else jnp.concatenate([os_ref[g, kb] for kb in range(nlb)], axis=-1))
        term = w * o_nat
        acc = term if acc is None else acc + term
    return acc.astype(BF16)


def _ffn_tile(x, a, wo_ref, nrm_ref, w1_ref, w2_ref, ff_chunk):
    x1 = x + jnp.dot(a, wo_ref[...], preferred_element_type=F32)
    h = (x1 * _rms_scale(x1, x1.shape[-1]) * nrm_ref[...]).astype(BF16)
    acc = x1
    for c in range(w1_ref.shape[1] // ff_chunk):
        cols = slice(c * ff_chunk, (c + 1) * ff_chunk)
        a = jnp.maximum(jnp.dot(h, w1_ref[:, cols], preferred_element_type=F32), 0.0)
        acc = acc + jnp.dot((a * a).astype(BF16), w2_ref[cols, :], preferred_element_type=F32)
    return acc


def _ffn_kernel(x_ref, a_ref, wo_ref, nrm_ref, w1_ref, w2_ref, out_ref, *, ff_chunk):
    out_ref[...] = _ffn_tile(x_ref[...], a_ref[...], wo_ref, nrm_ref, w1_ref, w2_ref, ff_chunk)


def _ffn(x2d, a2d, w_o, nrm, w1, w2, tm, ff_chunk=1024):
    T, D = x2d.shape
    return pl.pallas_call(
        functools.partial(_ffn_kernel, ff_chunk=ff_chunk),
        grid=(T // tm,),
        in_specs=[
            pl.BlockSpec((tm, D), lambda i: (i, 0)),
            pl.BlockSpec((tm, a2d.shape[1]), lambda i: (i, 0)),
            _resident(w_o.shape),
            _resident(nrm.shape),
            _resident(w1.shape),
            _resident(w2.shape),
        ],
        out_specs=pl.BlockSpec((tm, D), lambda i: (i, 0)),
        out_shape=jax.ShapeDtypeStruct((T, D), F32),
        compiler_params=_params(("parallel",)),
        name="outproj_ffn",
    )(x2d, a2d, w_o, nrm, w1, w2)


def _merge_kernel(o0_ref, o1_ref, o2_ref, l0_ref, l1_ref, l2_ref, e_ref, out_ref, os_ref, ls_ref):
    out_ref[...] = _merge_tile((o0_ref, o1_ref, o2_ref), (l0_ref, l1_ref, l2_ref), e_ref, os_ref, ls_ref)


def _merge(outs, stats, seq, tm):
    batch = outs[0].shape[0]
    spt = seq // tm
    expand = (jnp.arange(LANES)[:, None] == (jnp.arange(A_WIDTH)[None, :] // A_HEAD_DIM)).astype(BF16)

    def spec(arr):
        dil, width = arr.shape[1], arr.shape[3]
        return pl.BlockSpec((None, dil, tm // dil, width), lambda i: (i // spt, 0, i % spt, 0))

    return pl.pallas_call(
        _merge_kernel,
        grid=(batch * spt,),
        in_specs=[spec(o) for o in outs] + [spec(l) for l in stats] + [_resident(expand.shape)],
        out_specs=pl.BlockSpec((tm, A_WIDTH), lambda i: (i, 0)),
        out_shape=jax.ShapeDtypeStruct((batch * seq, A_WIDTH), BF16),
        scratch_shapes=[pltpu.VMEM((N_GROUPS, A_WIDTH // LANES, tm, LANES), F32),
                        pltpu.VMEM((N_GROUPS, tm, LANES), F32)],
        compiler_params=_params(("parallel",)),
        name="a_merge",
    )(*outs, *stats, expand)


def _b_proj_kernel(x_ref, nrm_ref, win_ref, qag_ref, kvag_ref, wqb_ref, wkn_ref, wv_ref,
                   gq_ref, gk_ref, cq_ref, ck_ref, cosq_ref, cosk_ref, sin_ref,
                   q_ref, k_ref, v_ref):
    x = x_ref[...]
    h = (x * _rms_scale(x, x.shape[-1]) * nrm_ref[...]).astype(BF16)
    lat = jnp.dot(h, win_ref[...], preferred_element_type=F32)
    c_q = lat[:, :Q_LORA]
    c_kv = lat[:, Q_LORA:Q_LORA + KV_LORA]
    k_rope = lat[:, Q_LORA + KV_LORA:Q_LORA + KV_LORA + LANES]
    k_roll = lat[:, Q_LORA + KV_LORA + LANES:]
    cqn = (c_q * _rms_scale(c_q, Q_LORA) * qag_ref[...]).astype(BF16)
    ckvn = (c_kv * _rms_scale(c_kv, KV_LORA) * kvag_ref[...]).astype(BF16)
    sin = sin_ref[...]

    vt_all = lax.dot_general(wv_ref[...], ckvn, NT_DIMS, preferred_element_type=F32)
    ones_rows = (lax.broadcasted_iota(jnp.int32, (VT_ROWS - V_DIM, x.shape[0]), 0) == 0).astype(BF16)
    for hh in range(B_HEADS):
        v_ref[hh * VT_ROWS:hh * VT_ROWS + V_DIM, :] = vt_all[hh * V_DIM:(hh + 1) * V_DIM, :].astype(BF16)
        v_ref[hh * VT_ROWS + V_DIM:(hh + 1) * VT_ROWS, :] = ones_rows

    q3 = jnp.dot(cqn, wqb_ref[...], preferred_element_type=F32)
    gq_n = gq_ref[:, :LANES]
    cos_q = cosq_ref[...]
    for hh in range(B_HEADS):
        qn = q3[:, hh * Q3_PAD:hh * Q3_PAD + LANES]
        qr = q3[:, hh * Q3_PAD + LANES:hh * Q3_PAD + QK_PAD]
        rq = lax.rsqrt(jnp.sum(qn * qn + qr * qr, axis=-1, keepdims=True) * (1.0 / QK_DIM) + EPS)
        qr = qr * cos_q + q3[:, hh * Q3_PAD + QK_PAD:(hh + 1) * Q3_PAD] * sin
        q_ref[:, hh * QK_PAD:hh * QK_PAD + LANES] = (qn * rq * gq_n).astype(BF16)
        q_ref[:, hh * QK_PAD + LANES:(hh + 1) * QK_PAD] = (qr * rq + cq_ref[...]).astype(BF16)

    kn_all = jnp.dot(ckvn, wkn_ref[...], preferred_element_type=F32)
    ss_kr = jnp.sum(k_rope * k_rope, axis=-1, keepdims=True)
    kr = k_rope * cosk_ref[...] + k_roll * sin
    gk_n = gk_ref[:, :LANES]
    for hh in range(B_HEADS):
        kn = kn_all[:, hh * LANES:(hh + 1) * LANES]
        rk = lax.rsqrt((jnp.sum(kn * kn, axis=-1, keepdims=True) + ss_kr) * (1.0 / QK_DIM) + EPS)
        k_ref[:, hh * QK_PAD:hh * QK_PAD + LANES] = (kn * rk * gk_n).astype(BF16)
        k_ref[:, hh * QK_PAD + LANES:(hh + 1) * QK_PAD] = (kr * rk + ck_ref[...]).astype(BF16)


def _b_proj(x2d, consts, tables, seq, tm):
    T, D = x2d.shape
    spt = seq // tm
    row = lambda i: (i, 0)
    pos = pl.BlockSpec((tm, LANES), lambda i: (i % spt, 0))
    return pl.pallas_call(
        _b_proj_kernel,
        grid=(T // tm,),
        in_specs=[pl.BlockSpec((tm, D), row)] + [_resident(c.shape) for c in consts] + [pos] * len(tables),
        out_specs=[
            pl.BlockSpec((tm, B_HEADS * QK_PAD), row),
            pl.BlockSpec((tm, B_HEADS * QK_PAD), row),
            pl.BlockSpec((B_HEADS * VT_ROWS, tm), lambda i: (0, i)),
        ],
        out_shape=[
            jax.ShapeDtypeStruct((T, B_HEADS * QK_PAD), BF16),
            jax.ShapeDtypeStruct((T, B_HEADS * QK_PAD), BF16),
            jax.ShapeDtypeStruct((B_HEADS * VT_ROWS, T), BF16),
        ],
        compiler_params=_params(("parallel",)),
        name="b_proj",
    )(x2d, *consts, *tables)


def _mla_attn_kernel(q_ref, k_ref, vt_ref, o_ref, acc_ref, *, tk, unroll):
    q = q_ref[...]
    tq = q.shape[0]
    nkv = k_ref.shape[0] // tk

    def scores_t(kb):
        r0 = pl.multiple_of(kb * tk, tk)
        st = lax.dot_general(k_ref[pl.ds(r0, tk), :], q, NT_DIMS, preferred_element_type=F32)
        return st, vt_ref[:, pl.ds(r0, tk)]

    def fast(kb, carry):
        for u in range(unroll):
            st, vt = scores_t(kb * unroll + u)
            acc_ref[...] += jnp.dot(vt, jnp.exp2(st).astype(BF16), preferred_element_type=F32)
        return carry

    def finish():
        inv = 1.0 / acc_ref[V_DIM:V_DIM + 1, :]
        o_ref[...] = (acc_ref[:V_DIM, :] * inv).T.astype(BF16)

    acc_ref[...] = jnp.zeros_like(acc_ref)
    lax.fori_loop(0, nkv // unroll, fast, 0)
    trusted = jnp.min(acc_ref[V_DIM:V_DIM + 1, :]) >= L_MIN

    @pl.when(trusted)
    def _():
        finish()

    @pl.when(jnp.logical_not(trusted))
    def _():
        def slow(kb, m):
            st, vt = scores_t(kb)
            m_new = jnp.maximum(m, jnp.max(st, axis=0, keepdims=True))
            pt = jnp.exp2(st - m_new).astype(BF16)
            acc_ref[...] = jnp.exp2(m - m_new) * acc_ref[...] + jnp.dot(vt, pt, preferred_element_type=F32)
            return m_new

        acc_ref[...] = jnp.zeros_like(acc_ref)
        lax.fori_loop(0, nkv, slow, jnp.full((1, tq), -jnp.inf, F32))
        finish()


def _mla_attn(q, k, vt, batch, seq, tq=2048, tk=512, unroll=16):
    nq = seq // tq
    unroll = min(unroll, seq // tk)
    assert seq % (tk * unroll) == 0
    return pl.pallas_call(
        functools.partial(_mla_attn_kernel, tk=tk, unroll=unroll),
        grid=(batch, B_HEADS, nq),
        in_specs=[
            pl.BlockSpec((tq, QK_PAD), lambda b, h, i: (b * nq + i, h)),
            pl.BlockSpec((seq, QK_PAD), lambda b, h, i: (b, h)),
            pl.BlockSpec((VT_ROWS, seq), lambda b, h, i: (h, b)),
        ],
        out_specs=pl.BlockSpec((tq, V_DIM), lambda b, h, i: (b * nq + i, h)),
        out_shape=jax.ShapeDtypeStruct((batch * seq, B_HEADS * V_DIM), BF16),
        scratch_shapes=[pltpu.VMEM((VT_ROWS, tq), F32)],
        compiler_params=_params(("parallel", "parallel", "parallel")),
        name="mla_attn",
    )(q, k, vt)


def _rope_angles(seq, d):
    pos = jnp.arange(seq, dtype=F32)
    freqs = ROPE_THETA ** (-jnp.arange(0, d, 2, dtype=F32) / d)
    ang = pos[:, None] * freqs[None, :]
    return jnp.cos(ang), jnp.sin(ang)


def _spread_rope(t, axis):
    a, b = jnp.split(t, 2, axis=axis)
    z = jnp.zeros_like(a)
    return jnp.concatenate([a, z, b, z], axis=axis)


def _pair_columns(w):
    d = w.shape[0]
    w = w.reshape(d, A_HEADS // 2, 2, 2, A_HEAD_DIM // 2)
    return w.transpose(0, 1, 3, 2, 4).reshape(d, A_WIDTH)


def _residue_major(table, tm, dil):
    s, w = table.shape
    return table.reshape(s // tm, tm // dil, dil, w).transpose(0, 2, 1, 3).reshape(s, w)


def kernel(x, norm_mix, norm_ffn, a_w_qkv, a_q_gain, a_k_gain, a_w_o, b_w_in, b_q_a_gain, b_w_qb,
           b_kv_a_gain, b_w_kvb, b_q_gain, b_k_gain, b_w_o, ffn_w1, ffn_w2):
    batch, seq, d_model = x.shape
    T = batch * seq
    x2d = x.reshape(T, d_model)

    tm_a = 512
    cos, sin = _rope_angles(seq, A_HEAD_DIM)
    cos_a = jnp.concatenate([cos, cos], -1)
    sin_a = jnp.concatenate([sin, sin], -1)
    w_qkv = a_w_qkv[0].reshape(d_model, 3, N_GROUPS, A_WIDTH)
    hd = A_HEAD_DIM // 2
    outs, lses = [], []
    for g, (_, dil) in enumerate(DIL_CONFIGS):
        w3 = jnp.stack([_pair_columns(w_qkv[:, 0, g]), _pair_columns(w_qkv[:, 1, g]), w_qkv[:, 2, g]]).astype(BF16)
        tables = jnp.stack([_residue_major(cos_a, tm_a, dil), _residue_major(sin_a, tm_a, dil)])

        def pair_gain(gain):
            return jnp.stack([jnp.tile(gain[:hd], 2), jnp.tile(gain[hd:], 2)])

        q_scale = LOG2E / math.sqrt(A_HEAD_DIM)
        gains = jnp.stack([pair_gain(a_q_gain[0, g] * q_scale), pair_gain(a_k_gain[0, g])])
        bound = 1.02 * A_HEAD_DIM * q_scale * jnp.max(jnp.abs(a_q_gain[0, g])) * jnp.max(jnp.abs(a_k_gain[0, g]))
        qkv = _a_proj(x2d, norm_mix[0][None, :], w3, gains, tables, batch, seq, dil, tm_a)
        o, lse = _dil_attn(qkv, jnp.reshape(bound, (1,)).astype(F32), g)
        outs.append(o)
        lses.append(lse)
    merged = _merge(outs, lses, seq, tm=512)
    x2d = _ffn(x2d, merged, a_w_o[0].astype(BF16), norm_ffn[0][None, :],
               ffn_w1[0].astype(BF16), ffn_w2[0].astype(BF16), tm=512)

    cos, sin = _rope_angles(seq, ROPE_DIM)
    cos_b = _spread_rope(jnp.concatenate([cos, cos], -1), -1)
    sin_b = _spread_rope(jnp.concatenate([-sin, sin], -1), -1)

    def head_gain(gain, scale):
        return (jnp.concatenate([gain[:NOPE_DIM], _spread_rope(gain[NOPE_DIM:], 0)]) * scale)[None, :]

    q_scale = LOG2E / math.sqrt(QK_DIM)
    gq = head_gain(b_q_gain[0], q_scale)
    gk = head_gain(b_k_gain[0], 1.0)
    bound = 1.02 * QK_DIM * q_scale * jnp.max(jnp.abs(b_q_gain[0])) * jnp.max(jnp.abs(b_k_gain[0]))
    const_lane = (jnp.arange(LANES) == CONST_LANE).astype(F32)[None, :]

    def half_roll(w, gain):
        return jnp.roll(w * gain, LANES // 2, axis=-1)

    w_in = b_w_in[0]
    k_rope_w = _spread_rope(w_in[:, Q_LORA + KV_LORA:], 1)
    w_in = jnp.concatenate([w_in[:, :Q_LORA + KV_LORA], k_rope_w, half_roll(k_rope_w, gk[:, LANES:])], 1)
    w_qb = b_w_qb[0].reshape(Q_LORA, B_HEADS, QK_DIM)
    q_rope_w = _spread_rope(w_qb[..., NOPE_DIM:], 2)
    w_qb = jnp.concatenate([w_qb[..., :NOPE_DIM], q_rope_w, half_roll(q_rope_w, gq[:, LANES:])], -1)
    w_qb = w_qb.reshape(Q_LORA, B_HEADS * Q3_PAD)
    w_kvb = b_w_kvb[0].reshape(KV_LORA, B_HEADS, NOPE_DIM + V_DIM)
    w_kn = w_kvb[..., :NOPE_DIM].reshape(KV_LORA, B_HEADS * NOPE_DIM)
    w_v = w_kvb[..., NOPE_DIM:].reshape(KV_LORA, B_HEADS * V_DIM).T

    consts = (norm_mix[1][None, :], w_in.astype(BF16), b_q_a_gain[0][None, :], b_kv_a_gain[0][None, :],
              w_qb.astype(BF16), w_kn.astype(BF16), w_v.astype(BF16), gq, gk,
              -bound * const_lane, const_lane)
    tables = (cos_b * gq[:, LANES:], cos_b * gk[:, LANES:], sin_b)
    q, k, v = _b_proj(x2d, consts, tables, seq, tm=512)
    o = _mla_attn(q, k, v, batch, seq)
    x2d = _ffn(x2d, o, b_w_o[0].astype(BF16), norm_ffn[1][None, :],
               ffn_w1[1].astype(BF16), ffn_w2[1].astype(BF16), tm=512)
    return x2d.reshape(batch, seq, d_model)
```

```python
import functools
import math

import jax
import jax.numpy as jnp
from jax import lax
from jax.experimental import pallas as pl
from jax.experimental.pallas import tpu as pltpu

EPS = 1e-6
ROPE_THETA = 10000.0
NEG_FILL = -1e30
LOG2E = math.log2(math.e)

DIL_CONFIGS = ((128, 1), (512, 4), (2048, 16))
N_GROUPS = len(DIL_CONFIGS)
A_HEADS = 8
A_HEAD_DIM = 128
A_WIDTH = A_HEADS * A_HEAD_DIM

B_HEADS = 8
Q_LORA = 256
KV_LORA = 128
NOPE_DIM = 128
ROPE_DIM = 64
V_DIM = 128
QK_DIM = NOPE_DIM + ROPE_DIM
QK_PAD = 256
Q3_PAD = 384
VT_ROWS = 144
CONST_LANE = 32
L_MIN = 2.0 ** -80

LANES = 128
VMEM_LIMIT = 48 * 1024 * 1024

BF16 = jnp.bfloat16
F32 = jnp.float32
NT_DIMS = (((1,), (1,)), ((), ()))


def _params(sem):
    return pltpu.CompilerParams(dimension_semantics=sem, vmem_limit_bytes=VMEM_LIMIT)


def _resident(shape):
    zeros = (0,) * len(shape)
    return pl.BlockSpec(shape, lambda *_: zeros, pipeline_mode=pl.Buffered(1))


def _rms_scale(x, width):
    return lax.rsqrt(jnp.sum(x * x, axis=-1, keepdims=True) * (1.0 / width) + EPS)


def _a_proj_kernel(x_ref, nrm_ref, w_ref, gain_ref, tab_ref, o_ref, h_ref, hs_ref, *, dil):
    tm = x_ref.shape[0]
    rows = tm // dil
    x = x_ref[...]
    hn = x * _rms_scale(x, x.shape[-1]) * nrm_ref[...]
    if dil == 1:
        h_ref[...] = hn.astype(BF16)
    else:
        for kb in range(hs_ref.shape[0]):
            hs_ref[kb] = hn[:, kb * LANES:(kb + 1) * LANES]
        for c in range(dil):
            for kb in range(hs_ref.shape[0]):
                h_ref[c * rows:(c + 1) * rows, kb * LANES:(kb + 1) * LANES] = (
                    hs_ref[kb, pl.ds(c, rows, stride=dil), :].astype(BF16))

    def store(which, col0, val):
        for c in range(dil):
            o_ref[which, c, :, col0:col0 + LANES] = val[c * rows:(c + 1) * rows].astype(BF16)

    for which in range(2):
        y = jnp.dot(h_ref[...], w_ref[which], preferred_element_type=F32)
        ga, gb = gain_ref[which, 0:1, :], gain_ref[which, 1:2, :]
        cos, sin = tab_ref[0], tab_ref[1]
        ca, sb, cb, sa = ga * cos, gb * sin, gb * cos, ga * sin
        for p in range(A_HEADS // 2):
            y2 = y[:, p * 2 * LANES:(p + 1) * 2 * LANES]
            t = y2[:, :LANES] * y2[:, :LANES] + y2[:, LANES:] * y2[:, LANES:]
            lo_lane = lax.broadcasted_iota(jnp.int32, t.shape, 1) < LANES // 2
            s_lo = jnp.sum(jnp.where(lo_lane, t, 0.0), axis=-1, keepdims=True)
            s_all = jnp.sum(t, axis=-1, keepdims=True)
            ms = jnp.where(lo_lane, s_lo, s_all - s_lo) * (1.0 / A_HEAD_DIM)
            rinv = lax.rsqrt(ms + EPS)
            ar = y2[:, :LANES] * rinv
            br = y2[:, LANES:] * rinv
            store(which, p * 2 * LANES, ar * ca - br * sb)
            store(which, p * 2 * LANES + LANES, br * cb + ar * sa)

    y = jnp.dot(h_ref[...], w_ref[2], preferred_element_type=F32)
    for hh in range(A_HEADS):
        store(2, hh * LANES, y[:, hh * LANES:(hh + 1) * LANES])


def _a_proj(x2d, nrm, w3, gains, tables, batch, seq, dil, tm):
    T, D = x2d.shape
    spt = seq // tm
    rows = tm // dil
    return pl.pallas_call(
        functools.partial(_a_proj_kernel, dil=dil),
        grid=(T // tm,),
        in_specs=[
            pl.BlockSpec((tm, D), lambda i: (i, 0)),
            _resident(nrm.shape),
            _resident(w3.shape),
            _resident(gains.shape),
            pl.BlockSpec((2, tm, LANES), lambda i: (0, i % spt, 0)),
        ],
        out_specs=pl.BlockSpec((3, None, dil, rows, A_WIDTH), lambda i: (0, i // spt, 0, i % spt, 0)),
        out_shape=jax.ShapeDtypeStruct((3, batch, dil, seq // dil, A_WIDTH), BF16),
        scratch_shapes=[pltpu.VMEM((tm, D), BF16), pltpu.VMEM((D // LANES, tm, LANES), F32)],
        compiler_params=_params(("parallel",)),
        name=f"a_proj_d{dil}",
    )(x2d, nrm, w3, gains, tables)


def _dil_attn_kernel(bound_ref, q_ref, k_ref, kp_ref, kn_ref, v_ref, vp_ref, vn_ref, hot_ref,
                     o_ref, st_ref, *, length, chunk, sub, half):
    i = pl.program_id(2)
    nk = sub + 2 * half

    def window(ref, prev_ref, next_ref, j, cols):
        lo, hi = j * sub - half, (j + 1) * sub + half
        parts = [prev_ref[:, cols]] if lo < 0 else []
        parts.append(ref[max(lo, 0):min(hi, chunk), cols])
        if hi > chunk:
            parts.append(next_ref[:, cols])
        return parts[0] if len(parts) == 1 else jnp.concatenate(parts, axis=0)

    qi = lax.broadcasted_iota(jnp.int32, (2 * sub, nk), 0) % sub
    kj = lax.broadcasted_iota(jnp.int32, (2 * sub, nk), 1)
    band = jnp.abs(qi + half - kj) <= half
    lane = lax.broadcasted_iota(jnp.int32, (sub, LANES), 1)
    lane2 = lax.broadcasted_iota(jnp.int32, (1, 2 * LANES), 1) % LANES
    first = (lane2 < LANES // 2).astype(BF16)
    second = (lane2 >= LANES // 2).astype(BF16)

    bound = bound_ref[0]

    def sub_block(j, use_bound):
        r0 = j * sub
        base = i * chunk + j * sub
        lo = half - base
        hi = length + half - base
        mask = band & (kj >= lo) & (kj < hi)
        bias = jnp.where(mask, -bound if use_bound else 0.0, NEG_FILL)
        stats = jnp.zeros((sub, LANES), F32)
        for p in range(A_HEADS // 2):
            cols = slice(p * 2 * LANES, (p + 1) * 2 * LANES)
            q2 = q_ref[pl.ds(r0, sub), cols]
            qq = jnp.concatenate([q2 * first, q2 * second], axis=0)
            s = lax.dot_general(qq, window(k_ref, kp_ref, kn_ref, j, cols), NT_DIMS,
                                preferred_element_type=F32) + bias
            if use_bound:
                eb = jnp.exp2(s).astype(BF16)
            else:
                ref = jnp.max(s, axis=-1, keepdims=True)
                e = jnp.exp2(s - ref)
                den = jnp.sum(e, axis=-1, keepdims=True)
                eb = e.astype(BF16)
                lse = (ref + jnp.log2(den)) * (1.0 / LOG2E)
            for t in range(2):
                h = 2 * p + t
                hc = slice(h * LANES, (h + 1) * LANES)
                rs = slice(t * sub, (t + 1) * sub)
                v = window(v_ref, vp_ref, vn_ref, j, hc)
                if use_bound:
                    r = jnp.dot(eb[rs], jnp.concatenate([v, hot_ref[h]], axis=1), preferred_element_type=F32)
                    o_ref[pl.ds(r0, sub), hc] = r[:, :LANES].astype(BF16)
                    stats = stats + r[:, LANES:]
                else:
                    o = jnp.dot(eb[rs], v, preferred_element_type=F32) / den[rs]
                    o_ref[pl.ds(r0, sub), hc] = o.astype(BF16)
                    stats = jnp.where(lane == h, lse[rs], stats)
        if use_bound:
            l2 = jnp.log2(stats)
            stats = jnp.where(lane < A_HEADS, (bound + l2) * (1.0 / LOG2E), jnp.where(lane < 2 * A_HEADS, l2, 0.0))
        st_ref[pl.ds(r0, sub), :] = stats

    def run(use_bound):
        for j in range(chunk // sub):
            sub_block(j, use_bound)

    run(True)
    head_lanes = lax.broadcasted_iota(jnp.int32, st_ref.shape, 1) < A_HEADS
    lse_min = jnp.min(jnp.where(head_lanes, st_ref[...], jnp.inf), keepdims=True)
    trusted = jnp.min(lse_min * LOG2E - bound) >= math.log2(L_MIN)

    @pl.when(jnp.logical_not(trusted))
    def _():
        run(False)


def _dil_attn(qkv, bound, g, chunk=1024, sub=128):
    window, dil = DIL_CONFIGS[g]
    half = window // (2 * dil)
    _, batch, _, length, _ = qkv.shape
    chunk = min(chunk, length)
    assert length % chunk == 0 and chunk % sub == 0 and chunk % half == 0
    cph = chunk // half
    last = length // half - 1

    def main(which):
        return pl.BlockSpec((None, None, None, chunk, A_WIDTH), lambda b, c, i: (which, b, c, i, 0))

    def prev(which):
        return pl.BlockSpec((None, None, None, half, A_WIDTH),
                            lambda b, c, i: (which, b, c, jnp.maximum(i * cph - 1, 0), 0))

    def nxt(which):
        return pl.BlockSpec((None, None, None, half, A_WIDTH),
                            lambda b, c, i: (which, b, c, jnp.minimum((i + 1) * cph, last), 0))

    col = jnp.arange(LANES)[None, None, :]
    head = jnp.arange(A_HEADS)[:, None, None]
    hot = jnp.broadcast_to((col == head) | (col == head + A_HEADS), (A_HEADS, sub + 2 * half, LANES)).astype(BF16)
    kern = functools.partial(_dil_attn_kernel, length=length, chunk=chunk, sub=sub, half=half)
    return pl.pallas_call(
        kern,
        grid=(batch, dil, length // chunk),
        in_specs=[pl.BlockSpec(memory_space=pltpu.SMEM), main(0), main(1), prev(1), nxt(1), main(2), prev(2), nxt(2),
                  _resident(hot.shape)],
        out_specs=[
            pl.BlockSpec((None, None, chunk, A_WIDTH), lambda b, c, i: (b, c, i, 0)),
            pl.BlockSpec((None, None, chunk, LANES), lambda b, c, i: (b, c, i, 0)),
        ],
        out_shape=[
            jax.ShapeDtypeStruct((batch, dil, length, A_WIDTH), BF16),
            jax.ShapeDtypeStruct((batch, dil, length, LANES), F32),
        ],
        compiler_params=_params(("parallel", "parallel", "parallel")),
        name=f"dil_attn_g{g}",
    )(bound, qkv, qkv, qkv, qkv, qkv, qkv, qkv, hot)


def _merge_tile(o_refs, l_refs, e_ref, out_ref, os_ref, ls_ref):
    nlb = os_ref.shape[1]
    natural = [o_refs[g].shape[0] == 1 for g in range(N_GROUPS)]
    for g in range(N_GROUPS):
        if natural[g]:
            continue
        dil, rows = o_refs[g].shape[0], o_refs[g].shape[1]
        for c in range(dil):
            oc = o_refs[g][c].astype(F32)
            for kb in range(nlb):
                os_ref[g, kb, pl.ds(c, rows, stride=dil), :] = oc[:, kb * LANES:(kb + 1) * LANES]
            ls_ref[g, pl.ds(c, rows, stride=dil), :] = l_refs[g][c]
    stats = [l_refs[g][0] if natural[g] else ls_ref[g] for g in range(N_GROUPS)]
    mx = jnp.maximum(jnp.maximum(stats[0], stats[1]), stats[2])
    e = [jnp.exp(st - mx) for st in stats]
    inv = 1.0 / (e[0] + e[1] + e[2])
    coef = []
    for g in range(N_GROUPS):
        pending = jnp.exp2(-pltpu.roll(stats[g], LANES - A_HEADS, 1))
        coef.append((e[g] * inv * pending).astype(BF16))
    for kb in range(nlb):
        cols = slice(kb * LANES, (kb + 1) * LANES)
        acc = None
        for g in range(N_GROUPS):
            w = jnp.dot(coef[g], e_ref[:, cols], preferred_element_type=F32)
            o_nat = o_refs[g][0, :, cols].astype(F32) if natural[g] else os_ref[g, kb]
            acc = w * o_nat if acc is None else acc + w * o_nat
        out_ref[:, cols] = acc.astype(BF16)


def _ffn_tile(x, a, wo_ref, nrm_ref, w1_ref, w2_ref, ff_chunk):
    x1 = x + jnp.dot(a, wo_ref[...], preferred_element_type=F32)
    h = (x1 * _rms_scale(x1, x1.shape[-1]) * nrm_ref[...]).astype(BF16)
    acc = x1
    for c in range(w1_ref.shape[1] // ff_chunk):
        cols = slice(c * ff_chunk, (c + 1) * ff_chunk)
        a = jnp.maximum(jnp.dot(h, w1_ref[:, cols], preferred_element_type=F32), 0.0)
        acc = acc + jnp.dot((a * a).astype(BF16), w2_ref[cols, :], preferred_element_type=F32)
    return acc


def _ffn_kernel(x_ref, a_ref, wo_ref, nrm_ref, w1_ref, w2_ref, out_ref, *, ff_chunk):
    out_ref[...] = _ffn_tile(x_ref[...], a_ref[...], wo_ref, nrm_ref, w1_ref, w2_ref, ff_chunk)


def _ffn(x2d, a2d, w_o, nrm, w1, w2, tm, ff_chunk=1024):
    T, D = x2d.shape
    return pl.pallas_call(
        functools.partial(_ffn_kernel, ff_chunk=ff_chunk),
        grid=(T // tm,),
        in_specs=[
            pl.BlockSpec((tm, D), lambda i: (i, 0)),
            pl.BlockSpec((tm, a2d.shape[1]), lambda i: (i, 0)),
            _resident(w_o.shape),
            _resident(nrm.shape),
            _resident(w1.shape),
            _resident(w2.shape),
        ],
        out_specs=pl.BlockSpec((tm, D), lambda i: (i, 0)),
        out_shape=jax.ShapeDtypeStruct((T, D), F32),
        compiler_params=_params(("parallel",)),
        name="outproj_ffn",
    )(x2d, a2d, w_o, nrm, w1, w2)


def _merge_kernel(o0_ref, o1_ref, o2_ref, l0_ref, l1_ref, l2_ref, e_ref, out_ref, os_ref, ls_ref):
    _merge_tile((o0_ref, o1_ref, o2_ref), (l0_ref, l1_ref, l2_ref), e_ref, out_ref, os_ref, ls_ref)


def _merge(outs, stats, seq, tm):
    batch = outs[0].shape[0]
    spt = seq // tm
    expand = (jnp.arange(LANES)[:, None] == (jnp.arange(A_WIDTH)[None, :] // A_HEAD_DIM)).astype(BF16)

    def spec(arr):
        dil, width = arr.shape[1], arr.shape[3]
        return pl.BlockSpec((None, dil, tm // dil, width), lambda i: (i // spt, 0, i % spt, 0))

    return pl.pallas_call(
        _merge_kernel,
        grid=(batch * spt,),
        in_specs=[spec(o) for o in outs] + [spec(l) for l in stats] + [_resident(expand.shape)],
        out_specs=pl.BlockSpec((tm, A_WIDTH), lambda i: (i, 0)),
        out_shape=jax.ShapeDtypeStruct((batch * seq, A_WIDTH), BF16),
        scratch_shapes=[pltpu.VMEM((N_GROUPS, A_WIDTH // LANES, tm, LANES), F32),
                        pltpu.VMEM((N_GROUPS, tm, LANES), F32)],
        compiler_params=_params(("parallel",)),
        name="a_merge",
    )(*outs, *stats, expand)


def _b_proj_kernel(x_ref, nrm_ref, win_ref, qag_ref, kvag_ref, wqb_ref, wkn_ref, wv_ref,
                   gq_ref, gk_ref, cq_ref, ck_ref, cosq_ref, cosk_ref, sin_ref,
                   q_ref, k_ref, v_ref):
    x = x_ref[...]
    h = (x * _rms_scale(x, x.shape[-1]) * nrm_ref[...]).astype(BF16)
    lat = jnp.dot(h, win_ref[...], preferred_element_type=F32)
    c_q = lat[:, :Q_LORA]
    c_kv = lat[:, Q_LORA:Q_LORA + KV_LORA]
    k_rope = lat[:, Q_LORA + KV_LORA:Q_LORA + KV_LORA + LANES]
    k_roll = lat[:, Q_LORA + KV_LORA + LANES:]
    cqn = (c_q * _rms_scale(c_q, Q_LORA) * qag_ref[...]).astype(BF16)
    ckvn = (c_kv * _rms_scale(c_kv, KV_LORA) * kvag_ref[...]).astype(BF16)
    sin = sin_ref[...]

    vt_all = lax.dot_general(wv_ref[...], ckvn, NT_DIMS, preferred_element_type=F32)
    ones_rows = (lax.broadcasted_iota(jnp.int32, (VT_ROWS - V_DIM, x.shape[0]), 0) == 0).astype(BF16)
    for hh in range(B_HEADS):
        v_ref[hh * VT_ROWS:hh * VT_ROWS + V_DIM, :] = vt_all[hh * V_DIM:(hh + 1) * V_DIM, :].astype(BF16)
        v_ref[hh * VT_ROWS + V_DIM:(hh + 1) * VT_ROWS, :] = ones_rows

    q3 = jnp.dot(cqn, wqb_ref[...], preferred_element_type=F32)
    gq_n = gq_ref[:, :LANES]
    cos_q = cosq_ref[...]
    for hh in range(B_HEADS):
        qn = q3[:, hh * Q3_PAD:hh * Q3_PAD + LANES]
        qr = q3[:, hh * Q3_PAD + LANES:hh * Q3_PAD + QK_PAD]
        rq = lax.rsqrt(jnp.sum(qn * qn + qr * qr, axis=-1, keepdims=True) * (1.0 / QK_DIM) + EPS)
        qr = qr * cos_q + q3[:, hh * Q3_PAD + QK_PAD:(hh + 1) * Q3_PAD] * sin
        q_ref[:, hh * QK_PAD:hh * QK_PAD + LANES] = (qn * rq * gq_n).astype(BF16)
        q_ref[:, hh * QK_PAD + LANES:(hh + 1) * QK_PAD] = (qr * rq + cq_ref[...]).astype(BF16)

    kn_all = jnp.dot(ckvn, wkn_ref[...], preferred_element_type=F32)
    ss_kr = jnp.sum(k_rope * k_rope, axis=-1, keepdims=True)
    kr = k_rope * cosk_ref[...] + k_roll * sin
    gk_n = gk_ref[:, :LANES]
    for hh in range(B_HEADS):
        kn = kn_all[:, hh * LANES:(hh + 1) * LANES]
        rk = lax.rsqrt((jnp.sum(kn * kn, axis=-1, keepdims=True) + ss_kr) * (1.0 / QK_DIM) + EPS)
        k_ref[:, hh * QK_PAD:hh * QK_PAD + LANES] = (kn * rk * gk_n).astype(BF16)
        k_ref[:, hh * QK_PAD + LANES:(hh + 1) * QK_PAD] = (kr * rk + ck_ref[...]).astype(BF16)


def _b_proj(x2d, consts, tables, seq, tm):
    T, D = x2d.shape
    spt = seq // tm
    row = lambda i: (i, 0)
    pos = pl.BlockSpec((tm, LANES), lambda i: (i % spt, 0))
    return pl.pallas_call(
        _b_proj_kernel,
        grid=(T // tm,),
        in_specs=[pl.BlockSpec((tm, D), row)] + [_resident(c.shape) for c in consts] + [pos] * len(tables),
        out_specs=[
            pl.BlockSpec((tm, B_HEADS * QK_PAD), row),
            pl.BlockSpec((tm, B_HEADS * QK_PAD), row),
            pl.BlockSpec((B_HEADS * VT_ROWS, tm), lambda i: (0, i)),
        ],
        out_shape=[
            jax.ShapeDtypeStruct((T, B_HEADS * QK_PAD), BF16),
            jax.ShapeDtypeStruct((T, B_HEADS * QK_PAD), BF16),
            jax.ShapeDtypeStruct((B_HEADS * VT_ROWS, T), BF16),
        ],
        compiler_params=_params(("parallel",)),
        name="b_proj",
    )(x2d, *consts, *tables)


def _mla_attn_kernel(q_ref, k_ref, vt_ref, o_ref, acc_ref, *, tk, unroll):
    q = q_ref[...]
    tq = q.shape[0]
    nkv = k_ref.shape[0] // tk

    def scores_t(kb):
        r0 = pl.multiple_of(kb * tk, tk)
        st = lax.dot_general(k_ref[pl.ds(r0, tk), :], q, NT_DIMS, preferred_element_type=F32)
        return st, vt_ref[:, pl.ds(r0, tk)]

    def fast(kb, carry):
        for u in range(unroll):
            st, vt = scores_t(kb * unroll + u)
            acc_ref[...] += jnp.dot(vt, jnp.exp2(st).astype(BF16), preferred_element_type=F32)
        return carry

    def finish():
        inv = 1.0 / acc_ref[V_DIM:V_DIM + 1, :]
        o_ref[...] = (acc_ref[:V_DIM, :] * inv).T.astype(BF16)

    acc_ref[...] = jnp.zeros_like(acc_ref)
    lax.fori_loop(0, nkv // unroll, fast, 0)
    trusted = jnp.min(acc_ref[V_DIM:V_DIM + 1, :]) >= L_MIN

    @pl.when(trusted)
    def _():
        finish()

    @pl.when(jnp.logical_not(trusted))
    def _():
        def slow(kb, m):
            st, vt = scores_t(kb)
            m_new = jnp.maximum(m, jnp.max(st, axis=0, keepdims=True))
            pt = jnp.exp2(st - m_new).astype(BF16)
            acc_ref[...] = jnp.exp2(m - m_new) * acc_ref[...] + jnp.dot(vt, pt, preferred_element_type=F32)
            return m_new

        acc_ref[...] = jnp.zeros_like(acc_ref)
        lax.fori_loop(0, nkv, slow, jnp.full((1, tq), -jnp.inf, F32))
        finish()


def _mla_attn(q, k, vt, batch, seq, tq=2048, tk=512, unroll=16):
    nq = seq // tq
    unroll = min(unroll, seq // tk)
    assert seq % (tk * unroll) == 0
    return pl.pallas_call(
        functools.partial(_mla_attn_kernel, tk=tk, unroll=unroll),
        grid=(batch, B_HEADS, nq),
        in_specs=[
            pl.BlockSpec((tq, QK_PAD), lambda b, h, i: (b * nq + i, h)),
            pl.BlockSpec((seq, QK_PAD), lambda b, h, i: (b, h)),
            pl.BlockSpec((VT_ROWS, seq), lambda b, h, i: (h, b)),
        ],
        out_specs=pl.BlockSpec((tq, V_DIM), lambda b, h, i: (b * nq + i, h)),
        out_shape=jax.ShapeDtypeStruct((batch * seq, B_HEADS * V_DIM), BF16),
        scratch_shapes=[pltpu.VMEM((VT_ROWS, tq), F32)],
        compiler_params=_params(("parallel", "parallel", "parallel")),
        name="mla_attn",
    )(q, k, vt)


def _rope_angles(seq, d):
    pos = jnp.arange(seq, dtype=F32)
    freqs = ROPE_THETA ** (-jnp.arange(0, d, 2, dtype=F32) / d)
    ang = pos[:, None] * freqs[None, :]
    return jnp.cos(ang), jnp.sin(ang)


def _spread_rope(t, axis):
    a, b = jnp.split(t, 2, axis=axis)
    z = jnp.zeros_like(a)
    return jnp.concatenate([a, z, b, z], axis=axis)


def _pair_columns(w):
    d = w.shape[0]
    w = w.reshape(d, A_HEADS // 2, 2, 2, A_HEAD_DIM // 2)
    return w.transpose(0, 1, 3, 2, 4).reshape(d, A_WIDTH)


def _residue_major(table, tm, dil):
    s, w = table.shape
    return table.reshape(s // tm, tm // dil, dil, w).transpose(0, 2, 1, 3).reshape(s, w)


def kernel(x, norm_mix, norm_ffn, a_w_qkv, a_q_gain, a_k_gain, a_w_o, b_w_in, b_q_a_gain, b_w_qb,
           b_kv_a_gain, b_w_kvb, b_q_gain, b_k_gain, b_w_o, ffn_w1, ffn_w2):
    batch, seq, d_model = x.shape
    T = batch * seq
    x2d = x.reshape(T, d_model)

    tm_a = 1024
    cos, sin = _rope_angles(seq, A_HEAD_DIM)
    cos_a = jnp.concatenate([cos, cos], -1)
    sin_a = jnp.concatenate([sin, sin], -1)
    w_qkv = a_w_qkv[0].reshape(d_model, 3, N_GROUPS, A_WIDTH)
    hd = A_HEAD_DIM // 2
    outs, lses = [], []
    for g, (_, dil) in enumerate(DIL_CONFIGS):
        w3 = jnp.stack([_pair_columns(w_qkv[:, 0, g]), _pair_columns(w_qkv[:, 1, g]), w_qkv[:, 2, g]]).astype(BF16)
        tables = jnp.stack([_residue_major(cos_a, tm_a, dil), _residue_major(sin_a, tm_a, dil)])

        def pair_gain(gain):
            return jnp.stack([jnp.tile(gain[:hd], 2), jnp.tile(gain[hd:], 2)])

        q_scale = LOG2E / math.sqrt(A_HEAD_DIM)
        gains = jnp.stack([pair_gain(a_q_gain[0, g] * q_scale), pair_gain(a_k_gain[0, g])])
        bound = 1.02 * A_HEAD_DIM * q_scale * jnp.max(jnp.abs(a_q_gain[0, g])) * jnp.max(jnp.abs(a_k_gain[0, g]))
        qkv = _a_proj(x2d, norm_mix[0][None, :], w3, gains, tables, batch, seq, dil, tm_a)
        o, lse = _dil_attn(qkv, jnp.reshape(bound, (1,)).astype(F32), g)
        outs.append(o)
        lses.append(lse)
    merged = _merge(outs, lses, seq, tm=512)
    x2d = _ffn(x2d, merged, a_w_o[0].astype(BF16), norm_ffn[0][None, :],
               ffn_w1[0].astype(BF16), ffn_w2[0].astype(BF16), tm=512)

    cos, sin = _rope_angles(seq, ROPE_DIM)
    cos_b = _spread_rope(jnp.concatenate([cos, cos], -1), -1)
    sin_b = _spread_rope(jnp.concatenate([-sin, sin], -1), -1)

    def head_gain(gain, scale):
        return (jnp.concatenate([gain[:NOPE_DIM], _spread_rope(gain[NOPE_DIM:], 0)]) * scale)[None, :]

    q_scale = LOG2E / math.sqrt(QK_DIM)
    gq = head_gain(b_q_gain[0], q_scale)
    gk = head_gain(b_k_gain[0], 1.0)
    bound = 1.02 * QK_DIM * q_scale * jnp.max(jnp.abs(b_q_gain[0])) * jnp.max(jnp.abs(b_k_gain[0]))
    const_lane = (jnp.arange(LANES) == CONST_LANE).astype(F32)[None, :]

    def half_roll(w, gain):
        return jnp.roll(w * gain, LANES // 2, axis=-1)

    w_in = b_w_in[0]
    k_rope_w = _spread_rope(w_in[:, Q_LORA + KV_LORA:], 1)
    w_in = jnp.concatenate([w_in[:, :Q_LORA + KV_LORA], k_rope_w, half_roll(k_rope_w, gk[:, LANES:])], 1)
    w_qb = b_w_qb[0].reshape(Q_LORA, B_HEADS, QK_DIM)
    q_rope_w = _spread_rope(w_qb[..., NOPE_DIM:], 2)
    w_qb = jnp.concatenate([w_qb[..., :NOPE_DIM], q_rope_w, half_roll(q_rope_w, gq[:, LANES:])], -1)
    w_qb = w_qb.reshape(Q_LORA, B_HEADS * Q3_PAD)
    w_kvb = b_w_kvb[0].reshape(KV_LORA, B_HEADS, NOPE_DIM + V_DIM)
    w_kn = w_kvb[..., :NOPE_DIM].reshape(KV_LORA, B_HEADS * NOPE_DIM)
    w_v = w_kvb[..., NOPE_DIM:].reshape(KV_LORA, B_HEADS * V_DIM).T

    consts = (norm_mix[1][None, :], w_in.astype(BF16), b_q_a_gain[0][None, :], b_kv_a_gain[0][None, :],
              w_qb.astype(BF16), w_kn.astype(BF16), w_v.astype(BF16), gq, gk,
              -bound * const_lane, const_lane)
    tables = (cos_b * gq[:, LANES:], cos_b * gk[:, LANES:], sin_b)
    q, k, v = _b_proj(x2d, consts, tables, seq, tm=512)
    o = _mla_attn(q, k, v, batch, seq)
    x2d = _ffn(x2d, o, b_w_o[0].astype(BF16), norm_ffn[1][None, :],
               ffn_w1[1].astype(BF16), ffn_w2[1].astype(BF16), tm=512)
    return x2d.reshape(batch, seq, d_model)
```

```python
import functools
import math

import jax
import jax.numpy as jnp
from jax import lax
from jax.experimental import pallas as pl
from jax.experimental.pallas import tpu as pltpu

EPS = 1e-6
ROPE_THETA = 10000.0
NEG_FILL = -1e30
LOG2E = math.log2(math.e)

DIL_CONFIGS = ((128, 1), (512, 4), (2048, 16))
N_GROUPS = len(DIL_CONFIGS)
A_HEADS = 8
A_HEAD_DIM = 128
A_WIDTH = A_HEADS * A_HEAD_DIM

B_HEADS = 8
Q_LORA = 256
KV_LORA = 128
NOPE_DIM = 128
ROPE_DIM = 64
V_DIM = 128
QK_DIM = NOPE_DIM + ROPE_DIM
QK_PAD = 256
Q3_PAD = 384
VT_ROWS = 144
CONST_LANE = 32
L_MIN = 2.0 ** -80

LANES = 128
VMEM_LIMIT = 48 * 1024 * 1024

BF16 = jnp.bfloat16
F32 = jnp.float32
NT_DIMS = (((1,), (1,)), ((), ()))


def _params(sem):
    return pltpu.CompilerParams(dimension_semantics=sem, vmem_limit_bytes=VMEM_LIMIT)


def _resident(shape):
    zeros = (0,) * len(shape)
    return pl.BlockSpec(shape, lambda *_: zeros, pipeline_mode=pl.Buffered(1))


def _rms_scale(x, width):
    return lax.rsqrt(jnp.sum(x * x, axis=-1, keepdims=True) * (1.0 / width) + EPS)


def _a_proj_kernel(x_ref, nrm_ref, w_ref, gain_ref, tab_ref, o_ref, h_ref, hs_ref, *, dil):
    tm = x_ref.shape[0]
    rows = tm // dil
    x = x_ref[...]
    hn = x * _rms_scale(x, x.shape[-1]) * nrm_ref[...]
    if dil == 1:
        h_ref[...] = hn.astype(BF16)
    else:
        for kb in range(hs_ref.shape[0]):
            hs_ref[kb] = hn[:, kb * LANES:(kb + 1) * LANES]
        for c in range(dil):
            for kb in range(hs_ref.shape[0]):
                h_ref[c * rows:(c + 1) * rows, kb * LANES:(kb + 1) * LANES] = (
                    hs_ref[kb, pl.ds(c, rows, stride=dil), :].astype(BF16))

    def store(which, col0, val):
        for c in range(dil):
            o_ref[which, c, :, col0:col0 + LANES] = val[c * rows:(c + 1) * rows].astype(BF16)

    for which in range(2):
        y = jnp.dot(h_ref[...], w_ref[which], preferred_element_type=F32)
        ga, gb = gain_ref[which, 0:1, :], gain_ref[which, 1:2, :]
        cos, sin = tab_ref[0], tab_ref[1]
        ca, sb, cb, sa = ga * cos, gb * sin, gb * cos, ga * sin
        for p in range(A_HEADS // 2):
            y2 = y[:, p * 2 * LANES:(p + 1) * 2 * LANES]
            t = y2[:, :LANES] * y2[:, :LANES] + y2[:, LANES:] * y2[:, LANES:]
            lo_lane = lax.broadcasted_iota(jnp.int32, t.shape, 1) < LANES // 2
            s_lo = jnp.sum(jnp.where(lo_lane, t, 0.0), axis=-1, keepdims=True)
            s_all = jnp.sum(t, axis=-1, keepdims=True)
            ms = jnp.where(lo_lane, s_lo, s_all - s_lo) * (1.0 / A_HEAD_DIM)
            rinv = lax.rsqrt(ms + EPS)
            ar = y2[:, :LANES] * rinv
            br = y2[:, LANES:] * rinv
            store(which, p * 2 * LANES, ar * ca - br * sb)
            store(which, p * 2 * LANES + LANES, br * cb + ar * sa)

    y = jnp.dot(h_ref[...], w_ref[2], preferred_element_type=F32)
    for hh in range(A_HEADS):
        store(2, hh * LANES, y[:, hh * LANES:(hh + 1) * LANES])


def _a_proj(x2d, nrm, w_all, g, gains, tables, batch, seq, dil, tm):
    T, D = x2d.shape
    spt = seq // tm
    rows = tm // dil
    return pl.pallas_call(
        functools.partial(_a_proj_kernel, dil=dil),
        grid=(T // tm,),
        in_specs=[
            pl.BlockSpec((tm, D), lambda i: (i, 0)),
            _resident(nrm.shape),
            pl.BlockSpec((None,) + w_all.shape[1:], lambda i: (g, 0, 0, 0), pipeline_mode=pl.Buffered(1)),
            _resident(gains.shape),
            pl.BlockSpec((2, tm, LANES), lambda i: (0, i % spt, 0)),
        ],
        out_specs=pl.BlockSpec((3, None, dil, rows, A_WIDTH), lambda i: (0, i // spt, 0, i % spt, 0)),
        out_shape=jax.ShapeDtypeStruct((3, batch, dil, seq // dil, A_WIDTH), BF16),
        scratch_shapes=[pltpu.VMEM((tm, D), BF16), pltpu.VMEM((D // LANES, tm, LANES), F32)],
        compiler_params=_params(("parallel",)),
        name=f"a_proj_d{dil}",
    )(x2d, nrm, w_all, gains, tables)


def _dil_attn_kernel(bound_ref, q_ref, k_ref, kp_ref, kn_ref, v_ref, vp_ref, vn_ref, hot_ref,
                     o_ref, st_ref, *, length, chunk, sub, half):
    i = pl.program_id(2)
    nk = sub + 2 * half

    def window(ref, prev_ref, next_ref, r, j, cols):
        lo, hi = j * sub - half, (j + 1) * sub + half
        parts = [prev_ref[r, :, cols]] if lo < 0 else []
        parts.append(ref[r, max(lo, 0):min(hi, chunk), cols])
        if hi > chunk:
            parts.append(next_ref[r, :, cols])
        return parts[0] if len(parts) == 1 else jnp.concatenate(parts, axis=0)

    qi = lax.broadcasted_iota(jnp.int32, (2 * sub, nk), 0) % sub
    kj = lax.broadcasted_iota(jnp.int32, (2 * sub, nk), 1)
    band = jnp.abs(qi + half - kj) <= half
    lane = lax.broadcasted_iota(jnp.int32, (sub, LANES), 1)
    lane2 = lax.broadcasted_iota(jnp.int32, (1, 2 * LANES), 1) % LANES
    first = (lane2 < LANES // 2).astype(BF16)
    second = (lane2 >= LANES // 2).astype(BF16)

    bound = bound_ref[0]

    def sub_block(r, j, use_bound):
        r0 = j * sub
        base = i * chunk + j * sub
        lo = half - base
        hi = length + half - base
        mask = band & (kj >= lo) & (kj < hi)
        bias = jnp.where(mask, -bound if use_bound else 0.0, NEG_FILL)
        stats = jnp.zeros((sub, LANES), F32)
        for p in range(A_HEADS // 2):
            cols = slice(p * 2 * LANES, (p + 1) * 2 * LANES)
            q2 = q_ref[r, pl.ds(r0, sub), cols]
            qq = jnp.concatenate([q2 * first, q2 * second], axis=0)
            s = lax.dot_general(qq, window(k_ref, kp_ref, kn_ref, r, j, cols), NT_DIMS,
                                preferred_element_type=F32) + bias
            if use_bound:
                eb = jnp.exp2(s).astype(BF16)
            else:
                ref = jnp.max(s, axis=-1, keepdims=True)
                e = jnp.exp2(s - ref)
                den = jnp.sum(e, axis=-1, keepdims=True)
                eb = e.astype(BF16)
                lse = (ref + jnp.log2(den)) * (1.0 / LOG2E)
            for t in range(2):
                h = 2 * p + t
                hc = slice(h * LANES, (h + 1) * LANES)
                rs = slice(t * sub, (t + 1) * sub)
                v = window(v_ref, vp_ref, vn_ref, r, j, hc)
                if use_bound:
                    ov = jnp.dot(eb[rs], jnp.concatenate([v, hot_ref[h]], axis=1), preferred_element_type=F32)
                    o_ref[r, pl.ds(r0, sub), hc] = ov[:, :LANES].astype(BF16)
                    stats = stats + ov[:, LANES:]
                else:
                    o = jnp.dot(eb[rs], v, preferred_element_type=F32) / den[rs]
                    o_ref[r, pl.ds(r0, sub), hc] = o.astype(BF16)
                    stats = jnp.where(lane == h, lse[rs], stats)
        if use_bound:
            l2 = jnp.log2(stats)
            stats = jnp.where(lane < A_HEADS, (bound + l2) * (1.0 / LOG2E), jnp.where(lane < 2 * A_HEADS, l2, 0.0))
        st_ref[r, pl.ds(r0, sub), :] = stats

    def run(use_bound):
        for r in range(q_ref.shape[0]):
            for j in range(chunk // sub):
                sub_block(r, j, use_bound)

    run(True)
    head_lanes = lax.broadcasted_iota(jnp.int32, st_ref.shape, 2) < A_HEADS
    lse_min = jnp.min(jnp.where(head_lanes, st_ref[...], jnp.inf), keepdims=True)
    trusted = jnp.min(lse_min * LOG2E - bound) >= math.log2(L_MIN)

    @pl.when(jnp.logical_not(trusted))
    def _():
        run(False)


def _dil_attn(qkv, bound, g, step_rows=1024, sub=128):
    window, dil = DIL_CONFIGS[g]
    half = window // (2 * dil)
    _, batch, _, length, _ = qkv.shape
    chunk = min(step_rows, length)
    rc = max(1, min(dil, step_rows // chunk))
    assert length % chunk == 0 and chunk % sub == 0 and chunk % half == 0 and dil % rc == 0
    cph = chunk // half
    last = length // half - 1

    def main(which):
        return pl.BlockSpec((None, None, rc, chunk, A_WIDTH), lambda b, c, i: (which, b, c, i, 0))

    def prev(which):
        return pl.BlockSpec((None, None, rc, half, A_WIDTH),
                            lambda b, c, i: (which, b, c, jnp.maximum(i * cph - 1, 0), 0))

    def nxt(which):
        return pl.BlockSpec((None, None, rc, half, A_WIDTH),
                            lambda b, c, i: (which, b, c, jnp.minimum((i + 1) * cph, last), 0))

    col = jnp.arange(LANES)[None, None, :]
    head = jnp.arange(A_HEADS)[:, None, None]
    hot = jnp.broadcast_to((col == head) | (col == head + A_HEADS), (A_HEADS, sub + 2 * half, LANES)).astype(BF16)
    kern = functools.partial(_dil_attn_kernel, length=length, chunk=chunk, sub=sub, half=half)
    return pl.pallas_call(
        kern,
        grid=(batch, dil // rc, length // chunk),
        in_specs=[pl.BlockSpec(memory_space=pltpu.SMEM), main(0), main(1), prev(1), nxt(1), main(2), prev(2), nxt(2),
                  _resident(hot.shape)],
        out_specs=[
            pl.BlockSpec((None, rc, chunk, A_WIDTH), lambda b, c, i: (b, c, i, 0)),
            pl.BlockSpec((None, rc, chunk, LANES), lambda b, c, i: (b, c, i, 0)),
        ],
        out_shape=[
            jax.ShapeDtypeStruct((batch, dil, length, A_WIDTH), BF16),
            jax.ShapeDtypeStruct((batch, dil, length, LANES), F32),
        ],
        compiler_params=_params(("parallel", "parallel", "parallel")),
        name=f"dil_attn_g{g}",
    )(bound, qkv, qkv, qkv, qkv, qkv, qkv, qkv, hot)


def _merge_tile(o_refs, l_refs, e_ref, out_ref, os_ref, ls_ref):
    nlb = os_ref.shape[1]
    natural = [o_refs[g].shape[0] == 1 for g in range(N_GROUPS)]
    for g in range(N_GROUPS):
        if natural[g]:
            continue
        dil, rows = o_refs[g].shape[0], o_refs[g].shape[1]
        for c in range(dil):
            oc = o_refs[g][c].astype(F32)
            for kb in range(nlb):
                os_ref[g, kb, pl.ds(c, rows, stride=dil), :] = oc[:, kb * LANES:(kb + 1) * LANES]
            ls_ref[g, pl.ds(c, rows, stride=dil), :] = l_refs[g][c]
    stats = [l_refs[g][0] if natural[g] else ls_ref[g] for g in range(N_GROUPS)]
    mx = jnp.maximum(jnp.maximum(stats[0], stats[1]), stats[2])
    e = [jnp.exp(st - mx) for st in stats]
    inv = 1.0 / (e[0] + e[1] + e[2])
    coef = []
    for g in range(N_GROUPS):
        pending = jnp.exp2(-pltpu.roll(stats[g], LANES - A_HEADS, 1))
        coef.append((e[g] * inv * pending).astype(BF16))
    for kb in range(nlb):
        cols = slice(kb * LANES, (kb + 1) * LANES)
        acc = None
        for g in range(N_GROUPS):
            w = jnp.dot(coef[g], e_ref[:, cols], preferred_element_type=F32)
            o_nat = o_refs[g][0, :, cols].astype(F32) if natural[g] else os_ref[g, kb]
            acc = w * o_nat if acc is None else acc + w * o_nat
        out_ref[:, cols] = acc.astype(BF16)


def _ffn_tile(x, a, wo_ref, nrm_ref, w1_ref, w2_ref, ff_chunk):
    x1 = x + jnp.dot(a, wo_ref[...], preferred_element_type=F32)
    h = (x1 * _rms_scale(x1, x1.shape[-1]) * nrm_ref[...]).astype(BF16)
    acc = x1
    for c in range(w1_ref.shape[1] // ff_chunk):
        cols = slice(c * ff_chunk, (c + 1) * ff_chunk)
        a = jnp.maximum(jnp.dot(h, w1_ref[:, cols], preferred_element_type=F32), 0.0)
        acc = acc + jnp.dot((a * a).astype(BF16), w2_ref[cols, :], preferred_element_type=F32)
    return acc


def _ffn_kernel(x_ref, a_ref, wo_ref, nrm_ref, w1_ref, w2_ref, out_ref, *, ff_chunk):
    out_ref[...] = _ffn_tile(x_ref[...], a_ref[...], wo_ref, nrm_ref, w1_ref, w2_ref, ff_chunk)


def _ffn(x2d, a2d, w_o, nrm, w1, w2, tm, ff_chunk=1024):
    T, D = x2d.shape
    return pl.pallas_call(
        functools.partial(_ffn_kernel, ff_chunk=ff_chunk),
        grid=(T // tm,),
        in_specs=[
            pl.BlockSpec((tm, D), lambda i: (i, 0)),
            pl.BlockSpec((tm, a2d.shape[1]), lambda i: (i, 0)),
            _resident(w_o.shape),
            _resident(nrm.shape),
            _resident(w1.shape),
            _resident(w2.shape),
        ],
        out_specs=pl.BlockSpec((tm, D), lambda i: (i, 0)),
        out_shape=jax.ShapeDtypeStruct((T, D), F32),
        compiler_params=_params(("parallel",)),
        name="outproj_ffn",
    )(x2d, a2d, w_o, nrm, w1, w2)


def _merge_kernel(o0_ref, o1_ref, o2_ref, l0_ref, l1_ref, l2_ref, e_ref, out_ref, os_ref, ls_ref):
    _merge_tile((o0_ref, o1_ref, o2_ref), (l0_ref, l1_ref, l2_ref), e_ref, out_ref, os_ref, ls_ref)


def _merge(outs, stats, seq, tm):
    batch = outs[0].shape[0]
    spt = seq // tm
    expand = (jnp.arange(LANES)[:, None] == (jnp.arange(A_WIDTH)[None, :] // A_HEAD_DIM)).astype(BF16)

    def spec(arr):
        dil, width = arr.shape[1], arr.shape[3]
        return pl.BlockSpec((None, dil, tm // dil, width), lambda i: (i // spt, 0, i % spt, 0))

    return pl.pallas_call(
        _merge_kernel,
        grid=(batch * spt,),
        in_specs=[spec(o) for o in outs] + [spec(l) for l in stats] + [_resident(expand.shape)],
        out_specs=pl.BlockSpec((tm, A_WIDTH), lambda i: (i, 0)),
        out_shape=jax.ShapeDtypeStruct((batch * seq, A_WIDTH), BF16),
        scratch_shapes=[pltpu.VMEM((N_GROUPS, A_WIDTH // LANES, tm, LANES), F32),
                        pltpu.VMEM((N_GROUPS, tm, LANES), F32)],
        compiler_params=_params(("parallel",)),
        name="a_merge",
    )(*outs, *stats, expand)


def _b_proj_kernel(x_ref, nrm_ref, win_ref, qag_ref, kvag_ref, wqb_ref, wkn_ref, wv_ref,
                   gq_ref, gk_ref, cq_ref, ck_ref, cosq_ref, cosk_ref, sin_ref,
                   q_ref, k_ref, v_ref):
    x = x_ref[...]
    h = (x * _rms_scale(x, x.shape[-1]) * nrm_ref[...]).astype(BF16)
    lat = jnp.dot(h, win_ref[...], preferred_element_type=F32)
    c_q = lat[:, :Q_LORA]
    c_kv = lat[:, Q_LORA:Q_LORA + KV_LORA]
    k_rope = lat[:, Q_LORA + KV_LORA:Q_LORA + KV_LORA + LANES]
    k_roll = lat[:, Q_LORA + KV_LORA + LANES:]
    cqn = (c_q * _rms_scale(c_q, Q_LORA) * qag_ref[...]).astype(BF16)
    ckvn = (c_kv * _rms_scale(c_kv, KV_LORA) * kvag_ref[...]).astype(BF16)
    sin = sin_ref[...]

    vt_all = lax.dot_general(wv_ref[...], ckvn, NT_DIMS, preferred_element_type=F32)
    ones_rows = (lax.broadcasted_iota(jnp.int32, (VT_ROWS - V_DIM, x.shape[0]), 0) == 0).astype(BF16)
    for hh in range(B_HEADS):
        v_ref[hh * VT_ROWS:hh * VT_ROWS + V_DIM, :] = vt_all[hh * V_DIM:(hh + 1) * V_DIM, :].astype(BF16)
        v_ref[hh * VT_ROWS + V_DIM:(hh + 1) * VT_ROWS, :] = ones_rows

    q3 = jnp.dot(cqn, wqb_ref[...], preferred_element_type=F32)
    gq_n = gq_ref[:, :LANES]
    cos_q = cosq_ref[...]
    for hh in range(B_HEADS):
        qn = q3[:, hh * Q3_PAD:hh * Q3_PAD + LANES]
        qr = q3[:, hh * Q3_PAD + LANES:hh * Q3_PAD + QK_PAD]
        rq = lax.rsqrt(jnp.sum(qn * qn + qr * qr, axis=-1, keepdims=True) * (1.0 / QK_DIM) + EPS)
        qr = qr * cos_q + q3[:, hh * Q3_PAD + QK_PAD:(hh + 1) * Q3_PAD] * sin
        q_ref[:, hh * QK_PAD:hh * QK_PAD + LANES] = (qn * rq * gq_n).astype(BF16)
        q_ref[:, hh * QK_PAD + LANES:(hh + 1) * QK_PAD] = (qr * rq + cq_ref[...]).astype(BF16)

    kn_all = jnp.dot(ckvn, wkn_ref[...], preferred_element_type=F32)
    ss_kr = jnp.sum(k_rope * k_rope, axis=-1, keepdims=True)
    kr = k_rope * cosk_ref[...] + k_roll * sin
    gk_n = gk_ref[:, :LANES]
    for hh in range(B_HEADS):
        kn = kn_all[:, hh * LANES:(hh + 1) * LANES]
        rk = lax.rsqrt((jnp.sum(kn * kn, axis=-1, keepdims=True) + ss_kr) * (1.0 / QK_DIM) + EPS)
        k_ref[:, hh * QK_PAD:hh * QK_PAD + LANES] = (kn * rk * gk_n).astype(BF16)
        k_ref[:, hh * QK_PAD + LANES:(hh + 1) * QK_PAD] = (kr * rk + ck_ref[...]).astype(BF16)


def _b_proj(x2d, consts, tables, seq, tm):
    T, D = x2d.shape
    spt = seq // tm
    row = lambda i: (i, 0)
    pos = pl.BlockSpec((tm, LANES), lambda i: (i % spt, 0))
    return pl.pallas_call(
        _b_proj_kernel,
        grid=(T // tm,),
        in_specs=[pl.BlockSpec((tm, D), row)] + [_resident(c.shape) for c in consts] + [pos] * len(tables),
        out_specs=[
            pl.BlockSpec((tm, B_HEADS * QK_PAD), row),
            pl.BlockSpec((tm, B_HEADS * QK_PAD), row),
            pl.BlockSpec((B_HEADS * VT_ROWS, tm), lambda i: (0, i)),
        ],
        out_shape=[
            jax.ShapeDtypeStruct((T, B_HEADS * QK_PAD), BF16),
            jax.ShapeDtypeStruct((T, B_HEADS * QK_PAD), BF16),
            jax.ShapeDtypeStruct((B_HEADS * VT_ROWS, T), BF16),
        ],
        compiler_params=_params(("parallel",)),
        name="b_proj",
    )(x2d, *consts, *tables)


def _mla_attn_kernel(q_ref, k_ref, vt_ref, o_ref, acc_ref, *, tk, unroll):
    q = q_ref[...]
    tq = q.shape[0]
    nkv = k_ref.shape[0] // tk

    def scores_t(kb):
        r0 = pl.multiple_of(kb * tk, tk)
        st = lax.dot_general(k_ref[pl.ds(r0, tk), :], q, NT_DIMS, preferred_element_type=F32)
        return st, vt_ref[:, pl.ds(r0, tk)]

    def fast(kb, carry):
        for u in range(unroll):
            st, vt = scores_t(kb * unroll + u)
            acc_ref[...] += jnp.dot(vt, jnp.exp2(st).astype(BF16), preferred_element_type=F32)
        return carry

    def finish():
        inv = 1.0 / acc_ref[V_DIM:V_DIM + 1, :]
        o_ref[...] = (acc_ref[:V_DIM, :] * inv).T.astype(BF16)

    acc_ref[...] = jnp.zeros_like(acc_ref)
    lax.fori_loop(0, nkv // unroll, fast, 0)
    trusted = jnp.min(acc_ref[V_DIM:V_DIM + 1, :]) >= L_MIN

    @pl.when(trusted)
    def _():
        finish()

    @pl.when(jnp.logical_not(trusted))
    def _():
        def slow(kb, m):
            st, vt = scores_t(kb)
            m_new = jnp.maximum(m, jnp.max(st, axis=0, keepdims=True))
            pt = jnp.exp2(st - m_new).astype(BF16)
            acc_ref[...] = jnp.exp2(m - m_new) * acc_ref[...] + jnp.dot(vt, pt, preferred_element_type=F32)
            return m_new

        acc_ref[...] = jnp.zeros_like(acc_ref)
        lax.fori_loop(0, nkv, slow, jnp.full((1, tq), -jnp.inf, F32))
        finish()


def _mla_attn(q, k, vt, batch, seq, tq=2048, tk=512, unroll=16):
    nq = seq // tq
    unroll = min(unroll, seq // tk)
    assert seq % (tk * unroll) == 0
    return pl.pallas_call(
        functools.partial(_mla_attn_kernel, tk=tk, unroll=unroll),
        grid=(batch, B_HEADS, nq),
        in_specs=[
            pl.BlockSpec((tq, QK_PAD), lambda b, h, i: (b * nq + i, h)),
            pl.BlockSpec((seq, QK_PAD), lambda b, h, i: (b, h)),
            pl.BlockSpec((VT_ROWS, seq), lambda b, h, i: (h, b)),
        ],
        out_specs=pl.BlockSpec((tq, V_DIM), lambda b, h, i: (b * nq + i, h)),
        out_shape=jax.ShapeDtypeStruct((batch * seq, B_HEADS * V_DIM), BF16),
        scratch_shapes=[pltpu.VMEM((VT_ROWS, tq), F32)],
        compiler_params=_params(("parallel", "parallel", "parallel")),
        name="mla_attn",
    )(q, k, vt)


def _rope_angles(seq, d):
    pos = jnp.arange(seq, dtype=F32)
    freqs = ROPE_THETA ** (-jnp.arange(0, d, 2, dtype=F32) / d)
    ang = pos[:, None] * freqs[None, :]
    return jnp.cos(ang), jnp.sin(ang)


def _spread_rope(t, axis):
    a, b = jnp.split(t, 2, axis=axis)
    z = jnp.zeros_like(a)
    return jnp.concatenate([a, z, b, z], axis=axis)


def _residue_major(table, tm, dil):
    s, w = table.shape
    return table.reshape(s // tm, tm // dil, dil, w).transpose(0, 2, 1, 3).reshape(s, w)


def kernel(x, norm_mix, norm_ffn, a_w_qkv, a_q_gain, a_k_gain, a_w_o, b_w_in, b_q_a_gain, b_w_qb,
           b_kv_a_gain, b_w_kvb, b_q_gain, b_k_gain, b_w_o, ffn_w1, ffn_w2):
    batch, seq, d_model = x.shape
    T = batch * seq
    x2d = x.reshape(T, d_model)

    tm_a = 1024
    cos, sin = _rope_angles(seq, A_HEAD_DIM)
    cos_a = jnp.concatenate([cos, cos], -1)
    sin_a = jnp.concatenate([sin, sin], -1)
    w_qkv = a_w_qkv[0].astype(BF16).reshape(d_model, 3, N_GROUPS, A_HEADS // 2, 2, 2, A_HEAD_DIM // 2)
    w_qkv = jnp.concatenate([
        w_qkv[:, :2].transpose(2, 1, 0, 3, 5, 4, 6).reshape(N_GROUPS, 2, d_model, A_WIDTH),
        w_qkv[:, 2:].transpose(2, 1, 0, 3, 4, 5, 6).reshape(N_GROUPS, 1, d_model, A_WIDTH)], axis=1)
    hd = A_HEAD_DIM // 2
    outs, lses = [], []
    for g, (_, dil) in enumerate(DIL_CONFIGS):
        tables = jnp.stack([_residue_major(cos_a, tm_a, dil), _residue_major(sin_a, tm_a, dil)])

        def pair_gain(gain):
            return jnp.stack([jnp.tile(gain[:hd], 2), jnp.tile(gain[hd:], 2)])

        q_scale = LOG2E / math.sqrt(A_HEAD_DIM)
        gains = jnp.stack([pair_gain(a_q_gain[0, g] * q_scale), pair_gain(a_k_gain[0, g])])
        bound = 1.02 * A_HEAD_DIM * q_scale * jnp.max(jnp.abs(a_q_gain[0, g])) * jnp.max(jnp.abs(a_k_gain[0, g]))
        qkv = _a_proj(x2d, norm_mix[0][None, :], w_qkv, g, gains, tables, batch, seq, dil, tm_a)
        o, lse = _dil_attn(qkv, jnp.reshape(bound, (1,)).astype(F32), g)
        outs.append(o)
        lses.append(lse)
    merged = _merge(outs, lses, seq, tm=512)
    x2d = _ffn(x2d, merged, a_w_o[0].astype(BF16), norm_ffn[0][None, :],
               ffn_w1[0].astype(BF16), ffn_w2[0].astype(BF16), tm=512)

    cos, sin = _rope_angles(seq, ROPE_DIM)
    cos_b = _spread_rope(jnp.concatenate([cos, cos], -1), -1)
    sin_b = _spread_rope(jnp.concatenate([-sin, sin], -1), -1)

    def head_gain(gain, scale):
        return (jnp.concatenate([gain[:NOPE_DIM], _spread_rope(gain[NOPE_DIM:], 0)]) * scale)[None, :]

    q_scale = LOG2E / math.sqrt(QK_DIM)
    gq = head_gain(b_q_gain[0], q_scale)
    gk = head_gain(b_k_gain[0], 1.0)
    bound = 1.02 * QK_DIM * q_scale * jnp.max(jnp.abs(b_q_gain[0])) * jnp.max(jnp.abs(b_k_gain[0]))
    const_lane = (jnp.arange(LANES) == CONST_LANE).astype(F32)[None, :]

    def half_roll(w, gain):
        return jnp.roll(w * gain, LANES // 2, axis=-1)

    w_in = b_w_in[0]
    k_rope_w = _spread_rope(w_in[:, Q_LORA + KV_LORA:], 1)
    w_in = jnp.concatenate([w_in[:, :Q_LORA + KV_LORA], k_rope_w, half_roll(k_rope_w, gk[:, LANES:])], 1)
    w_qb = b_w_qb[0].reshape(Q_LORA, B_HEADS, QK_DIM)
    q_rope_w = _spread_rope(w_qb[..., NOPE_DIM:], 2)
    w_qb = jnp.concatenate([w_qb[..., :NOPE_DIM], q_rope_w, half_roll(q_rope_w, gq[:, LANES:])], -1)
    w_qb = w_qb.reshape(Q_LORA, B_HEADS * Q3_PAD)
    w_kvb = b_w_kvb[0].reshape(KV_LORA, B_HEADS, NOPE_DIM + V_DIM)
    w_kn = w_kvb[..., :NOPE_DIM].reshape(KV_LORA, B_HEADS * NOPE_DIM)
    w_v = w_kvb[..., NOPE_DIM:].reshape(KV_LORA, B_HEADS * V_DIM).T

    consts = (norm_mix[1][None, :], w_in.astype(BF16), b_q_a_gain[0][None, :], b_kv_a_gain[0][None, :],
              w_qb.astype(BF16), w_kn.astype(BF16), w_v.astype(BF16), gq, gk,
              -bound * const_lane, const_lane)
    tables = (cos_b * gq[:, LANES:], cos_b * gk[:, LANES:], sin_b)
    q, k, v = _b_proj(x2d, consts, tables, seq, tm=512)
    o = _mla_attn(q, k, v, batch, seq)
    x2d = _ffn(x2d, o, b_w_o[0].astype(BF16), norm_ffn[1][None, :],
               ffn_w1[1].astype(BF16), ffn_w2[1].astype(BF16), tm=512)
    return x2d.reshape(batch, seq, d_model)
```

```python
import functools
import math

import jax
import jax.numpy as jnp
from jax import lax
from jax.experimental import pallas as pl
from jax.experimental.pallas import tpu as pltpu

EPS = 1e-6
ROPE_THETA = 10000.0
NEG_FILL = -1e30
LOG2E = math.log2(math.e)

DIL_CONFIGS = ((128, 1), (512, 4), (2048, 16))
N_GROUPS = len(DIL_CONFIGS)
A_HEADS = 8
A_HEAD_DIM = 128
A_WIDTH = A_HEADS * A_HEAD_DIM

B_HEADS = 8
Q_LORA = 256
KV_LORA = 128
NOPE_DIM = 128
ROPE_DIM = 64
V_DIM = 128
QK_DIM = NOPE_DIM + ROPE_DIM
QK_PAD = 256
Q3_PAD = 384
VT_ROWS = 144
CONST_LANE = 32
L_MIN = 2.0 ** -80

LANES = 128
VMEM_LIMIT = 48 * 1024 * 1024

BF16 = jnp.bfloat16
F32 = jnp.float32
NT_DIMS = (((1,), (1,)), ((), ()))


def _params(sem):
    return pltpu.CompilerParams(dimension_semantics=sem, vmem_limit_bytes=VMEM_LIMIT)


def _resident(shape):
    zeros = (0,) * len(shape)
    return pl.BlockSpec(shape, lambda *_: zeros, pipeline_mode=pl.Buffered(1))


def _rms_scale(x, width):
    return lax.rsqrt(jnp.sum(x * x, axis=-1, keepdims=True) * (1.0 / width) + EPS)


def _a_proj_kernel(x_ref, nrm_ref, w_ref, gain_ref, tab_ref, o_ref, h_ref, hs_ref, *, dil):
    tm = x_ref.shape[0]
    rows = tm // dil
    x = x_ref[...]
    hn = x * _rms_scale(x, x.shape[-1]) * nrm_ref[...]
    if dil == 1:
        h_ref[...] = hn.astype(BF16)
    else:
        for kb in range(hs_ref.shape[0]):
            hs_ref[kb] = hn[:, kb * LANES:(kb + 1) * LANES]
        for c in range(dil):
            for kb in range(hs_ref.shape[0]):
                h_ref[c * rows:(c + 1) * rows, kb * LANES:(kb + 1) * LANES] = (
                    hs_ref[kb, pl.ds(c, rows, stride=dil), :].astype(BF16))

    def store(which, col0, val):
        for c in range(dil):
            o_ref[which, c, :, col0:col0 + LANES] = val[c * rows:(c + 1) * rows].astype(BF16)

    for which in range(2):
        y = jnp.dot(h_ref[...], w_ref[which], preferred_element_type=F32)
        ga, gb = gain_ref[which, 0:1, :], gain_ref[which, 1:2, :]
        cos, sin = tab_ref[0], tab_ref[1]
        ca, sb, cb, sa = ga * cos, gb * sin, gb * cos, ga * sin
        for p in range(A_HEADS // 2):
            y2 = y[:, p * 2 * LANES:(p + 1) * 2 * LANES]
            t = y2[:, :LANES] * y2[:, :LANES] + y2[:, LANES:] * y2[:, LANES:]
            lo_lane = lax.broadcasted_iota(jnp.int32, t.shape, 1) < LANES // 2
            s_lo = jnp.sum(jnp.where(lo_lane, t, 0.0), axis=-1, keepdims=True)
            s_all = jnp.sum(t, axis=-1, keepdims=True)
            ms = jnp.where(lo_lane, s_lo, s_all - s_lo) * (1.0 / A_HEAD_DIM)
            rinv = lax.rsqrt(ms + EPS)
            ar = y2[:, :LANES] * rinv
            br = y2[:, LANES:] * rinv
            store(which, p * 2 * LANES, ar * ca - br * sb)
            store(which, p * 2 * LANES + LANES, br * cb + ar * sa)

    y = jnp.dot(h_ref[...], w_ref[2], preferred_element_type=F32)
    for hh in range(A_HEADS):
        store(2, hh * LANES, y[:, hh * LANES:(hh + 1) * LANES])


def _a_proj(x2d, nrm, w_all, g, gains, tables, batch, seq, dil, tm):
    T, D = x2d.shape
    spt = seq // tm
    rows = tm // dil
    return pl.pallas_call(
        functools.partial(_a_proj_kernel, dil=dil),
        grid=(T // tm,),
        in_specs=[
            pl.BlockSpec((tm, D), lambda i: (i, 0)),
            _resident(nrm.shape),
            pl.BlockSpec((None,) + w_all.shape[1:], lambda i: (g, 0, 0, 0), pipeline_mode=pl.Buffered(1)),
            _resident(gains.shape),
            pl.BlockSpec((2, tm, LANES), lambda i: (0, i % spt, 0)),
        ],
        out_specs=pl.BlockSpec((3, None, dil, rows, A_WIDTH), lambda i: (0, i // spt, 0, i % spt, 0)),
        out_shape=jax.ShapeDtypeStruct((3, batch, dil, seq // dil, A_WIDTH), BF16),
        scratch_shapes=[pltpu.VMEM((tm, D), BF16), pltpu.VMEM((D // LANES, tm, LANES), F32)],
        compiler_params=_params(("parallel",)),
        name=f"a_proj_d{dil}",
    )(x2d, nrm, w_all, gains, tables)


def _dil_attn_kernel(bound_ref, q_ref, k_ref, kp_ref, kn_ref, v_ref, vp_ref, vn_ref, hot_ref,
                     o_ref, st_ref, *, length, chunk, sub, half):
    i = pl.program_id(2)
    nk = sub + 2 * half

    def window(ref, prev_ref, next_ref, r, j, cols):
        lo, hi = j * sub - half, (j + 1) * sub + half
        parts = [prev_ref[r, :, cols]] if lo < 0 else []
        parts.append(ref[r, max(lo, 0):min(hi, chunk), cols])
        if hi > chunk:
            parts.append(next_ref[r, :, cols])
        return parts[0] if len(parts) == 1 else jnp.concatenate(parts, axis=0)

    qi = lax.broadcasted_iota(jnp.int32, (2 * sub, nk), 0) % sub
    kj = lax.broadcasted_iota(jnp.int32, (2 * sub, nk), 1)
    band = jnp.abs(qi + half - kj) <= half
    lane = lax.broadcasted_iota(jnp.int32, (sub, LANES), 1)
    lane2 = lax.broadcasted_iota(jnp.int32, (1, 2 * LANES), 1) % LANES
    first = (lane2 < LANES // 2).astype(BF16)
    second = (lane2 >= LANES // 2).astype(BF16)

    bound = bound_ref[0]

    def sub_block(r, j, use_bound):
        r0 = j * sub
        base = i * chunk + j * sub
        lo = half - base
        hi = length + half - base
        mask = band & (kj >= lo) & (kj < hi)
        bias = jnp.where(mask, -bound if use_bound else 0.0, NEG_FILL)
        stats = jnp.zeros((sub, LANES), F32)
        for p in range(A_HEADS // 2):
            cols = slice(p * 2 * LANES, (p + 1) * 2 * LANES)
            q2 = q_ref[r, pl.ds(r0, sub), cols]
            qq = jnp.concatenate([q2 * first, q2 * second], axis=0)
            s = lax.dot_general(qq, window(k_ref, kp_ref, kn_ref, r, j, cols), NT_DIMS,
                                preferred_element_type=F32) + bias
            if use_bound:
                eb = jnp.exp2(s).astype(BF16)
            else:
                ref = jnp.max(s, axis=-1, keepdims=True)
                e = jnp.exp2(s - ref)
                den = jnp.sum(e, axis=-1, keepdims=True)
                eb = e.astype(BF16)
                lse = (ref + jnp.log2(den)) * (1.0 / LOG2E)
            for t in range(2):
                h = 2 * p + t
                hc = slice(h * LANES, (h + 1) * LANES)
                rs = slice(t * sub, (t + 1) * sub)
                v = window(v_ref, vp_ref, vn_ref, r, j, hc)
                if use_bound:
                    ov = jnp.dot(eb[rs], jnp.concatenate([v, hot_ref[h]], axis=1), preferred_element_type=F32)
                    o_ref[r, pl.ds(r0, sub), hc] = ov[:, :LANES].astype(BF16)
                    stats = stats + ov[:, LANES:]
                else:
                    o = jnp.dot(eb[rs], v, preferred_element_type=F32) / den[rs]
                    o_ref[r, pl.ds(r0, sub), hc] = o.astype(BF16)
                    stats = jnp.where(lane == h, lse[rs], stats)
        if use_bound:
            l2 = jnp.log2(stats)
            stats = jnp.where(lane < A_HEADS, (bound + l2) * (1.0 / LOG2E), jnp.where(lane < 2 * A_HEADS, l2, 0.0))
        st_ref[r, pl.ds(r0, sub), :] = stats

    def run(use_bound):
        for r in range(q_ref.shape[0]):
            for j in range(chunk // sub):
                sub_block(r, j, use_bound)

    run(True)
    head_lanes = lax.broadcasted_iota(jnp.int32, st_ref.shape, 2) < A_HEADS
    lse_min = jnp.min(jnp.where(head_lanes, st_ref[...], jnp.inf), keepdims=True)
    trusted = jnp.min(lse_min * LOG2E - bound) >= math.log2(L_MIN)

    @pl.when(jnp.logical_not(trusted))
    def _():
        run(False)


def _dil_attn(qkv, bound, g, step_rows=1024, sub=128):
    window, dil = DIL_CONFIGS[g]
    half = window // (2 * dil)
    _, batch, _, length, _ = qkv.shape
    chunk = min(step_rows, length)
    rc = max(1, min(dil, step_rows // chunk))
    assert length % chunk == 0 and chunk % sub == 0 and chunk % half == 0 and dil % rc == 0
    cph = chunk // half
    last = length // half - 1

    def main(which):
        return pl.BlockSpec((None, None, rc, chunk, A_WIDTH), lambda b, c, i: (which, b, c, i, 0))

    def prev(which):
        return pl.BlockSpec((None, None, rc, half, A_WIDTH),
                            lambda b, c, i: (which, b, c, jnp.maximum(i * cph - 1, 0), 0))

    def nxt(which):
        return pl.BlockSpec((None, None, rc, half, A_WIDTH),
                            lambda b, c, i: (which, b, c, jnp.minimum((i + 1) * cph, last), 0))

    col = jnp.arange(LANES)[None, None, :]
    head = jnp.arange(A_HEADS)[:, None, None]
    hot = jnp.broadcast_to((col == head) | (col == head + A_HEADS), (A_HEADS, sub + 2 * half, LANES)).astype(BF16)
    kern = functools.partial(_dil_attn_kernel, length=length, chunk=chunk, sub=sub, half=half)
    return pl.pallas_call(
        kern,
        grid=(batch, dil // rc, length // chunk),
        in_specs=[pl.BlockSpec(memory_space=pltpu.SMEM), main(0), main(1), prev(1), nxt(1), main(2), prev(2), nxt(2),
                  _resident(hot.shape)],
        out_specs=[
            pl.BlockSpec((None, rc, chunk, A_WIDTH), lambda b, c, i: (b, c, i, 0)),
            pl.BlockSpec((None, rc, chunk, LANES), lambda b, c, i: (b, c, i, 0)),
        ],
        out_shape=[
            jax.ShapeDtypeStruct((batch, dil, length, A_WIDTH), BF16),
            jax.ShapeDtypeStruct((batch, dil, length, LANES), F32),
        ],
        compiler_params=_params(("parallel", "parallel", "parallel")),
        name=f"dil_attn_g{g}",
    )(bound, qkv, qkv, qkv, qkv, qkv, qkv, qkv, hot)


def _merge_tile(o_refs, l_refs, e_ref, out_ref, os_ref, ls_ref):
    nlb = os_ref.shape[1]
    natural = [o_refs[g].shape[0] == 1 for g in range(N_GROUPS)]
    for g in range(N_GROUPS):
        if natural[g]:
            continue
        dil, rows = o_refs[g].shape[0], o_refs[g].shape[1]
        for c in range(dil):
            oc = o_refs[g][c].astype(F32)
            for kb in range(nlb):
                os_ref[g, kb, pl.ds(c, rows, stride=dil), :] = oc[:, kb * LANES:(kb + 1) * LANES]
            ls_ref[g, pl.ds(c, rows, stride=dil), :] = l_refs[g][c]
    stats = [l_refs[g][0] if natural[g] else ls_ref[g] for g in range(N_GROUPS)]
    mx = jnp.maximum(jnp.maximum(stats[0], stats[1]), stats[2])
    e = [jnp.exp(st - mx) for st in stats]
    inv = 1.0 / (e[0] + e[1] + e[2])
    coef = []
    for g in range(N_GROUPS):
        pending = jnp.exp2(-pltpu.roll(stats[g], LANES - A_HEADS, 1))
        coef.append((e[g] * inv * pending).astype(BF16))
    for kb in range(nlb):
        cols = slice(kb * LANES, (kb + 1) * LANES)
        acc = None
        for g in range(N_GROUPS):
            w = jnp.dot(coef[g], e_ref[:, cols], preferred_element_type=F32)
            o_nat = o_refs[g][0, :, cols].astype(F32) if natural[g] else os_ref[g, kb]
            acc = w * o_nat if acc is None else acc + w * o_nat
        out_ref[:, cols] = acc.astype(BF16)


def _ffn_tile(x, a, wo_ref, nrm_ref, w1_ref, w2_ref, ff_chunk):
    x1 = x + jnp.dot(a, wo_ref[...], preferred_element_type=F32)
    h = (x1 * _rms_scale(x1, x1.shape[-1]) * nrm_ref[...]).astype(BF16)
    acc = x1
    for c in range(w1_ref.shape[1] // ff_chunk):
        cols = slice(c * ff_chunk, (c + 1) * ff_chunk)
        a = jnp.maximum(jnp.dot(h, w1_ref[:, cols], preferred_element_type=F32), 0.0)
        acc = acc + jnp.dot((a * a).astype(BF16), w2_ref[cols, :], preferred_element_type=F32)
    return acc


def _ffn_kernel(x_ref, a_ref, wo_ref, nrm_ref, w1_ref, w2_ref, out_ref, *, ff_chunk):
    out_ref[...] = _ffn_tile(x_ref[...], a_ref[...], wo_ref, nrm_ref, w1_ref, w2_ref, ff_chunk)


def _ffn(x2d, a2d, w_o, nrm, w1, w2, tm, ff_chunk=1024):
    T, D = x2d.shape
    return pl.pallas_call(
        functools.partial(_ffn_kernel, ff_chunk=ff_chunk),
        grid=(T // tm,),
        in_specs=[
            pl.BlockSpec((tm, D), lambda i: (i, 0)),
            pl.BlockSpec((tm, a2d.shape[1]), lambda i: (i, 0)),
            _resident(w_o.shape),
            _resident(nrm.shape),
            _resident(w1.shape),
            _resident(w2.shape),
        ],
        out_specs=pl.BlockSpec((tm, D), lambda i: (i, 0)),
        out_shape=jax.ShapeDtypeStruct((T, D), F32),
        compiler_params=_params(("parallel",)),
        name="outproj_ffn",
    )(x2d, a2d, w_o, nrm, w1, w2)


def _merge_kernel(o0_ref, o1_ref, o2_ref, l0_ref, l1_ref, l2_ref, e_ref, out_ref, os_ref, ls_ref):
    _merge_tile((o0_ref, o1_ref, o2_ref), (l0_ref, l1_ref, l2_ref), e_ref, out_ref, os_ref, ls_ref)


def _merge(outs, stats, seq, tm):
    batch = outs[0].shape[0]
    spt = seq // tm
    expand = (jnp.arange(LANES)[:, None] == (jnp.arange(A_WIDTH)[None, :] // A_HEAD_DIM)).astype(BF16)

    def spec(arr):
        dil, width = arr.shape[1], arr.shape[3]
        return pl.BlockSpec((None, dil, tm // dil, width), lambda i: (i // spt, 0, i % spt, 0))

    return pl.pallas_call(
        _merge_kernel,
        grid=(batch * spt,),
        in_specs=[spec(o) for o in outs] + [spec(l) for l in stats] + [_resident(expand.shape)],
        out_specs=pl.BlockSpec((tm, A_WIDTH), lambda i: (i, 0)),
        out_shape=jax.ShapeDtypeStruct((batch * seq, A_WIDTH), BF16),
        scratch_shapes=[pltpu.VMEM((N_GROUPS, A_WIDTH // LANES, tm, LANES), F32),
                        pltpu.VMEM((N_GROUPS, tm, LANES), F32)],
        compiler_params=_params(("parallel",)),
        name="a_merge",
    )(*outs, *stats, expand)


def _b_proj_kernel(x_ref, nrm_ref, win_ref, qag_ref, kvag_ref, wqb_ref, wkn_ref, wv_ref,
                   gq_ref, gk_ref, cq_ref, ck_ref, cosq_ref, cosk_ref, sin_ref,
                   q_ref, k_ref, v_ref):
    x = x_ref[...]
    h = (x * _rms_scale(x, x.shape[-1]) * nrm_ref[...]).astype(BF16)
    lat = jnp.dot(h, win_ref[...], preferred_element_type=F32)
    c_q = lat[:, :Q_LORA]
    c_kv = lat[:, Q_LORA:Q_LORA + KV_LORA]
    k_rope = lat[:, Q_LORA + KV_LORA:Q_LORA + KV_LORA + LANES]
    k_roll = lat[:, Q_LORA + KV_LORA + LANES:]
    cqn = (c_q * _rms_scale(c_q, Q_LORA) * qag_ref[...]).astype(BF16)
    ckvn = (c_kv * _rms_scale(c_kv, KV_LORA) * kvag_ref[...]).astype(BF16)
    sin = sin_ref[...]

    vt_all = lax.dot_general(wv_ref[...], ckvn, NT_DIMS, preferred_element_type=F32)
    ones_rows = (lax.broadcasted_iota(jnp.int32, (VT_ROWS - V_DIM, x.shape[0]), 0) == 0).astype(BF16)
    for hh in range(B_HEADS):
        v_ref[hh * VT_ROWS:hh * VT_ROWS + V_DIM, :] = vt_all[hh * V_DIM:(hh + 1) * V_DIM, :].astype(BF16)
        v_ref[hh * VT_ROWS + V_DIM:(hh + 1) * VT_ROWS, :] = ones_rows

    q3 = jnp.dot(cqn, wqb_ref[...], preferred_element_type=F32)
    gq_n = gq_ref[:, :LANES]
    cos_q = cosq_ref[...]
    for hh in range(B_HEADS):
        qn = q3[:, hh * Q3_PAD:hh * Q3_PAD + LANES]
        qr = q3[:, hh * Q3_PAD + LANES:hh * Q3_PAD + QK_PAD]
        rq = lax.rsqrt(jnp.sum(qn * qn + qr * qr, axis=-1, keepdims=True) * (1.0 / QK_DIM) + EPS)
        qr = qr * cos_q + q3[:, hh * Q3_PAD + QK_PAD:(hh + 1) * Q3_PAD] * sin
        q_ref[:, hh * QK_PAD:hh * QK_PAD + LANES] = (qn * rq * gq_n).astype(BF16)
        q_ref[:, hh * QK_PAD + LANES:(hh + 1) * QK_PAD] = (qr * rq + cq_ref[...]).astype(BF16)

    kn_all = jnp.dot(ckvn, wkn_ref[...], preferred_element_type=F32)
    ss_kr = jnp.sum(k_rope * k_rope, axis=-1, keepdims=True)
    kr = k_rope * cosk_ref[...] + k_roll * sin
    gk_n = gk_ref[:, :LANES]
    for hh in range(B_HEADS):
        kn = kn_all[:, hh * LANES:(hh + 1) * LANES]
        rk = lax.rsqrt((jnp.sum(kn * kn, axis=-1, keepdims=True) + ss_kr) * (1.0 / QK_DIM) + EPS)
        k_ref[:, hh * QK_PAD:hh * QK_PAD + LANES] = (kn * rk * gk_n).astype(BF16)
        k_ref[:, hh * QK_PAD + LANES:(hh + 1) * QK_PAD] = (kr * rk + ck_ref[...]).astype(BF16)


def _b_proj(x2d, consts, tables, seq, tm):
    T, D = x2d.shape
    spt = seq // tm
    row = lambda i: (i, 0)
    pos = pl.BlockSpec((tm, LANES), lambda i: (i % spt, 0))
    return pl.pallas_call(
        _b_proj_kernel,
        grid=(T // tm,),
        in_specs=[pl.BlockSpec((tm, D), row)] + [_resident(c.shape) for c in consts] + [pos] * len(tables),
        out_specs=[
            pl.BlockSpec((tm, B_HEADS * QK_PAD), row),
            pl.BlockSpec((tm, B_HEADS * QK_PAD), row),
            pl.BlockSpec((B_HEADS * VT_ROWS, tm), lambda i: (0, i)),
        ],
        out_shape=[
            jax.ShapeDtypeStruct((T, B_HEADS * QK_PAD), BF16),
            jax.ShapeDtypeStruct((T, B_HEADS * QK_PAD), BF16),
            jax.ShapeDtypeStruct((B_HEADS * VT_ROWS, T), BF16),
        ],
        compiler_params=_params(("parallel",)),
        name="b_proj",
    )(x2d, *consts, *tables)


def _mla_attn_kernel(q_ref, k_ref, vt_ref, o_ref, acc_ref, *, tk):
    parts, _, tp = acc_ref.shape
    nkv = k_ref.shape[0] // tk

    def scores_t(part, r0):
        st = lax.dot_general(k_ref[pl.ds(r0, tk), :], q_ref[part * tp:(part + 1) * tp, :], NT_DIMS,
                             preferred_element_type=F32)
        return st, vt_ref[:, pl.ds(r0, tk)]

    def finish(part):
        inv = 1.0 / acc_ref[part, V_DIM:V_DIM + 1, :]
        o_ref[part * tp:(part + 1) * tp, :] = (acc_ref[part, :V_DIM, :] * inv).T.astype(BF16)

    for part in range(parts):
        acc_ref[part] = jnp.zeros(acc_ref.shape[1:], F32)
        for kb in range(nkv):
            st, vt = scores_t(part, kb * tk)
            acc_ref[part] += jnp.dot(vt, jnp.exp2(st).astype(BF16), preferred_element_type=F32)
        finish(part)
    trusted = jnp.min(acc_ref[:, V_DIM:V_DIM + 1, :]) >= L_MIN

    @pl.when(jnp.logical_not(trusted))
    def _():
        for part in range(parts):
            def slow(kb, m):
                st, vt = scores_t(part, pl.multiple_of(kb * tk, tk))
                m_new = jnp.maximum(m, jnp.max(st, axis=0, keepdims=True))
                pt = jnp.exp2(st - m_new).astype(BF16)
                acc_ref[part] = (jnp.exp2(m - m_new) * acc_ref[part]
                                 + jnp.dot(vt, pt, preferred_element_type=F32))
                return m_new

            acc_ref[part] = jnp.zeros(acc_ref.shape[1:], F32)
            lax.fori_loop(0, nkv, slow, jnp.full((1, tp), -jnp.inf, F32))
            finish(part)


def _mla_attn(q, k, vt, batch, seq, tq=2048, tk=1024, parts=1):
    nq = seq // tq
    tk = min(tk, seq)
    assert seq % tk == 0 and tq % parts == 0
    return pl.pallas_call(
        functools.partial(_mla_attn_kernel, tk=tk),
        grid=(batch, B_HEADS, nq),
        in_specs=[
            pl.BlockSpec((tq, QK_PAD), lambda b, h, i: (b * nq + i, h)),
            pl.BlockSpec((seq, QK_PAD), lambda b, h, i: (b, h)),
            pl.BlockSpec((VT_ROWS, seq), lambda b, h, i: (h, b)),
        ],
        out_specs=pl.BlockSpec((tq, V_DIM), lambda b, h, i: (b * nq + i, h)),
        out_shape=jax.ShapeDtypeStruct((batch * seq, B_HEADS * V_DIM), BF16),
        scratch_shapes=[pltpu.VMEM((parts, VT_ROWS, tq // parts), F32)],
        compiler_params=_params(("parallel", "parallel", "parallel")),
        name="mla_attn",
    )(q, k, vt)


def _rope_angles(seq, d):
    pos = jnp.arange(seq, dtype=F32)
    freqs = ROPE_THETA ** (-jnp.arange(0, d, 2, dtype=F32) / d)
    ang = pos[:, None] * freqs[None, :]
    return jnp.cos(ang), jnp.sin(ang)


def _spread_rope(t, axis):
    a, b = jnp.split(t, 2, axis=axis)
    z = jnp.zeros_like(a)
    return jnp.concatenate([a, z, b, z], axis=axis)


def _residue_major(table, tm, dil):
    s, w = table.shape
    return table.reshape(s // tm, tm // dil, dil, w).transpose(0, 2, 1, 3).reshape(s, w)


def kernel(x, norm_mix, norm_ffn, a_w_qkv, a_q_gain, a_k_gain, a_w_o, b_w_in, b_q_a_gain, b_w_qb,
           b_kv_a_gain, b_w_kvb, b_q_gain, b_k_gain, b_w_o, ffn_w1, ffn_w2):
    batch, seq, d_model = x.shape
    T = batch * seq
    x2d = x.reshape(T, d_model)

    tm_a = 1024
    cos, sin = _rope_angles(seq, A_HEAD_DIM)
    cos_a = jnp.concatenate([cos, cos], -1)
    sin_a = jnp.concatenate([sin, sin], -1)
    w_qkv = a_w_qkv[0].astype(BF16).reshape(d_model, 3, N_GROUPS, A_HEADS // 2, 2, 2, A_HEAD_DIM // 2)
    w_qkv = jnp.concatenate([
        w_qkv[:, :2].transpose(2, 1, 0, 3, 5, 4, 6).reshape(N_GROUPS, 2, d_model, A_WIDTH),
        w_qkv[:, 2:].transpose(2, 1, 0, 3, 4, 5, 6).reshape(N_GROUPS, 1, d_model, A_WIDTH)], axis=1)
    hd = A_HEAD_DIM // 2
    outs, lses = [], []
    for g, (_, dil) in enumerate(DIL_CONFIGS):
        tables = jnp.stack([_residue_major(cos_a, tm_a, dil), _residue_major(sin_a, tm_a, dil)])

        def pair_gain(gain):
            return jnp.stack([jnp.tile(gain[:hd], 2), jnp.tile(gain[hd:], 2)])

        q_scale = LOG2E / math.sqrt(A_HEAD_DIM)
        gains = jnp.stack([pair_gain(a_q_gain[0, g] * q_scale), pair_gain(a_k_gain[0, g])])
        bound = 1.02 * A_HEAD_DIM * q_scale * jnp.max(jnp.abs(a_q_gain[0, g])) * jnp.max(jnp.abs(a_k_gain[0, g]))
        qkv = _a_proj(x2d, norm_mix[0][None, :], w_qkv, g, gains, tables, batch, seq, dil, tm_a)
        o, lse = _dil_attn(qkv, jnp.reshape(bound, (1,)).astype(F32), g)
        outs.append(o)
        lses.append(lse)
    merged = _merge(outs, lses, seq, tm=512)
    x2d = _ffn(x2d, merged, a_w_o[0].astype(BF16), norm_ffn[0][None, :],
               ffn_w1[0].astype(BF16), ffn_w2[0].astype(BF16), tm=512)

    cos, sin = _rope_angles(seq, ROPE_DIM)
    cos_b = _spread_rope(jnp.concatenate([cos, cos], -1), -1)
    sin_b = _spread_rope(jnp.concatenate([-sin, sin], -1), -1)

    def head_gain(gain, scale):
        return (jnp.concatenate([gain[:NOPE_DIM], _spread_rope(gain[NOPE_DIM:], 0)]) * scale)[None, :]

    q_scale = LOG2E / math.sqrt(QK_DIM)
    gq = head_gain(b_q_gain[0], q_scale)
    gk = head_gain(b_k_gain[0], 1.0)
    bound = 1.02 * QK_DIM * q_scale * jnp.max(jnp.abs(b_q_gain[0])) * jnp.max(jnp.abs(b_k_gain[0]))
    const_lane = (jnp.arange(LANES) == CONST_LANE).astype(F32)[None, :]

    def half_roll(w, gain):
        return jnp.roll(w * gain, LANES // 2, axis=-1)

    w_in = b_w_in[0]
    k_rope_w = _spread_rope(w_in[:, Q_LORA + KV_LORA:], 1)
    w_in = jnp.concatenate([w_in[:, :Q_LORA + KV_LORA], k_rope_w, half_roll(k_rope_w, gk[:, LANES:])], 1)
    w_qb = b_w_qb[0].reshape(Q_LORA, B_HEADS, QK_DIM)
    q_rope_w = _spread_rope(w_qb[..., NOPE_DIM:], 2)
    w_qb = jnp.concatenate([w_qb[..., :NOPE_DIM], q_rope_w, half_roll(q_rope_w, gq[:, LANES:])], -1)
    w_qb = w_qb.reshape(Q_LORA, B_HEADS * Q3_PAD)
    w_kvb = b_w_kvb[0].reshape(KV_LORA, B_HEADS, NOPE_DIM + V_DIM)
    w_kn = w_kvb[..., :NOPE_DIM].reshape(KV_LORA, B_HEADS * NOPE_DIM)
    w_v = w_kvb[..., NOPE_DIM:].reshape(KV_LORA, B_HEADS * V_DIM).T

    consts = (norm_mix[1][None, :], w_in.astype(BF16), b_q_a_gain[0][None, :], b_kv_a_gain[0][None, :],
              w_qb.astype(BF16), w_kn.astype(BF16), w_v.astype(BF16), gq, gk,
              -bound * const_lane, const_lane)
    tables = (cos_b * gq[:, LANES:], cos_b * gk[:, LANES:], sin_b)
    q, k, v = _b_proj(x2d, consts, tables, seq, tm=512)
    o = _mla_attn(q, k, v, batch, seq)
    x2d = _ffn(x2d, o, b_w_o[0].astype(BF16), norm_ffn[1][None, :],
               ffn_w1[1].astype(BF16), ffn_w2[1].astype(BF16), tm=512)
    return x2d.reshape(batch, seq, d_model)
```

```python
import functools
import math

import jax
import jax.numpy as jnp
from jax import lax
from jax.experimental import pallas as pl
from jax.experimental.pallas import tpu as pltpu

EPS = 1e-6
ROPE_THETA = 10000.0
NEG_FILL = -1e30
LOG2E = math.log2(math.e)

DIL_CONFIGS = ((128, 1), (512, 4), (2048, 16))
N_GROUPS = len(DIL_CONFIGS)
A_HEADS = 8
A_HEAD_DIM = 128
A_WIDTH = A_HEADS * A_HEAD_DIM

B_HEADS = 8
Q_LORA = 256
KV_LORA = 128
NOPE_DIM = 128
ROPE_DIM = 64
V_DIM = 128
QK_DIM = NOPE_DIM + ROPE_DIM
QK_PAD = 256
Q3_PAD = 384
VT_ROWS = 144
CONST_LANE = 32
L_MIN = 2.0 ** -80

LANES = 128
VMEM_LIMIT = 48 * 1024 * 1024

BF16 = jnp.bfloat16
F32 = jnp.float32
NT_DIMS = (((1,), (1,)), ((), ()))


def _params(sem):
    return pltpu.CompilerParams(dimension_semantics=sem, vmem_limit_bytes=VMEM_LIMIT)


def _resident(shape):
    zeros = (0,) * len(shape)
    return pl.BlockSpec(shape, lambda *_: zeros, pipeline_mode=pl.Buffered(1))


def _rms_scale(x, width):
    return lax.rsqrt(jnp.sum(x * x, axis=-1, keepdims=True) * (1.0 / width) + EPS)


def _a_proj_kernel(x_ref, nrm_ref, w_ref, gain_ref, tab_ref, o_ref, h_ref, hs_ref, *, dil):
    tm = x_ref.shape[0]
    rows = tm // dil
    x = x_ref[...]
    hn = x * _rms_scale(x, x.shape[-1]) * nrm_ref[...]
    if dil == 1:
        h_ref[...] = hn.astype(BF16)
    else:
        for kb in range(hs_ref.shape[0]):
            hs_ref[kb] = hn[:, kb * LANES:(kb + 1) * LANES]
        for c in range(dil):
            for kb in range(hs_ref.shape[0]):
                h_ref[c * rows:(c + 1) * rows, kb * LANES:(kb + 1) * LANES] = (
                    hs_ref[kb, pl.ds(c, rows, stride=dil), :].astype(BF16))

    def store(which, col0, val):
        for c in range(dil):
            o_ref[which, c, :, col0:col0 + LANES] = val[c * rows:(c + 1) * rows].astype(BF16)

    for which in range(2):
        y = jnp.dot(h_ref[...], w_ref[which], preferred_element_type=F32)
        ga, gb = gain_ref[which, 0:1, :], gain_ref[which, 1:2, :]
        cos, sin = tab_ref[0], tab_ref[1]
        ca, sb, cb, sa = ga * cos, gb * sin, gb * cos, ga * sin
        for p in range(A_HEADS // 2):
            y2 = y[:, p * 2 * LANES:(p + 1) * 2 * LANES]
            t = y2[:, :LANES] * y2[:, :LANES] + y2[:, LANES:] * y2[:, LANES:]
            lo_lane = lax.broadcasted_iota(jnp.int32, t.shape, 1) < LANES // 2
            s_lo = jnp.sum(jnp.where(lo_lane, t, 0.0), axis=-1, keepdims=True)
            s_all = jnp.sum(t, axis=-1, keepdims=True)
            ms = jnp.where(lo_lane, s_lo, s_all - s_lo) * (1.0 / A_HEAD_DIM)
            rinv = lax.rsqrt(ms + EPS)
            ar = y2[:, :LANES] * rinv
            br = y2[:, LANES:] * rinv
            store(which, p * 2 * LANES, ar * ca - br * sb)
            store(which, p * 2 * LANES + LANES, br * cb + ar * sa)

    y = jnp.dot(h_ref[...], w_ref[2], preferred_element_type=F32)
    for hh in range(A_HEADS):
        store(2, hh * LANES, y[:, hh * LANES:(hh + 1) * LANES])


def _a_proj(x2d, nrm, w_all, g, gains, tables, batch, seq, dil, tm):
    T, D = x2d.shape
    spt = seq // tm
    rows = tm // dil
    return pl.pallas_call(
        functools.partial(_a_proj_kernel, dil=dil),
        grid=(T // tm,),
        in_specs=[
            pl.BlockSpec((tm, D), lambda i: (i, 0)),
            _resident(nrm.shape),
            pl.BlockSpec((None,) + w_all.shape[1:], lambda i: (g, 0, 0, 0), pipeline_mode=pl.Buffered(1)),
            _resident(gains.shape),
            pl.BlockSpec((2, tm, LANES), lambda i: (0, i % spt, 0)),
        ],
        out_specs=pl.BlockSpec((3, None, dil, rows, A_WIDTH), lambda i: (0, i // spt, 0, i % spt, 0)),
        out_shape=jax.ShapeDtypeStruct((3, batch, dil, seq // dil, A_WIDTH), BF16),
        scratch_shapes=[pltpu.VMEM((tm, D), BF16), pltpu.VMEM((D // LANES, tm, LANES), F32)],
        compiler_params=_params(("parallel",)),
        name=f"a_proj_d{dil}",
    )(x2d, nrm, w_all, gains, tables)


def _dil_attn_kernel(bound_ref, q_ref, k_ref, kp_ref, kn_ref, v_ref, vp_ref, vn_ref, hot_ref,
                     o_ref, st_ref, *, length, chunk, sub, half):
    i = pl.program_id(2)
    nk = sub + 2 * half

    def window(ref, prev_ref, next_ref, r, j, cols):
        lo, hi = j * sub - half, (j + 1) * sub + half
        parts = [prev_ref[r, :, cols]] if lo < 0 else []
        parts.append(ref[r, max(lo, 0):min(hi, chunk), cols])
        if hi > chunk:
            parts.append(next_ref[r, :, cols])
        return parts[0] if len(parts) == 1 else jnp.concatenate(parts, axis=0)

    qi = lax.broadcasted_iota(jnp.int32, (2 * sub, nk), 0) % sub
    kj = lax.broadcasted_iota(jnp.int32, (2 * sub, nk), 1)
    band = jnp.abs(qi + half - kj) <= half
    lane = lax.broadcasted_iota(jnp.int32, (sub, LANES), 1)
    lane2 = lax.broadcasted_iota(jnp.int32, (1, 2 * LANES), 1) % LANES
    first = (lane2 < LANES // 2).astype(BF16)
    second = (lane2 >= LANES // 2).astype(BF16)

    bound = bound_ref[0]

    def sub_block(r, j, use_bound):
        r0 = j * sub
        base = i * chunk + j * sub
        lo = half - base
        hi = length + half - base
        mask = band & (kj >= lo) & (kj < hi)
        bias = jnp.where(mask, -bound if use_bound else 0.0, NEG_FILL)
        stats = jnp.zeros((sub, LANES), F32)
        for p in range(A_HEADS // 2):
            cols = slice(p * 2 * LANES, (p + 1) * 2 * LANES)
            q2 = q_ref[r, pl.ds(r0, sub), cols]
            qq = jnp.concatenate([q2 * first, q2 * second], axis=0)
            s = lax.dot_general(qq, window(k_ref, kp_ref, kn_ref, r, j, cols), NT_DIMS,
                                preferred_element_type=F32) + bias
            if use_bound:
                eb = jnp.exp2(s).astype(BF16)
            else:
                ref = jnp.max(s, axis=-1, keepdims=True)
                e = jnp.exp2(s - ref)
                den = jnp.sum(e, axis=-1, keepdims=True)
                eb = e.astype(BF16)
                lse = (ref + jnp.log2(den)) * (1.0 / LOG2E)
            for t in range(2):
                h = 2 * p + t
                hc = slice(h * LANES, (h + 1) * LANES)
                rs = slice(t * sub, (t + 1) * sub)
                v = window(v_ref, vp_ref, vn_ref, r, j, hc)
                if use_bound:
                    ov = jnp.dot(eb[rs], jnp.concatenate([v, hot_ref[h]], axis=1), preferred_element_type=F32)
                    o_ref[r, pl.ds(r0, sub), hc] = ov[:, :LANES].astype(BF16)
                    stats = stats + ov[:, LANES:]
                else:
                    o = jnp.dot(eb[rs], v, preferred_element_type=F32) / den[rs]
                    o_ref[r, pl.ds(r0, sub), hc] = o.astype(BF16)
                    stats = jnp.where(lane == h, lse[rs], stats)
        if use_bound:
            l2 = jnp.log2(stats)
            stats = jnp.where(lane < A_HEADS, (bound + l2) * (1.0 / LOG2E), jnp.where(lane < 2 * A_HEADS, l2, 0.0))
        st_ref[r, pl.ds(r0, sub), :] = stats

    def run(use_bound):
        for r in range(q_ref.shape[0]):
            for j in range(chunk // sub):
                sub_block(r, j, use_bound)

    run(True)
    head_lanes = lax.broadcasted_iota(jnp.int32, st_ref.shape, 2) < A_HEADS
    lse_min = jnp.min(jnp.where(head_lanes, st_ref[...], jnp.inf), keepdims=True)
    trusted = jnp.min(lse_min * LOG2E - bound) >= math.log2(L_MIN)

    @pl.when(jnp.logical_not(trusted))
    def _():
        run(False)


def _dil_attn(qkv, bound, g, step_rows=2048, sub=128):
    window, dil = DIL_CONFIGS[g]
    half = window // (2 * dil)
    _, batch, _, length, _ = qkv.shape
    chunk = min(step_rows, length)
    rc = max(1, min(dil, step_rows // chunk))
    assert length % chunk == 0 and chunk % sub == 0 and chunk % half == 0 and dil % rc == 0
    cph = chunk // half
    last = length // half - 1

    def main(which):
        return pl.BlockSpec((None, None, rc, chunk, A_WIDTH), lambda b, c, i: (which, b, c, i, 0))

    def prev(which):
        return pl.BlockSpec((None, None, rc, half, A_WIDTH),
                            lambda b, c, i: (which, b, c, jnp.maximum(i * cph - 1, 0), 0))

    def nxt(which):
        return pl.BlockSpec((None, None, rc, half, A_WIDTH),
                            lambda b, c, i: (which, b, c, jnp.minimum((i + 1) * cph, last), 0))

    col = jnp.arange(LANES)[None, None, :]
    head = jnp.arange(A_HEADS)[:, None, None]
    hot = jnp.broadcast_to((col == head) | (col == head + A_HEADS), (A_HEADS, sub + 2 * half, LANES)).astype(BF16)
    kern = functools.partial(_dil_attn_kernel, length=length, chunk=chunk, sub=sub, half=half)
    return pl.pallas_call(
        kern,
        grid=(batch, dil // rc, length // chunk),
        in_specs=[pl.BlockSpec(memory_space=pltpu.SMEM), main(0), main(1), prev(1), nxt(1), main(2), prev(2), nxt(2),
                  _resident(hot.shape)],
        out_specs=[
            pl.BlockSpec((None, rc, chunk, A_WIDTH), lambda b, c, i: (b, c, i, 0)),
            pl.BlockSpec((None, rc, chunk, LANES), lambda b, c, i: (b, c, i, 0)),
        ],
        out_shape=[
            jax.ShapeDtypeStruct((batch, dil, length, A_WIDTH), BF16),
            jax.ShapeDtypeStruct((batch, dil, length, LANES), F32),
        ],
        compiler_params=_params(("parallel", "parallel", "parallel")),
        name=f"dil_attn_g{g}",
    )(bound, qkv, qkv, qkv, qkv, qkv, qkv, qkv, hot)


def _merge_tile(o_refs, l_refs, e_ref, out_ref, os_ref, ls_ref):
    nlb = os_ref.shape[1]
    natural = [o_refs[g].shape[0] == 1 for g in range(N_GROUPS)]
    for g in range(N_GROUPS):
        if natural[g]:
            continue
        dil, rows = o_refs[g].shape[0], o_refs[g].shape[1]
        for c in range(dil):
            oc = o_refs[g][c].astype(F32)
            for kb in range(nlb):
                os_ref[g, kb, pl.ds(c, rows, stride=dil), :] = oc[:, kb * LANES:(kb + 1) * LANES]
            ls_ref[g, pl.ds(c, rows, stride=dil), :] = l_refs[g][c]
    stats = [l_refs[g][0] if natural[g] else ls_ref[g] for g in range(N_GROUPS)]
    mx = jnp.maximum(jnp.maximum(stats[0], stats[1]), stats[2])
    e = [jnp.exp(st - mx) for st in stats]
    inv = 1.0 / (e[0] + e[1] + e[2])
    coef = []
    for g in range(N_GROUPS):
        pending = jnp.exp2(-pltpu.roll(stats[g], LANES - A_HEADS, 1))
        coef.append((e[g] * inv * pending).astype(BF16))
    for kb in range(nlb):
        cols = slice(kb * LANES, (kb + 1) * LANES)
        acc = None
        for g in range(N_GROUPS):
            w = jnp.dot(coef[g], e_ref[:, cols], preferred_element_type=F32)
            o_nat = o_refs[g][0, :, cols].astype(F32) if natural[g] else os_ref[g, kb]
            acc = w * o_nat if acc is None else acc + w * o_nat
        out_ref[:, cols] = acc.astype(BF16)


def _ffn_tile(x, a, wo_ref, nrm_ref, w1_ref, w2_ref, ff_chunk):
    x1 = x + jnp.dot(a, wo_ref[...], preferred_element_type=F32)
    h = (x1 * _rms_scale(x1, x1.shape[-1]) * nrm_ref[...]).astype(BF16)
    acc = x1
    for c in range(w1_ref.shape[1] // ff_chunk):
        cols = slice(c * ff_chunk, (c + 1) * ff_chunk)
        a = jnp.maximum(jnp.dot(h, w1_ref[:, cols], preferred_element_type=F32), 0.0)
        acc = acc + jnp.dot((a * a).astype(BF16), w2_ref[cols, :], preferred_element_type=F32)
    return acc


def _ffn_kernel(x_ref, a_ref, wo_ref, nrm_ref, w1_ref, w2_ref, out_ref, *, ff_chunk):
    out_ref[...] = _ffn_tile(x_ref[...], a_ref[...], wo_ref, nrm_ref, w1_ref, w2_ref, ff_chunk)


def _ffn(x2d, a2d, w_o, nrm, w1, w2, tm, ff_chunk=1024):
    T, D = x2d.shape
    return pl.pallas_call(
        functools.partial(_ffn_kernel, ff_chunk=ff_chunk),
        grid=(T // tm,),
        in_specs=[
            pl.BlockSpec((tm, D), lambda i: (i, 0)),
            pl.BlockSpec((tm, a2d.shape[1]), lambda i: (i, 0)),
            _resident(w_o.shape),
            _resident(nrm.shape),
            _resident(w1.shape),
            _resident(w2.shape),
        ],
        out_specs=pl.BlockSpec((tm, D), lambda i: (i, 0)),
        out_shape=jax.ShapeDtypeStruct((T, D), F32),
        compiler_params=_params(("parallel",)),
        name="outproj_ffn",
    )(x2d, a2d, w_o, nrm, w1, w2)


def _merge_kernel(o0_ref, o1_ref, o2_ref, l0_ref, l1_ref, l2_ref, e_ref, out_ref, os_ref, ls_ref):
    _merge_tile((o0_ref, o1_ref, o2_ref), (l0_ref, l1_ref, l2_ref), e_ref, out_ref, os_ref, ls_ref)


def _merge(outs, stats, seq, tm):
    batch = outs[0].shape[0]
    spt = seq // tm
    expand = (jnp.arange(LANES)[:, None] == (jnp.arange(A_WIDTH)[None, :] // A_HEAD_DIM)).astype(BF16)

    def spec(arr):
        dil, width = arr.shape[1], arr.shape[3]
        return pl.BlockSpec((None, dil, tm // dil, width), lambda i: (i // spt, 0, i % spt, 0))

    return pl.pallas_call(
        _merge_kernel,
        grid=(batch * spt,),
        in_specs=[spec(o) for o in outs] + [spec(l) for l in stats] + [_resident(expand.shape)],
        out_specs=pl.BlockSpec((tm, A_WIDTH), lambda i: (i, 0)),
        out_shape=jax.ShapeDtypeStruct((batch * seq, A_WIDTH), BF16),
        scratch_shapes=[pltpu.VMEM((N_GROUPS, A_WIDTH // LANES, tm, LANES), F32),
                        pltpu.VMEM((N_GROUPS, tm, LANES), F32)],
        compiler_params=_params(("parallel",)),
        name="a_merge",
    )(*outs, *stats, expand)


def _b_proj_kernel(x_ref, nrm_ref, win_ref, qag_ref, kvag_ref, wqb_ref, wkn_ref, wv_ref,
                   gq_ref, gk_ref, cq_ref, ck_ref, cosq_ref, cosk_ref, sin_ref,
                   q_ref, k_ref, v_ref):
    x = x_ref[...]
    h = (x * _rms_scale(x, x.shape[-1]) * nrm_ref[...]).astype(BF16)
    lat = jnp.dot(h, win_ref[...], preferred_element_type=F32)
    c_q = lat[:, :Q_LORA]
    c_kv = lat[:, Q_LORA:Q_LORA + KV_LORA]
    k_rope = lat[:, Q_LORA + KV_LORA:Q_LORA + KV_LORA + LANES]
    k_roll = lat[:, Q_LORA + KV_LORA + LANES:]
    cqn = (c_q * _rms_scale(c_q, Q_LORA) * qag_ref[...]).astype(BF16)
    ckvn = (c_kv * _rms_scale(c_kv, KV_LORA) * kvag_ref[...]).astype(BF16)
    sin = sin_ref[...]

    vt_all = lax.dot_general(wv_ref[...], ckvn, NT_DIMS, preferred_element_type=F32)
    ones_rows = (lax.broadcasted_iota(jnp.int32, (VT_ROWS - V_DIM, x.shape[0]), 0) == 0).astype(BF16)
    for hh in range(B_HEADS):
        v_ref[hh * VT_ROWS:hh * VT_ROWS + V_DIM, :] = vt_all[hh * V_DIM:(hh + 1) * V_DIM, :].astype(BF16)
        v_ref[hh * VT_ROWS + V_DIM:(hh + 1) * VT_ROWS, :] = ones_rows

    q3 = jnp.dot(cqn, wqb_ref[...], preferred_element_type=F32)
    gq_n = gq_ref[:, :LANES]
    cos_q = cosq_ref[...]
    for hh in range(B_HEADS):
        qn = q3[:, hh * Q3_PAD:hh * Q3_PAD + LANES]
        qr = q3[:, hh * Q3_PAD + LANES:hh * Q3_PAD + QK_PAD]
        rq = lax.rsqrt(jnp.sum(qn * qn + qr * qr, axis=-1, keepdims=True) * (1.0 / QK_DIM) + EPS)
        qr = qr * cos_q + q3[:, hh * Q3_PAD + QK_PAD:(hh + 1) * Q3_PAD] * sin
        q_ref[:, hh * QK_PAD:hh * QK_PAD + LANES] = (qn * rq * gq_n).astype(BF16)
        q_ref[:, hh * QK_PAD + LANES:(hh + 1) * QK_PAD] = (qr * rq + cq_ref[...]).astype(BF16)

    kn_all = jnp.dot(ckvn, wkn_ref[...], preferred_element_type=F32)
    ss_kr = jnp.sum(k_rope * k_rope, axis=-1, keepdims=True)
    kr = k_rope * cosk_ref[...] + k_roll * sin
    gk_n = gk_ref[:, :LANES]
    for hh in range(B_HEADS):
        kn = kn_all[:, hh * LANES:(hh + 1) * LANES]
        rk = lax.rsqrt((jnp.sum(kn * kn, axis=-1, keepdims=True) + ss_kr) * (1.0 / QK_DIM) + EPS)
        k_ref[:, hh * QK_PAD:hh * QK_PAD + LANES] = (kn * rk * gk_n).astype(BF16)
        k_ref[:, hh * QK_PAD + LANES:(hh + 1) * QK_PAD] = (kr * rk + ck_ref[...]).astype(BF16)


def _b_proj(x2d, consts, tables, seq, tm):
    T, D = x2d.shape
    spt = seq // tm
    row = lambda i: (i, 0)
    pos = pl.BlockSpec((tm, LANES), lambda i: (i % spt, 0))
    return pl.pallas_call(
        _b_proj_kernel,
        grid=(T // tm,),
        in_specs=[pl.BlockSpec((tm, D), row)] + [_resident(c.shape) for c in consts] + [pos] * len(tables),
        out_specs=[
            pl.BlockSpec((tm, B_HEADS * QK_PAD), row),
            pl.BlockSpec((tm, B_HEADS * QK_PAD), row),
            pl.BlockSpec((B_HEADS * VT_ROWS, tm), lambda i: (0, i)),
        ],
        out_shape=[
            jax.ShapeDtypeStruct((T, B_HEADS * QK_PAD), BF16),
            jax.ShapeDtypeStruct((T, B_HEADS * QK_PAD), BF16),
            jax.ShapeDtypeStruct((B_HEADS * VT_ROWS, T), BF16),
        ],
        compiler_params=_params(("parallel",)),
        name="b_proj",
    )(x2d, *consts, *tables)


def _mla_attn_kernel(q_ref, k_ref, vt_ref, o_ref, acc_ref, *, tk):
    parts, _, tp = acc_ref.shape
    nkv = k_ref.shape[0] // tk

    def scores_t(part, r0):
        st = lax.dot_general(k_ref[pl.ds(r0, tk), :], q_ref[part * tp:(part + 1) * tp, :], NT_DIMS,
                             preferred_element_type=F32)
        return st, vt_ref[:, pl.ds(r0, tk)]

    def finish(part):
        inv = 1.0 / acc_ref[part, V_DIM:V_DIM + 1, :]
        o_ref[part * tp:(part + 1) * tp, :] = (acc_ref[part, :V_DIM, :] * inv).T.astype(BF16)

    for part in range(parts):
        acc_ref[part] = jnp.zeros(acc_ref.shape[1:], F32)
        for kb in range(nkv):
            st, vt = scores_t(part, kb * tk)
            acc_ref[part] += jnp.dot(vt, jnp.exp2(st).astype(BF16), preferred_element_type=F32)
        finish(part)
    trusted = jnp.min(acc_ref[:, V_DIM:V_DIM + 1, :]) >= L_MIN

    @pl.when(jnp.logical_not(trusted))
    def _():
        for part in range(parts):
            def slow(kb, m):
                st, vt = scores_t(part, pl.multiple_of(kb * tk, tk))
                m_new = jnp.maximum(m, jnp.max(st, axis=0, keepdims=True))
                pt = jnp.exp2(st - m_new).astype(BF16)
                acc_ref[part] = (jnp.exp2(m - m_new) * acc_ref[part]
                                 + jnp.dot(vt, pt, preferred_element_type=F32))
                return m_new

            acc_ref[part] = jnp.zeros(acc_ref.shape[1:], F32)
            lax.fori_loop(0, nkv, slow, jnp.full((1, tp), -jnp.inf, F32))
            finish(part)


def _mla_attn(q, k, vt, batch, seq, tq=2048, tk=1024, parts=1):
    nq = seq // tq
    tk = min(tk, seq)
    assert seq % tk == 0 and tq % parts == 0
    return pl.pallas_call(
        functools.partial(_mla_attn_kernel, tk=tk),
        grid=(batch, B_HEADS, nq),
        in_specs=[
            pl.BlockSpec((tq, QK_PAD), lambda b, h, i: (b * nq + i, h)),
            pl.BlockSpec((seq, QK_PAD), lambda b, h, i: (b, h)),
            pl.BlockSpec((VT_ROWS, seq), lambda b, h, i: (h, b)),
        ],
        out_specs=pl.BlockSpec((tq, V_DIM), lambda b, h, i: (b * nq + i, h)),
        out_shape=jax.ShapeDtypeStruct((batch * seq, B_HEADS * V_DIM), BF16),
        scratch_shapes=[pltpu.VMEM((parts, VT_ROWS, tq // parts), F32)],
        compiler_params=_params(("parallel", "parallel", "parallel")),
        name="mla_attn",
    )(q, k, vt)


def _rope_angles(seq, d):
    pos = jnp.arange(seq, dtype=F32)
    freqs = ROPE_THETA ** (-jnp.arange(0, d, 2, dtype=F32) / d)
    ang = pos[:, None] * freqs[None, :]
    return jnp.cos(ang), jnp.sin(ang)


def _spread_rope(t, axis):
    a, b = jnp.split(t, 2, axis=axis)
    z = jnp.zeros_like(a)
    return jnp.concatenate([a, z, b, z], axis=axis)


def _residue_major(table, tm, dil):
    s, w = table.shape
    return table.reshape(s // tm, tm // dil, dil, w).transpose(0, 2, 1, 3).reshape(s, w)


def kernel(x, norm_mix, norm_ffn, a_w_qkv, a_q_gain, a_k_gain, a_w_o, b_w_in, b_q_a_gain, b_w_qb,
           b_kv_a_gain, b_w_kvb, b_q_gain, b_k_gain, b_w_o, ffn_w1, ffn_w2):
    batch, seq, d_model = x.shape
    T = batch * seq
    x2d = x.reshape(T, d_model)

    tm_a = 1024
    cos, sin = _rope_angles(seq, A_HEAD_DIM)
    cos_a = jnp.concatenate([cos, cos], -1)
    sin_a = jnp.concatenate([sin, sin], -1)
    w_qkv = a_w_qkv[0].astype(BF16).reshape(d_model, 3, N_GROUPS, A_HEADS // 2, 2, 2, A_HEAD_DIM // 2)
    w_qkv = jnp.concatenate([
        w_qkv[:, :2].transpose(2, 1, 0, 3, 5, 4, 6).reshape(N_GROUPS, 2, d_model, A_WIDTH),
        w_qkv[:, 2:].transpose(2, 1, 0, 3, 4, 5, 6).reshape(N_GROUPS, 1, d_model, A_WIDTH)], axis=1)
    hd = A_HEAD_DIM // 2
    outs, lses = [], []
    for g, (_, dil) in enumerate(DIL_CONFIGS):
        tables = jnp.stack([_residue_major(cos_a, tm_a, dil), _residue_major(sin_a, tm_a, dil)])

        def pair_gain(gain):
            return jnp.stack([jnp.tile(gain[:hd], 2), jnp.tile(gain[hd:], 2)])

        q_scale = LOG2E / math.sqrt(A_HEAD_DIM)
        gains = jnp.stack([pair_gain(a_q_gain[0, g] * q_scale), pair_gain(a_k_gain[0, g])])
        bound = 1.02 * A_HEAD_DIM * q_scale * jnp.max(jnp.abs(a_q_gain[0, g])) * jnp.max(jnp.abs(a_k_gain[0, g]))
        qkv = _a_proj(x2d, norm_mix[0][None, :], w_qkv, g, gains, tables, batch, seq, dil, tm_a)
        o, lse = _dil_attn(qkv, jnp.reshape(bound, (1,)).astype(F32), g)
        outs.append(o)
        lses.append(lse)
    merged = _merge(outs, lses, seq, tm=512)
    x2d = _ffn(x2d, merged, a_w_o[0].astype(BF16), norm_ffn[0][None, :],
               ffn_w1[0].astype(BF16), ffn_w2[0].astype(BF16), tm=512)

    cos, sin = _rope_angles(seq, ROPE_DIM)
    cos_b = _spread_rope(jnp.concatenate([cos, cos], -1), -1)
    sin_b = _spread_rope(jnp.concatenate([-sin, sin], -1), -1)

    def head_gain(gain, scale):
        return (jnp.concatenate([gain[:NOPE_DIM], _spread_rope(gain[NOPE_DIM:], 0)]) * scale)[None, :]

    q_scale = LOG2E / math.sqrt(QK_DIM)
    gq = head_gain(b_q_gain[0], q_scale)
    gk = head_gain(b_k_gain[0], 1.0)
    bound = 1.02 * QK_DIM * q_scale * jnp.max(jnp.abs(b_q_gain[0])) * jnp.max(jnp.abs(b_k_gain[0]))
    const_lane = (jnp.arange(LANES) == CONST_LANE).astype(F32)[None, :]

    def half_roll(w, gain):
        return jnp.roll(w * gain, LANES // 2, axis=-1)

    w_in = b_w_in[0]
    k_rope_w = _spread_rope(w_in[:, Q_LORA + KV_LORA:], 1)
    w_in = jnp.concatenate([w_in[:, :Q_LORA + KV_LORA], k_rope_w, half_roll(k_rope_w, gk[:, LANES:])], 1)
    w_qb = b_w_qb[0].reshape(Q_LORA, B_HEADS, QK_DIM)
    q_rope_w = _spread_rope(w_qb[..., NOPE_DIM:], 2)
    w_qb = jnp.concatenate([w_qb[..., :NOPE_DIM], q_rope_w, half_roll(q_rope_w, gq[:, LANES:])], -1)
    w_qb = w_qb.reshape(Q_LORA, B_HEADS * Q3_PAD)
    w_kvb = b_w_kvb[0].reshape(KV_LORA, B_HEADS, NOPE_DIM + V_DIM)
    w_kn = w_kvb[..., :NOPE_DIM].reshape(KV_LORA, B_HEADS * NOPE_DIM)
    w_v = w_kvb[..., NOPE_DIM:].reshape(KV_LORA, B_HEADS * V_DIM).T

    consts = (norm_mix[1][None, :], w_in.astype(BF16), b_q_a_gain[0][None, :], b_kv_a_gain[0][None, :],
              w_qb.astype(BF16), w_kn.astype(BF16), w_v.astype(BF16), gq, gk,
              -bound * const_lane, const_lane)
    tables = (cos_b * gq[:, LANES:], cos_b * gk[:, LANES:], sin_b)
    q, k, v = _b_proj(x2d, consts, tables, seq, tm=512)
    o = _mla_attn(q, k, v, batch, seq)
    x2d = _ffn(x2d, o, b_w_o[0].astype(BF16), norm_ffn[1][None, :],
               ffn_w1[1].astype(BF16), ffn_w2[1].astype(BF16), tm=512)
    return x2d.reshape(batch, seq, d_model)
```

```python
import functools
import math

import jax
import jax.numpy as jnp
from jax import lax
from jax.experimental import pallas as pl
from jax.experimental.pallas import tpu as pltpu

EPS = 1e-6
ROPE_THETA = 10000.0
NEG_FILL = -1e30
LOG2E = math.log2(math.e)

DIL_CONFIGS = ((128, 1), (512, 4), (2048, 16))
N_GROUPS = len(DIL_CONFIGS)
A_HEADS = 8
A_HEAD_DIM = 128
A_WIDTH = A_HEADS * A_HEAD_DIM

B_HEADS = 8
Q_LORA = 256
KV_LORA = 128
NOPE_DIM = 128
ROPE_DIM = 64
V_DIM = 128
QK_DIM = NOPE_DIM + ROPE_DIM
QK_PAD = 256
Q3_PAD = 384
VT_ROWS = 144
CONST_LANE = 32
L_MIN = 2.0 ** -80

LANES = 128
VMEM_LIMIT = 48 * 1024 * 1024

BF16 = jnp.bfloat16
F32 = jnp.float32
NT_DIMS = (((1,), (1,)), ((), ()))


def _params(sem):
    return pltpu.CompilerParams(dimension_semantics=sem, vmem_limit_bytes=VMEM_LIMIT)


def _resident(shape):
    zeros = (0,) * len(shape)
    return pl.BlockSpec(shape, lambda *_: zeros, pipeline_mode=pl.Buffered(1))


def _rms_scale(x, width):
    return lax.rsqrt(jnp.sum(x * x, axis=-1, keepdims=True) * (1.0 / width) + EPS)


def _a_proj_kernel(x_ref, nrm_ref, w_ref, gain_ref, tab_ref, o_ref, h_ref, hs_ref=None, *, dil):
    tm = x_ref.shape[0]
    rows = tm // dil
    x = x_ref[...]
    hn = x * _rms_scale(x, x.shape[-1]) * nrm_ref[...]
    if dil == 1:
        h_ref[...] = hn.astype(BF16)
    else:
        for kb in range(hs_ref.shape[0]):
            hs_ref[kb] = hn[:, kb * LANES:(kb + 1) * LANES]
        for c in range(dil):
            for kb in range(hs_ref.shape[0]):
                h_ref[c * rows:(c + 1) * rows, kb * LANES:(kb + 1) * LANES] = (
                    hs_ref[kb, pl.ds(c, rows, stride=dil), :].astype(BF16))

    def store(which, col0, val):
        for c in range(dil):
            o_ref[which, c, :, col0:col0 + LANES] = val[c * rows:(c + 1) * rows].astype(BF16)

    for which in range(2):
        y = jnp.dot(h_ref[...], w_ref[which], preferred_element_type=F32)
        ga, gb = gain_ref[which, 0:1, :], gain_ref[which, 1:2, :]
        cos, sin = tab_ref[0], tab_ref[1]
        ca, sb, cb, sa = ga * cos, gb * sin, gb * cos, ga * sin
        for p in range(A_HEADS // 2):
            y2 = y[:, p * 2 * LANES:(p + 1) * 2 * LANES]
            t = y2[:, :LANES] * y2[:, :LANES] + y2[:, LANES:] * y2[:, LANES:]
            lo_lane = lax.broadcasted_iota(jnp.int32, t.shape, 1) < LANES // 2
            s_lo = jnp.sum(jnp.where(lo_lane, t, 0.0), axis=-1, keepdims=True)
            s_all = jnp.sum(t, axis=-1, keepdims=True)
            ms = jnp.where(lo_lane, s_lo, s_all - s_lo) * (1.0 / A_HEAD_DIM)
            rinv = lax.rsqrt(ms + EPS)
            ar = y2[:, :LANES] * rinv
            br = y2[:, LANES:] * rinv
            store(which, p * 2 * LANES, ar * ca - br * sb)
            store(which, p * 2 * LANES + LANES, br * cb + ar * sa)

    y = jnp.dot(h_ref[...], w_ref[2], preferred_element_type=F32)
    for hh in range(A_HEADS):
        store(2, hh * LANES, y[:, hh * LANES:(hh + 1) * LANES])


def _a_proj(x2d, nrm, w_all, g, gains, tables, batch, seq, dil, tm):
    T, D = x2d.shape
    spt = seq // tm
    rows = tm // dil
    return pl.pallas_call(
        functools.partial(_a_proj_kernel, dil=dil),
        grid=(T // tm,),
        in_specs=[
            pl.BlockSpec((tm, D), lambda i: (i, 0)),
            _resident(nrm.shape),
            pl.BlockSpec((None,) + w_all.shape[1:], lambda i: (g, 0, 0, 0), pipeline_mode=pl.Buffered(1)),
            _resident(gains.shape),
            pl.BlockSpec((2, tm, LANES), lambda i: (0, i % spt, 0)),
        ],
        out_specs=pl.BlockSpec((3, None, dil, rows, A_WIDTH), lambda i: (0, i // spt, 0, i % spt, 0)),
        out_shape=jax.ShapeDtypeStruct((3, batch, dil, seq // dil, A_WIDTH), BF16),
        scratch_shapes=[pltpu.VMEM((tm, D), BF16)] + ([pltpu.VMEM((D // LANES, tm, LANES), F32)] if dil > 1 else []),
        compiler_params=_params(("parallel",)),
        name=f"a_proj_d{dil}",
    )(x2d, nrm, w_all, gains, tables)


def _dil_attn_kernel(bound_ref, q_ref, k_ref, kp_ref, kn_ref, v_ref, vp_ref, vn_ref, hot_ref,
                     o_ref, st_ref, *, length, chunk, sub, half):
    i = pl.program_id(2)
    nk = sub + 2 * half

    def window(ref, prev_ref, next_ref, r, j, cols):
        lo, hi = j * sub - half, (j + 1) * sub + half
        parts = [prev_ref[r, :, cols]] if lo < 0 else []
        parts.append(ref[r, max(lo, 0):min(hi, chunk), cols])
        if hi > chunk:
            parts.append(next_ref[r, :, cols])
        return parts[0] if len(parts) == 1 else jnp.concatenate(parts, axis=0)

    qi = lax.broadcasted_iota(jnp.int32, (2 * sub, nk), 0) % sub
    kj = lax.broadcasted_iota(jnp.int32, (2 * sub, nk), 1)
    band = jnp.abs(qi + half - kj) <= half
    lane = lax.broadcasted_iota(jnp.int32, (sub, LANES), 1)
    lane2 = lax.broadcasted_iota(jnp.int32, (1, 2 * LANES), 1) % LANES
    first = (lane2 < LANES // 2).astype(BF16)
    second = (lane2 >= LANES // 2).astype(BF16)

    bound = bound_ref[0]

    def sub_block(r, j, use_bound):
        r0 = j * sub
        base = i * chunk + j * sub
        lo = half - base
        hi = length + half - base
        mask = band & (kj >= lo) & (kj < hi)
        bias = jnp.where(mask, -bound if use_bound else 0.0, NEG_FILL)
        stats = jnp.zeros((sub, LANES), F32)
        for p in range(A_HEADS // 2):
            cols = slice(p * 2 * LANES, (p + 1) * 2 * LANES)
            q2 = q_ref[r, pl.ds(r0, sub), cols]
            qq = jnp.concatenate([q2 * first, q2 * second], axis=0)
            s = lax.dot_general(qq, window(k_ref, kp_ref, kn_ref, r, j, cols), NT_DIMS,
                                preferred_element_type=F32) + bias
            if use_bound:
                eb = jnp.exp2(s).astype(BF16)
            else:
                ref = jnp.max(s, axis=-1, keepdims=True)
                e = jnp.exp2(s - ref)
                den = jnp.sum(e, axis=-1, keepdims=True)
                eb = e.astype(BF16)
                lse = (ref + jnp.log2(den)) * (1.0 / LOG2E)
            for t in range(2):
                h = 2 * p + t
                hc = slice(h * LANES, (h + 1) * LANES)
                rs = slice(t * sub, (t + 1) * sub)
                v = window(v_ref, vp_ref, vn_ref, r, j, hc)
                if use_bound:
                    ov = jnp.dot(eb[rs], jnp.concatenate([v, hot_ref[h]], axis=1), preferred_element_type=F32)
                    o_ref[r, pl.ds(r0, sub), hc] = ov[:, :LANES].astype(BF16)
                    stats = stats + ov[:, LANES:]
                else:
                    o = jnp.dot(eb[rs], v, preferred_element_type=F32) / den[rs]
                    o_ref[r, pl.ds(r0, sub), hc] = o.astype(BF16)
                    stats = jnp.where(lane == h, lse[rs], stats)
        if use_bound:
            l2 = jnp.log2(stats)
            stats = jnp.where(lane < A_HEADS, (bound + l2) * (1.0 / LOG2E), jnp.where(lane < 2 * A_HEADS, l2, 0.0))
        st_ref[r, pl.ds(r0, sub), :] = stats

    def run(use_bound):
        for r in range(q_ref.shape[0]):
            for j in range(chunk // sub):
                sub_block(r, j, use_bound)

    run(True)
    head_lanes = lax.broadcasted_iota(jnp.int32, st_ref.shape, 2) < A_HEADS
    lse_min = jnp.min(jnp.where(head_lanes, st_ref[...], jnp.inf), keepdims=True)
    trusted = jnp.min(lse_min * LOG2E - bound) >= math.log2(L_MIN)

    @pl.when(jnp.logical_not(trusted))
    def _():
        run(False)


def _dil_attn(qkv, bound, g, step_rows=2048, sub=128):
    window, dil = DIL_CONFIGS[g]
    half = window // (2 * dil)
    _, batch, _, length, _ = qkv.shape
    chunk = min(step_rows, length)
    rc = max(1, min(dil, step_rows // chunk))
    assert length % chunk == 0 and chunk % sub == 0 and chunk % half == 0 and dil % rc == 0
    cph = chunk // half
    last = length // half - 1

    def main(which):
        return pl.BlockSpec((None, None, rc, chunk, A_WIDTH), lambda b, c, i: (which, b, c, i, 0))

    def prev(which):
        return pl.BlockSpec((None, None, rc, half, A_WIDTH),
                            lambda b, c, i: (which, b, c, jnp.maximum(i * cph - 1, 0), 0))

    def nxt(which):
        return pl.BlockSpec((None, None, rc, half, A_WIDTH),
                            lambda b, c, i: (which, b, c, jnp.minimum((i + 1) * cph, last), 0))

    col = jnp.arange(LANES)[None, None, :]
    head = jnp.arange(A_HEADS)[:, None, None]
    hot = jnp.broadcast_to((col == head) | (col == head + A_HEADS), (A_HEADS, sub + 2 * half, LANES)).astype(BF16)
    kern = functools.partial(_dil_attn_kernel, length=length, chunk=chunk, sub=sub, half=half)
    return pl.pallas_call(
        kern,
        grid=(batch, dil // rc, length // chunk),
        in_specs=[pl.BlockSpec(memory_space=pltpu.SMEM), main(0), main(1), prev(1), nxt(1), main(2), prev(2), nxt(2),
                  _resident(hot.shape)],
        out_specs=[
            pl.BlockSpec((None, rc, chunk, A_WIDTH), lambda b, c, i: (b, c, i, 0)),
            pl.BlockSpec((None, rc, chunk, LANES), lambda b, c, i: (b, c, i, 0)),
        ],
        out_shape=[
            jax.ShapeDtypeStruct((batch, dil, length, A_WIDTH), BF16),
            jax.ShapeDtypeStruct((batch, dil, length, LANES), F32),
        ],
        compiler_params=_params(("parallel", "parallel", "parallel")),
        name=f"dil_attn_g{g}",
    )(bound, qkv, qkv, qkv, qkv, qkv, qkv, qkv, hot)


def _merge_tile(o_refs, l_refs, e_ref, out_ref, os_ref, ls_ref):
    nlb = os_ref.shape[1]
    natural = [o_refs[g].shape[0] == 1 for g in range(N_GROUPS)]
    for g in range(N_GROUPS):
        if natural[g]:
            continue
        dil, rows = o_refs[g].shape[0], o_refs[g].shape[1]
        for c in range(dil):
            oc = o_refs[g][c].astype(F32)
            for kb in range(nlb):
                os_ref[g, kb, pl.ds(c, rows, stride=dil), :] = oc[:, kb * LANES:(kb + 1) * LANES]
            ls_ref[g, pl.ds(c, rows, stride=dil), :] = l_refs[g][c]
    stats = [l_refs[g][0] if natural[g] else ls_ref[g] for g in range(N_GROUPS)]
    mx = jnp.maximum(jnp.maximum(stats[0], stats[1]), stats[2])
    e = [jnp.exp(st - mx) for st in stats]
    inv = 1.0 / (e[0] + e[1] + e[2])
    coef = []
    for g in range(N_GROUPS):
        pending = jnp.exp2(-pltpu.roll(stats[g], LANES - A_HEADS, 1))
        coef.append((e[g] * inv * pending).astype(BF16))
    for kb in range(nlb):
        cols = slice(kb * LANES, (kb + 1) * LANES)
        acc = None
        for g in range(N_GROUPS):
            w = jnp.dot(coef[g], e_ref[:, cols], preferred_element_type=F32)
            o_nat = o_refs[g][0, :, cols].astype(F32) if natural[g] else os_ref[g, kb]
            acc = w * o_nat if acc is None else acc + w * o_nat
        out_ref[:, cols] = acc.astype(BF16)


def _ffn_tile(x, a, wo_ref, nrm_ref, w1_ref, w2_ref, ff_chunk):
    x1 = x + jnp.dot(a, wo_ref[...], preferred_element_type=F32)
    h = (x1 * _rms_scale(x1, x1.shape[-1]) * nrm_ref[...]).astype(BF16)
    acc = x1
    for c in range(w1_ref.shape[1] // ff_chunk):
        cols = slice(c * ff_chunk, (c + 1) * ff_chunk)
        a = jnp.maximum(jnp.dot(h, w1_ref[:, cols], preferred_element_type=F32), 0.0)
        acc = acc + jnp.dot((a * a).astype(BF16), w2_ref[cols, :], preferred_element_type=F32)
    return acc


def _ffn_kernel(x_ref, a_ref, wo_ref, nrm_ref, w1_ref, w2_ref, out_ref, *, ff_chunk):
    out_ref[...] = _ffn_tile(x_ref[...], a_ref[...], wo_ref, nrm_ref, w1_ref, w2_ref, ff_chunk)


def _ffn(x2d, a2d, w_o, nrm, w1, w2, tm, ff_chunk=1024):
    T, D = x2d.shape
    return pl.pallas_call(
        functools.partial(_ffn_kernel, ff_chunk=ff_chunk),
        grid=(T // tm,),
        in_specs=[
            pl.BlockSpec((tm, D), lambda i: (i, 0)),
            pl.BlockSpec((tm, a2d.shape[1]), lambda i: (i, 0)),
            _resident(w_o.shape),
            _resident(nrm.shape),
            _resident(w1.shape),
            _resident(w2.shape),
        ],
        out_specs=pl.BlockSpec((tm, D), lambda i: (i, 0)),
        out_shape=jax.ShapeDtypeStruct((T, D), F32),
        compiler_params=_params(("parallel",)),
        name="outproj_ffn",
    )(x2d, a2d, w_o, nrm, w1, w2)


def _merge_kernel(o0_ref, o1_ref, o2_ref, l0_ref, l1_ref, l2_ref, e_ref, out_ref, os_ref, ls_ref):
    _merge_tile((o0_ref, o1_ref, o2_ref), (l0_ref, l1_ref, l2_ref), e_ref, out_ref, os_ref, ls_ref)


def _merge(outs, stats, seq, tm):
    batch = outs[0].shape[0]
    spt = seq // tm
    expand = (jnp.arange(LANES)[:, None] == (jnp.arange(A_WIDTH)[None, :] // A_HEAD_DIM)).astype(BF16)

    def spec(arr):
        dil, width = arr.shape[1], arr.shape[3]
        return pl.BlockSpec((None, dil, tm // dil, width), lambda i: (i // spt, 0, i % spt, 0))

    return pl.pallas_call(
        _merge_kernel,
        grid=(batch * spt,),
        in_specs=[spec(o) for o in outs] + [spec(l) for l in stats] + [_resident(expand.shape)],
        out_specs=pl.BlockSpec((tm, A_WIDTH), lambda i: (i, 0)),
        out_shape=jax.ShapeDtypeStruct((batch * seq, A_WIDTH), BF16),
        scratch_shapes=[pltpu.VMEM((N_GROUPS, A_WIDTH // LANES, tm, LANES), F32),
                        pltpu.VMEM((N_GROUPS, tm, LANES), F32)],
        compiler_params=_params(("parallel",)),
        name="a_merge",
    )(*outs, *stats, expand)


def _b_proj_kernel(x_ref, nrm_ref, win_ref, qag_ref, kvag_ref, wqb_ref, wkn_ref, wv_ref,
                   gq_ref, gk_ref, cq_ref, ck_ref, cosq_ref, cosk_ref, sin_ref,
                   q_ref, k_ref, v_ref):
    x = x_ref[...]
    h = (x * _rms_scale(x, x.shape[-1]) * nrm_ref[...]).astype(BF16)
    lat = jnp.dot(h, win_ref[...], preferred_element_type=F32)
    c_q = lat[:, :Q_LORA]
    c_kv = lat[:, Q_LORA:Q_LORA + KV_LORA]
    k_rope = lat[:, Q_LORA + KV_LORA:Q_LORA + KV_LORA + LANES]
    k_roll = lat[:, Q_LORA + KV_LORA + LANES:]
    cqn = (c_q * _rms_scale(c_q, Q_LORA) * qag_ref[...]).astype(BF16)
    ckvn = (c_kv * _rms_scale(c_kv, KV_LORA) * kvag_ref[...]).astype(BF16)
    sin = sin_ref[...]

    vt_all = lax.dot_general(wv_ref[...], ckvn, NT_DIMS, preferred_element_type=F32)
    ones_rows = (lax.broadcasted_iota(jnp.int32, (VT_ROWS - V_DIM, x.shape[0]), 0) == 0).astype(BF16)
    for hh in range(B_HEADS):
        v_ref[hh * VT_ROWS:hh * VT_ROWS + V_DIM, :] = vt_all[hh * V_DIM:(hh + 1) * V_DIM, :].astype(BF16)
        v_ref[hh * VT_ROWS + V_DIM:(hh + 1) * VT_ROWS, :] = ones_rows

    q3 = jnp.dot(cqn, wqb_ref[...], preferred_element_type=F32)
    gq_n = gq_ref[:, :LANES]
    cos_q = cosq_ref[...]
    for hh in range(B_HEADS):
        qn = q3[:, hh * Q3_PAD:hh * Q3_PAD + LANES]
        qr = q3[:, hh * Q3_PAD + LANES:hh * Q3_PAD + QK_PAD]
        rq = lax.rsqrt(jnp.sum(qn * qn + qr * qr, axis=-1, keepdims=True) * (1.0 / QK_DIM) + EPS)
        qr = qr * cos_q + q3[:, hh * Q3_PAD + QK_PAD:(hh + 1) * Q3_PAD] * sin
        q_ref[:, hh * QK_PAD:hh * QK_PAD + LANES] = (qn * rq * gq_n).astype(BF16)
        q_ref[:, hh * QK_PAD + LANES:(hh + 1) * QK_PAD] = (qr * rq + cq_ref[...]).astype(BF16)

    kn_all = jnp.dot(ckvn, wkn_ref[...], preferred_element_type=F32)
    ss_kr = jnp.sum(k_rope * k_rope, axis=-1, keepdims=True)
    kr = k_rope * cosk_ref[...] + k_roll * sin
    gk_n = gk_ref[:, :LANES]
    for hh in range(B_HEADS):
        kn = kn_all[:, hh * LANES:(hh + 1) * LANES]
        rk = lax.rsqrt((jnp.sum(kn * kn, axis=-1, keepdims=True) + ss_kr) * (1.0 / QK_DIM) + EPS)
        k_ref[:, hh * QK_PAD:hh * QK_PAD + LANES] = (kn * rk * gk_n).astype(BF16)
        k_ref[:, hh * QK_PAD + LANES:(hh + 1) * QK_PAD] = (kr * rk + ck_ref[...]).astype(BF16)


def _b_proj(x2d, consts, tables, seq, tm):
    T, D = x2d.shape
    spt = seq // tm
    row = lambda i: (i, 0)
    pos = pl.BlockSpec((tm, LANES), lambda i: (i % spt, 0))
    return pl.pallas_call(
        _b_proj_kernel,
        grid=(T // tm,),
        in_specs=[pl.BlockSpec((tm, D), row)] + [_resident(c.shape) for c in consts] + [pos] * len(tables),
        out_specs=[
            pl.BlockSpec((tm, B_HEADS * QK_PAD), row),
            pl.BlockSpec((tm, B_HEADS * QK_PAD), row),
            pl.BlockSpec((B_HEADS * VT_ROWS, tm), lambda i: (0, i)),
        ],
        out_shape=[
            jax.ShapeDtypeStruct((T, B_HEADS * QK_PAD), BF16),
            jax.ShapeDtypeStruct((T, B_HEADS * QK_PAD), BF16),
            jax.ShapeDtypeStruct((B_HEADS * VT_ROWS, T), BF16),
        ],
        compiler_params=_params(("parallel",)),
        name="b_proj",
    )(x2d, *consts, *tables)


def _mla_attn_kernel(q_ref, k_ref, vt_ref, o_ref, acc_ref, *, tk):
    parts, _, tp = acc_ref.shape
    nkv = k_ref.shape[0] // tk

    def scores_t(part, r0):
        st = lax.dot_general(k_ref[pl.ds(r0, tk), :], q_ref[part * tp:(part + 1) * tp, :], NT_DIMS,
                             preferred_element_type=F32)
        return st, vt_ref[:, pl.ds(r0, tk)]

    def finish(part):
        inv = 1.0 / acc_ref[part, V_DIM:V_DIM + 1, :]
        o_ref[part * tp:(part + 1) * tp, :] = (acc_ref[part, :V_DIM, :] * inv).T.astype(BF16)

    for part in range(parts):
        acc_ref[part] = jnp.zeros(acc_ref.shape[1:], F32)
        for kb in range(nkv):
            st, vt = scores_t(part, kb * tk)
            acc_ref[part] += jnp.dot(vt, jnp.exp2(st).astype(BF16), preferred_element_type=F32)
        finish(part)
    trusted = jnp.min(acc_ref[:, V_DIM:V_DIM + 1, :]) >= L_MIN

    @pl.when(jnp.logical_not(trusted))
    def _():
        for part in range(parts):
            def slow(kb, m):
                st, vt = scores_t(part, pl.multiple_of(kb * tk, tk))
                m_new = jnp.maximum(m, jnp.max(st, axis=0, keepdims=True))
                pt = jnp.exp2(st - m_new).astype(BF16)
                acc_ref[part] = (jnp.exp2(m - m_new) * acc_ref[part]
                                 + jnp.dot(vt, pt, preferred_element_type=F32))
                return m_new

            acc_ref[part] = jnp.zeros(acc_ref.shape[1:], F32)
            lax.fori_loop(0, nkv, slow, jnp.full((1, tp), -jnp.inf, F32))
            finish(part)


def _mla_attn(q, k, vt, batch, seq, tq=2048, tk=1024, parts=1):
    nq = seq // tq
    tk = min(tk, seq)
    assert seq % tk == 0 and tq % parts == 0
    return pl.pallas_call(
        functools.partial(_mla_attn_kernel, tk=tk),
        grid=(batch, B_HEADS, nq),
        in_specs=[
            pl.BlockSpec((tq, QK_PAD), lambda b, h, i: (b * nq + i, h)),
            pl.BlockSpec((seq, QK_PAD), lambda b, h, i: (b, h)),
            pl.BlockSpec((VT_ROWS, seq), lambda b, h, i: (h, b)),
        ],
        out_specs=pl.BlockSpec((tq, V_DIM), lambda b, h, i: (b * nq + i, h)),
        out_shape=jax.ShapeDtypeStruct((batch * seq, B_HEADS * V_DIM), BF16),
        scratch_shapes=[pltpu.VMEM((parts, VT_ROWS, tq // parts), F32)],
        compiler_params=_params(("parallel", "parallel", "parallel")),
        name="mla_attn",
    )(q, k, vt)


def _rope_angles(seq, d):
    pos = jnp.arange(seq, dtype=F32)
    freqs = ROPE_THETA ** (-jnp.arange(0, d, 2, dtype=F32) / d)
    ang = pos[:, None] * freqs[None, :]
    return jnp.cos(ang), jnp.sin(ang)


def _spread_rope(t, axis):
    a, b = jnp.split(t, 2, axis=axis)
    z = jnp.zeros_like(a)
    return jnp.concatenate([a, z, b, z], axis=axis)


def _residue_major(table, tm, dil):
    s, w = table.shape
    return table.reshape(s // tm, tm // dil, dil, w).transpose(0, 2, 1, 3).reshape(s, w)


def kernel(x, norm_mix, norm_ffn, a_w_qkv, a_q_gain, a_k_gain, a_w_o, b_w_in, b_q_a_gain, b_w_qb,
           b_kv_a_gain, b_w_kvb, b_q_gain, b_k_gain, b_w_o, ffn_w1, ffn_w2):
    batch, seq, d_model = x.shape
    T = batch * seq
    x2d = x.reshape(T, d_model)

    tm_a = 1024
    cos, sin = _rope_angles(seq, A_HEAD_DIM)
    cos_a = jnp.concatenate([cos, cos], -1)
    sin_a = jnp.concatenate([sin, sin], -1)
    w_qkv = a_w_qkv[0].astype(BF16).reshape(d_model, 3, N_GROUPS, A_HEADS // 2, 2, 2, A_HEAD_DIM // 2)
    w_qkv = jnp.concatenate([
        w_qkv[:, :2].transpose(2, 1, 0, 3, 5, 4, 6).reshape(N_GROUPS, 2, d_model, A_WIDTH),
        w_qkv[:, 2:].transpose(2, 1, 0, 3, 4, 5, 6).reshape(N_GROUPS, 1, d_model, A_WIDTH)], axis=1)
    hd = A_HEAD_DIM // 2
    outs, lses = [], []
    for g, (_, dil) in enumerate(DIL_CONFIGS):
        tables = jnp.stack([_residue_major(cos_a, tm_a, dil), _residue_major(sin_a, tm_a, dil)])

        def pair_gain(gain):
            return jnp.stack([jnp.tile(gain[:hd], 2), jnp.tile(gain[hd:], 2)])

        q_scale = LOG2E / math.sqrt(A_HEAD_DIM)
        gains = jnp.stack([pair_gain(a_q_gain[0, g] * q_scale), pair_gain(a_k_gain[0, g])])
        bound = 1.02 * A_HEAD_DIM * q_scale * jnp.max(jnp.abs(a_q_gain[0, g])) * jnp.max(jnp.abs(a_k_gain[0, g]))
        qkv = _a_proj(x2d, norm_mix[0][None, :], w_qkv, g, gains, tables, batch, seq, dil, tm_a)
        o, lse = _dil_attn(qkv, jnp.reshape(bound, (1,)).astype(F32), g)
        outs.append(o)
        lses.append(lse)
    merged = _merge(outs, lses, seq, tm=512)
    x2d = _ffn(x2d, merged, a_w_o[0].astype(BF16), norm_ffn[0][None, :],
               ffn_w1[0].astype(BF16), ffn_w2[0].astype(BF16), tm=512)

    cos, sin = _rope_angles(seq, ROPE_DIM)
    cos_b = _spread_rope(jnp.concatenate([cos, cos], -1), -1)
    sin_b = _spread_rope(jnp.concatenate([-sin, sin], -1), -1)

    def head_gain(gain, scale):
        return (jnp.concatenate([gain[:NOPE_DIM], _spread_rope(gain[NOPE_DIM:], 0)]) * scale)[None, :]

    q_scale = LOG2E / math.sqrt(QK_DIM)
    gq = head_gain(b_q_gain[0], q_scale)
    gk = head_gain(b_k_gain[0], 1.0)
    bound = 1.02 * QK_DIM * q_scale * jnp.max(jnp.abs(b_q_gain[0])) * jnp.max(jnp.abs(b_k_gain[0]))
    const_lane = (jnp.arange(LANES) == CONST_LANE).astype(F32)[None, :]

    def half_roll(w, gain):
        return jnp.roll(w * gain, LANES // 2, axis=-1)

    w_in = b_w_in[0]
    k_rope_w = _spread_rope(w_in[:, Q_LORA + KV_LORA:], 1)
    w_in = jnp.concatenate([w_in[:, :Q_LORA + KV_LORA], k_rope_w, half_roll(k_rope_w, gk[:, LANES:])], 1)
    w_qb = b_w_qb[0].reshape(Q_LORA, B_HEADS, QK_DIM)
    q_rope_w = _spread_rope(w_qb[..., NOPE_DIM:], 2)
    w_qb = jnp.concatenate([w_qb[..., :NOPE_DIM], q_rope_w, half_roll(q_rope_w, gq[:, LANES:])], -1)
    w_qb = w_qb.reshape(Q_LORA, B_HEADS * Q3_PAD)
    w_kvb = b_w_kvb[0].reshape(KV_LORA, B_HEADS, NOPE_DIM + V_DIM)
    w_kn = w_kvb[..., :NOPE_DIM].reshape(KV_LORA, B_HEADS * NOPE_DIM)
    w_v = w_kvb[..., NOPE_DIM:].reshape(KV_LORA, B_HEADS * V_DIM).T

    consts = (norm_mix[1][None, :], w_in.astype(BF16), b_q_a_gain[0][None, :], b_kv_a_gain[0][None, :],
              w_qb.astype(BF16), w_kn.astype(BF16), w_v.astype(BF16), gq, gk,
              -bound * const_lane, const_lane)
    tables = (cos_b * gq[:, LANES:], cos_b * gk[:, LANES:], sin_b)
    q, k, v = _b_proj(x2d, consts, tables, seq, tm=512)
    o = _mla_attn(q, k, v, batch, seq)
    x2d = _ffn(x2d, o, b_w_o[0].astype(BF16), norm_ffn[1][None, :],
               ffn_w1[1].astype(BF16), ffn_w2[1].astype(BF16), tm=512)
    return x2d.reshape(batch, seq, d_model)
```

```python
import functools
import math

import jax
import jax.numpy as jnp
from jax import lax
from jax.experimental import pallas as pl
from jax.experimental.pallas import tpu as pltpu

EPS = 1e-6
ROPE_THETA = 10000.0
NEG_FILL = -1e30
LOG2E = math.log2(math.e)

DIL_CONFIGS = ((128, 1), (512, 4), (2048, 16))
N_GROUPS = len(DIL_CONFIGS)
A_HEADS = 8
A_HEAD_DIM = 128
A_WIDTH = A_HEADS * A_HEAD_DIM

B_HEADS = 8
Q_LORA = 256
KV_LORA = 128
NOPE_DIM = 128
ROPE_DIM = 64
V_DIM = 128
QK_DIM = NOPE_DIM + ROPE_DIM
QK_PAD = 256
Q3_PAD = 384
VT_ROWS = 144
CONST_LANE = 32
L_MIN = 2.0 ** -80

LANES = 128
VMEM_LIMIT = 48 * 1024 * 1024

BF16 = jnp.bfloat16
F32 = jnp.float32
NT_DIMS = (((1,), (1,)), ((), ()))


def _params(sem):
    return pltpu.CompilerParams(dimension_semantics=sem, vmem_limit_bytes=VMEM_LIMIT)


def _resident(shape):
    zeros = (0,) * len(shape)
    return pl.BlockSpec(shape, lambda *_: zeros, pipeline_mode=pl.Buffered(1))


def _rms_scale(x, width):
    return lax.rsqrt(jnp.sum(x * x, axis=-1, keepdims=True) * (1.0 / width) + EPS)


def _a_proj_kernel(x_ref, nrm_ref, w_ref, gain_ref, tab_ref, o_ref, h_ref, hs_ref=None, *, dil):
    tm = x_ref.shape[0]
    rows = tm // dil
    x = x_ref[...]
    hn = x * _rms_scale(x, x.shape[-1]) * nrm_ref[...]
    if dil == 1:
        h_ref[...] = hn.astype(BF16)
    else:
        for kb in range(hs_ref.shape[0]):
            hs_ref[kb] = hn[:, kb * LANES:(kb + 1) * LANES]
        for c in range(dil):
            for kb in range(hs_ref.shape[0]):
                h_ref[c * rows:(c + 1) * rows, kb * LANES:(kb + 1) * LANES] = (
                    hs_ref[kb, pl.ds(c, rows, stride=dil), :].astype(BF16))

    def store(which, col0, val):
        for c in range(dil):
            o_ref[which, c, :, col0:col0 + LANES] = val[c * rows:(c + 1) * rows].astype(BF16)

    for which in range(2):
        y = jnp.dot(h_ref[...], w_ref[which], preferred_element_type=F32)
        ga, gb = gain_ref[which, 0:1, :], gain_ref[which, 1:2, :]
        cos, sin = tab_ref[0], tab_ref[1]
        ca, sb, cb, sa = ga * cos, gb * sin, gb * cos, ga * sin
        for p in range(A_HEADS // 2):
            y2 = y[:, p * 2 * LANES:(p + 1) * 2 * LANES]
            t = y2[:, :LANES] * y2[:, :LANES] + y2[:, LANES:] * y2[:, LANES:]
            lo_lane = lax.broadcasted_iota(jnp.int32, t.shape, 1) < LANES // 2
            s_lo = jnp.sum(jnp.where(lo_lane, t, 0.0), axis=-1, keepdims=True)
            s_all = jnp.sum(t, axis=-1, keepdims=True)
            ms = jnp.where(lo_lane, s_lo, s_all - s_lo) * (1.0 / A_HEAD_DIM)
            rinv = lax.rsqrt(ms + EPS)
            ar = y2[:, :LANES] * rinv
            br = y2[:, LANES:] * rinv
            store(which, p * 2 * LANES, ar * ca - br * sb)
            store(which, p * 2 * LANES + LANES, br * cb + ar * sa)

    y = jnp.dot(h_ref[...], w_ref[2], preferred_element_type=F32)
    for hh in range(A_HEADS):
        store(2, hh * LANES, y[:, hh * LANES:(hh + 1) * LANES])


def _a_proj(x2d, nrm, w_all, g, gains, tables, batch, seq, dil, tm):
    T, D = x2d.shape
    spt = seq // tm
    rows = tm // dil
    return pl.pallas_call(
        functools.partial(_a_proj_kernel, dil=dil),
        grid=(T // tm,),
        in_specs=[
            pl.BlockSpec((tm, D), lambda i: (i, 0)),
            _resident(nrm.shape),
            pl.BlockSpec((None,) + w_all.shape[1:], lambda i: (g, 0, 0, 0), pipeline_mode=pl.Buffered(1)),
            _resident(gains.shape),
            pl.BlockSpec((2, tm, LANES), lambda i: (0, i % spt, 0)),
        ],
        out_specs=pl.BlockSpec((3, None, dil, rows, A_WIDTH), lambda i: (0, i // spt, 0, i % spt, 0)),
        out_shape=jax.ShapeDtypeStruct((3, batch, dil, seq // dil, A_WIDTH), BF16),
        scratch_shapes=[pltpu.VMEM((tm, D), BF16)] + ([pltpu.VMEM((D // LANES, tm, LANES), F32)] if dil > 1 else []),
        compiler_params=_params(("parallel",)),
        name=f"a_proj_d{dil}",
    )(x2d, nrm, w_all, gains, tables)


def _dil_attn_kernel(bound_ref, q_ref, k_ref, kp_ref, kn_ref, v_ref, vp_ref, vn_ref, hot_ref,
                     o_ref, st_ref, *, length, chunk, sub, half):
    i = pl.program_id(2)
    nk = sub + 2 * half

    def window(ref, prev_ref, next_ref, r, j, cols):
        lo, hi = j * sub - half, (j + 1) * sub + half
        parts = [prev_ref[r, :, cols]] if lo < 0 else []
        parts.append(ref[r, max(lo, 0):min(hi, chunk), cols])
        if hi > chunk:
            parts.append(next_ref[r, :, cols])
        return parts[0] if len(parts) == 1 else jnp.concatenate(parts, axis=0)

    qi = lax.broadcasted_iota(jnp.int32, (2 * sub, nk), 0) % sub
    kj = lax.broadcasted_iota(jnp.int32, (2 * sub, nk), 1)
    band = jnp.abs(qi + half - kj) <= half
    lane = lax.broadcasted_iota(jnp.int32, (sub, LANES), 1)
    lane2 = lax.broadcasted_iota(jnp.int32, (1, 2 * LANES), 1) % LANES
    first = (lane2 < LANES // 2).astype(BF16)
    second = (lane2 >= LANES // 2).astype(BF16)

    bound = bound_ref[0]

    def sub_block(r, j, use_bound):
        r0 = j * sub
        base = i * chunk + j * sub
        lo = half - base
        hi = length + half - base
        mask = band & (kj >= lo) & (kj < hi)
        bias = jnp.where(mask, -bound if use_bound else 0.0, NEG_FILL)
        stats = jnp.zeros((sub, LANES), F32)
        for p in range(A_HEADS // 2):
            cols = slice(p * 2 * LANES, (p + 1) * 2 * LANES)
            q2 = q_ref[r, pl.ds(r0, sub), cols]
            qq = jnp.concatenate([q2 * first, q2 * second], axis=0)
            s = lax.dot_general(qq, window(k_ref, kp_ref, kn_ref, r, j, cols), NT_DIMS,
                                preferred_element_type=F32) + bias
            if use_bound:
                eb = jnp.exp2(s).astype(BF16)
            else:
                ref = jnp.max(s, axis=-1, keepdims=True)
                e = jnp.exp2(s - ref)
                den = jnp.sum(e, axis=-1, keepdims=True)
                eb = e.astype(BF16)
                lse = (ref + jnp.log2(den)) * (1.0 / LOG2E)
            for t in range(2):
                h = 2 * p + t
                hc = slice(h * LANES, (h + 1) * LANES)
                rs = slice(t * sub, (t + 1) * sub)
                v = window(v_ref, vp_ref, vn_ref, r, j, hc)
                if use_bound:
                    ov = jnp.dot(eb[rs], jnp.concatenate([v, hot_ref[h]], axis=1), preferred_element_type=F32)
                    o_ref[r, pl.ds(r0, sub), hc] = ov[:, :LANES].astype(BF16)
                    stats = stats + ov[:, LANES:]
                else:
                    o = jnp.dot(eb[rs], v, preferred_element_type=F32) / den[rs]
                    o_ref[r, pl.ds(r0, sub), hc] = o.astype(BF16)
                    stats = jnp.where(lane == h, lse[rs], stats)
        if use_bound:
            l2 = jnp.log2(stats)
            stats = jnp.where(lane < A_HEADS, (bound + l2) * (1.0 / LOG2E), jnp.where(lane < 2 * A_HEADS, l2, 0.0))
        st_ref[r, pl.ds(r0, sub), :] = stats

    def run(use_bound):
        for r in range(q_ref.shape[0]):
            for j in range(chunk // sub):
                sub_block(r, j, use_bound)

    run(True)
    head_lanes = lax.broadcasted_iota(jnp.int32, st_ref.shape, 2) < A_HEADS
    lse_min = jnp.min(jnp.where(head_lanes, st_ref[...], jnp.inf), keepdims=True)
    trusted = jnp.min(lse_min * LOG2E - bound) >= math.log2(L_MIN)

    @pl.when(jnp.logical_not(trusted))
    def _():
        run(False)


def _dil_attn(qkv, bound, g, step_rows=2048, sub=128):
    window, dil = DIL_CONFIGS[g]
    half = window // (2 * dil)
    _, batch, _, length, _ = qkv.shape
    chunk = min(step_rows, length)
    rc = max(1, min(dil, step_rows // chunk))
    assert length % chunk == 0 and chunk % sub == 0 and chunk % half == 0 and dil % rc == 0
    cph = chunk // half
    last = length // half - 1

    def main(which):
        return pl.BlockSpec((None, None, rc, chunk, A_WIDTH), lambda b, c, i: (which, b, c, i, 0))

    def prev(which):
        return pl.BlockSpec((None, None, rc, half, A_WIDTH),
                            lambda b, c, i: (which, b, c, jnp.maximum(i * cph - 1, 0), 0))

    def nxt(which):
        return pl.BlockSpec((None, None, rc, half, A_WIDTH),
                            lambda b, c, i: (which, b, c, jnp.minimum((i + 1) * cph, last), 0))

    col = jnp.arange(LANES)[None, None, :]
    head = jnp.arange(A_HEADS)[:, None, None]
    hot = jnp.broadcast_to((col == head) | (col == head + A_HEADS), (A_HEADS, sub + 2 * half, LANES)).astype(BF16)
    kern = functools.partial(_dil_attn_kernel, length=length, chunk=chunk, sub=sub, half=half)
    return pl.pallas_call(
        kern,
        grid=(batch, dil // rc, length // chunk),
        in_specs=[pl.BlockSpec(memory_space=pltpu.SMEM), main(0), main(1), prev(1), nxt(1), main(2), prev(2), nxt(2),
                  _resident(hot.shape)],
        out_specs=[
            pl.BlockSpec((None, rc, chunk, A_WIDTH), lambda b, c, i: (b, c, i, 0)),
            pl.BlockSpec((None, rc, chunk, LANES), lambda b, c, i: (b, c, i, 0)),
        ],
        out_shape=[
            jax.ShapeDtypeStruct((batch, dil, length, A_WIDTH), BF16),
            jax.ShapeDtypeStruct((batch, dil, length, LANES), F32),
        ],
        compiler_params=_params(("parallel", "parallel", "parallel")),
        name=f"dil_attn_g{g}",
    )(bound, qkv, qkv, qkv, qkv, qkv, qkv, qkv, hot)


def _merge_tile(o_refs, l_refs, e_ref, out_ref, os_ref, ls_ref):
    nlb = os_ref.shape[1]
    natural = [o_refs[g].shape[0] == 1 for g in range(N_GROUPS)]
    for g in range(N_GROUPS):
        if natural[g]:
            continue
        dil, rows = o_refs[g].shape[0], o_refs[g].shape[1]
        for c in range(dil):
            oc = o_refs[g][c].astype(F32)
            for kb in range(nlb):
                os_ref[g, kb, pl.ds(c, rows, stride=dil), :] = oc[:, kb * LANES:(kb + 1) * LANES]
            ls_ref[g, pl.ds(c, rows, stride=dil), :] = l_refs[g][c]
    stats = [l_refs[g][0] if natural[g] else ls_ref[g] for g in range(N_GROUPS)]
    mx = jnp.maximum(jnp.maximum(stats[0], stats[1]), stats[2])
    e = [jnp.exp(st - mx) for st in stats]
    inv = 1.0 / (e[0] + e[1] + e[2])
    coef = []
    for g in range(N_GROUPS):
        pending = jnp.exp2(-pltpu.roll(stats[g], LANES - A_HEADS, 1))
        coef.append((e[g] * inv * pending).astype(BF16))
    for kb in range(nlb):
        cols = slice(kb * LANES, (kb + 1) * LANES)
        acc = None
        for g in range(N_GROUPS):
            w = jnp.dot(coef[g], e_ref[:, cols], preferred_element_type=F32)
            o_nat = o_refs[g][0, :, cols].astype(F32) if natural[g] else os_ref[g, kb]
            acc = w * o_nat if acc is None else acc + w * o_nat
        out_ref[:, cols] = acc.astype(BF16)


def _ffn_tile(x, a, wo_ref, nrm_ref, w1_ref, w2_ref, ff_chunk):
    x1 = x + jnp.dot(a, wo_ref[...], preferred_element_type=F32)
    h = (x1 * _rms_scale(x1, x1.shape[-1]) * nrm_ref[...]).astype(BF16)
    acc = x1
    for c in range(w1_ref.shape[1] // ff_chunk):
        cols = slice(c * ff_chunk, (c + 1) * ff_chunk)
        a = jnp.maximum(jnp.dot(h, w1_ref[:, cols], preferred_element_type=F32), 0.0)
        acc = acc + jnp.dot((a * a).astype(BF16), w2_ref[cols, :], preferred_element_type=F32)
    return acc


def _ffn_kernel(x_ref, a_ref, wo_ref, nrm_ref, w1_ref, w2_ref, out_ref, *, ff_chunk):
    out_ref[...] = _ffn_tile(x_ref[...], a_ref[...], wo_ref, nrm_ref, w1_ref, w2_ref, ff_chunk)


def _ffn(x2d, a2d, w_o, nrm, w1, w2, tm, ff_chunk=1024):
    T, D = x2d.shape
    return pl.pallas_call(
        functools.partial(_ffn_kernel, ff_chunk=ff_chunk),
        grid=(T // tm,),
        in_specs=[
            pl.BlockSpec((tm, D), lambda i: (i, 0)),
            pl.BlockSpec((tm, a2d.shape[1]), lambda i: (i, 0)),
            _resident(w_o.shape),
            _resident(nrm.shape),
            _resident(w1.shape),
            _resident(w2.shape),
        ],
        out_specs=pl.BlockSpec((tm, D), lambda i: (i, 0)),
        out_shape=jax.ShapeDtypeStruct((T, D), F32),
        compiler_params=_params(("parallel",)),
        name="outproj_ffn",
    )(x2d, a2d, w_o, nrm, w1, w2)


def _merge_kernel(o0_ref, o1_ref, o2_ref, l0_ref, l1_ref, l2_ref, e_ref, out_ref, os_ref, ls_ref):
    _merge_tile((o0_ref, o1_ref, o2_ref), (l0_ref, l1_ref, l2_ref), e_ref, out_ref, os_ref, ls_ref)


def _merge(outs, stats, seq, tm):
    batch = outs[0].shape[0]
    spt = seq // tm
    expand = (jnp.arange(LANES)[:, None] == (jnp.arange(A_WIDTH)[None, :] // A_HEAD_DIM)).astype(BF16)

    def spec(arr):
        dil, width = arr.shape[1], arr.shape[3]
        return pl.BlockSpec((None, dil, tm // dil, width), lambda i: (i // spt, 0, i % spt, 0))

    return pl.pallas_call(
        _merge_kernel,
        grid=(batch * spt,),
        in_specs=[spec(o) for o in outs] + [spec(l) for l in stats] + [_resident(expand.shape)],
        out_specs=pl.BlockSpec((tm, A_WIDTH), lambda i: (i, 0)),
        out_shape=jax.ShapeDtypeStruct((batch * seq, A_WIDTH), BF16),
        scratch_shapes=[pltpu.VMEM((N_GROUPS, A_WIDTH // LANES, tm, LANES), F32),
                        pltpu.VMEM((N_GROUPS, tm, LANES), F32)],
        compiler_params=_params(("parallel",)),
        name="a_merge",
    )(*outs, *stats, expand)


def _b_proj_kernel(x_ref, nrm_ref, win_ref, qag_ref, kvag_ref, wqb_ref, wkn_ref, wv_ref,
                   gq_ref, gk_ref, cq_ref, ck_ref, cosq_ref, cosk_ref, sin_ref,
                   q_ref, k_ref, v_ref):
    x = x_ref[...]
    h = (x * _rms_scale(x, x.shape[-1]) * nrm_ref[...]).astype(BF16)
    lat = jnp.dot(h, win_ref[...], preferred_element_type=F32)
    c_q = lat[:, :Q_LORA]
    c_kv = lat[:, Q_LORA:Q_LORA + KV_LORA]
    k_rope = lat[:, Q_LORA + KV_LORA:Q_LORA + KV_LORA + LANES]
    k_roll = lat[:, Q_LORA + KV_LORA + LANES:]
    cqn = (c_q * _rms_scale(c_q, Q_LORA) * qag_ref[...]).astype(BF16)
    ckvn = (c_kv * _rms_scale(c_kv, KV_LORA) * kvag_ref[...]).astype(BF16)
    sin = sin_ref[...]

    vt_all = lax.dot_general(wv_ref[...], ckvn, NT_DIMS, preferred_element_type=F32)
    ones_rows = (lax.broadcasted_iota(jnp.int32, (VT_ROWS - V_DIM, x.shape[0]), 0) == 0).astype(BF16)
    for hh in range(B_HEADS):
        v_ref[hh * VT_ROWS:hh * VT_ROWS + V_DIM, :] = vt_all[hh * V_DIM:(hh + 1) * V_DIM, :].astype(BF16)
        v_ref[hh * VT_ROWS + V_DIM:(hh + 1) * VT_ROWS, :] = ones_rows

    q3 = jnp.dot(cqn, wqb_ref[...], preferred_element_type=F32)
    gq_n = gq_ref[:, :LANES]
    cos_q = cosq_ref[...]
    for hh in range(B_HEADS):
        qn = q3[:, hh * Q3_PAD:hh * Q3_PAD + LANES]
        qr = q3[:, hh * Q3_PAD + LANES:hh * Q3_PAD + QK_PAD]
        rq = lax.rsqrt(jnp.sum(qn * qn + qr * qr, axis=-1, keepdims=True) * (1.0 / QK_DIM) + EPS)
        qr = qr * cos_q + q3[:, hh * Q3_PAD + QK_PAD:(hh + 1) * Q3_PAD] * sin
        q_ref[:, hh * QK_PAD:hh * QK_PAD + LANES] = (qn * rq * gq_n).astype(BF16)
        q_ref[:, hh * QK_PAD + LANES:(hh + 1) * QK_PAD] = (qr * rq + cq_ref[...]).astype(BF16)

    kn_all = jnp.dot(ckvn, wkn_ref[...], preferred_element_type=F32)
    ss_kr = jnp.sum(k_rope * k_rope, axis=-1, keepdims=True)
    kr = k_rope * cosk_ref[...] + k_roll * sin
    gk_n = gk_ref[:, :LANES]
    for hh in range(B_HEADS):
        kn = kn_all[:, hh * LANES:(hh + 1) * LANES]
        rk = lax.rsqrt((jnp.sum(kn * kn, axis=-1, keepdims=True) + ss_kr) * (1.0 / QK_DIM) + EPS)
        k_ref[:, hh * QK_PAD:hh * QK_PAD + LANES] = (kn * rk * gk_n).astype(BF16)
        k_ref[:, hh * QK_PAD + LANES:(hh + 1) * QK_PAD] = (kr * rk + ck_ref[...]).astype(BF16)


def _b_proj(x2d, consts, tables, seq, tm):
    T, D = x2d.shape
    spt = seq // tm
    row = lambda i: (i, 0)
    pos = pl.BlockSpec((tm, LANES), lambda i: (i % spt, 0))
    return pl.pallas_call(
        _b_proj_kernel,
        grid=(T // tm,),
        in_specs=[pl.BlockSpec((tm, D), row)] + [_resident(c.shape) for c in consts] + [pos] * len(tables),
        out_specs=[
            pl.BlockSpec((tm, B_HEADS * QK_PAD), row),
            pl.BlockSpec((tm, B_HEADS * QK_PAD), row),
            pl.BlockSpec((B_HEADS * VT_ROWS, tm), lambda i: (0, i)),
        ],
        out_shape=[
            jax.ShapeDtypeStruct((T, B_HEADS * QK_PAD), BF16),
            jax.ShapeDtypeStruct((T, B_HEADS * QK_PAD), BF16),
            jax.ShapeDtypeStruct((B_HEADS * VT_ROWS, T), BF16),
        ],
        compiler_params=_params(("parallel",)),
        name="b_proj",
    )(x2d, *consts, *tables)


def _mla_attn_kernel(q_ref, k_ref, vt_ref, o_ref, acc_ref, *, tk):
    parts, _, tp = acc_ref.shape
    nkv = k_ref.shape[0] // tk

    def scores_t(part, r0):
        st = lax.dot_general(k_ref[pl.ds(r0, tk), :], q_ref[part * tp:(part + 1) * tp, :], NT_DIMS,
                             preferred_element_type=F32)
        return st, vt_ref[:, pl.ds(r0, tk)]

    def finish(part):
        inv = 1.0 / acc_ref[part, V_DIM:V_DIM + 1, :]
        o_ref[part * tp:(part + 1) * tp, :] = (acc_ref[part, :V_DIM, :] * inv).T.astype(BF16)

    for part in range(parts):
        acc_ref[part] = jnp.zeros(acc_ref.shape[1:], F32)
        for kb in range(nkv):
            st, vt = scores_t(part, kb * tk)
            acc_ref[part] += jnp.dot(vt, jnp.exp2(st).astype(BF16), preferred_element_type=F32)
        finish(part)
    trusted = jnp.min(acc_ref[:, V_DIM:V_DIM + 1, :]) >= L_MIN

    @pl.when(jnp.logical_not(trusted))
    def _():
        for part in range(parts):
            def slow(kb, m):
                st, vt = scores_t(part, pl.multiple_of(kb * tk, tk))
                m_new = jnp.maximum(m, jnp.max(st, axis=0, keepdims=True))
                pt = jnp.exp2(st - m_new).astype(BF16)
                acc_ref[part] = (jnp.exp2(m - m_new) * acc_ref[part]
                                 + jnp.dot(vt, pt, preferred_element_type=F32))
                return m_new

            acc_ref[part] = jnp.zeros(acc_ref.shape[1:], F32)
            lax.fori_loop(0, nkv, slow, jnp.full((1, tp), -jnp.inf, F32))
            finish(part)


def _mla_attn(q, k, vt, batch, seq, tq=2048, tk=1024, parts=1):
    nq = seq // tq
    tk = min(tk, seq)
    assert seq % tk == 0 and tq % parts == 0
    return pl.pallas_call(
        functools.partial(_mla_attn_kernel, tk=tk),
        grid=(batch, B_HEADS, nq),
        in_specs=[
            pl.BlockSpec((tq, QK_PAD), lambda b, h, i: (b * nq + i, h)),
            pl.BlockSpec((seq, QK_PAD), lambda b, h, i: (b, h)),
            pl.BlockSpec((VT_ROWS, seq), lambda b, h, i: (h, b)),
        ],
        out_specs=pl.BlockSpec((tq, V_DIM), lambda b, h, i: (b * nq + i, h)),
        out_shape=jax.ShapeDtypeStruct((batch * seq, B_HEADS * V_DIM), BF16),
        scratch_shapes=[pltpu.VMEM((parts, VT_ROWS, tq // parts), F32)],
        compiler_params=_params(("parallel", "parallel", "parallel")),
        name="mla_attn",
    )(q, k, vt)


def _rope_angles(seq, d):
    pos = jnp.arange(seq, dtype=F32)
    freqs = ROPE_THETA ** (-jnp.arange(0, d, 2, dtype=F32) / d)
    ang = pos[:, None] * freqs[None, :]
    return jnp.cos(ang), jnp.sin(ang)


def _spread_rope(t, axis):
    a, b = jnp.split(t, 2, axis=axis)
    z = jnp.zeros_like(a)
    return jnp.concatenate([a, z, b, z], axis=axis)


def _residue_major(table, tm, dil):
    s, w = table.shape
    return table.reshape(s // tm, tm // dil, dil, w).transpose(0, 2, 1, 3).reshape(s, w)


def kernel(x, norm_mix, norm_ffn, a_w_qkv, a_q_gain, a_k_gain, a_w_o, b_w_in, b_q_a_gain, b_w_qb,
           b_kv_a_gain, b_w_kvb, b_q_gain, b_k_gain, b_w_o, ffn_w1, ffn_w2):
    batch, seq, d_model = x.shape
    T = batch * seq
    x2d = x.reshape(T, d_model)

    tm_a = 1024
    cos, sin = _rope_angles(seq, A_HEAD_DIM)
    cos_a = jnp.concatenate([cos, cos], -1)
    sin_a = jnp.concatenate([sin, sin], -1)
    w_qkv = a_w_qkv[0].astype(BF16).reshape(d_model, 3, N_GROUPS, A_HEADS // 2, 2, 2, A_HEAD_DIM // 2)
    w_qkv = jnp.concatenate([
        w_qkv[:, :2].transpose(2, 1, 0, 3, 5, 4, 6).reshape(N_GROUPS, 2, d_model, A_WIDTH),
        w_qkv[:, 2:].transpose(2, 1, 0, 3, 4, 5, 6).reshape(N_GROUPS, 1, d_model, A_WIDTH)], axis=1)
    hd = A_HEAD_DIM // 2
    outs, lses = [], []
    for g, (_, dil) in enumerate(DIL_CONFIGS):
        tables = jnp.stack([_residue_major(cos_a, tm_a, dil), _residue_major(sin_a, tm_a, dil)])

        def pair_gain(gain):
            return jnp.stack([jnp.tile(gain[:hd], 2), jnp.tile(gain[hd:], 2)])

        q_scale = LOG2E / math.sqrt(A_HEAD_DIM)
        gains = jnp.stack([pair_gain(a_q_gain[0, g] * q_scale), pair_gain(a_k_gain[0, g])])
        bound = 1.02 * A_HEAD_DIM * q_scale * jnp.max(jnp.abs(a_q_gain[0, g])) * jnp.max(jnp.abs(a_k_gain[0, g]))
        qkv = _a_proj(x2d, norm_mix[0][None, :], w_qkv, g, gains, tables, batch, seq, dil, tm_a)
        o, lse = _dil_attn(qkv, jnp.reshape(bound, (1,)).astype(F32), g)
        outs.append(o)
        lses.append(lse)
    merged = _merge(outs, lses, seq, tm=1024)
    x2d = _ffn(x2d, merged, a_w_o[0].astype(BF16), norm_ffn[0][None, :],
               ffn_w1[0].astype(BF16), ffn_w2[0].astype(BF16), tm=512)

    cos, sin = _rope_angles(seq, ROPE_DIM)
    cos_b = _spread_rope(jnp.concatenate([cos, cos], -1), -1)
    sin_b = _spread_rope(jnp.concatenate([-sin, sin], -1), -1)

    def head_gain(gain, scale):
        return (jnp.concatenate([gain[:NOPE_DIM], _spread_rope(gain[NOPE_DIM:], 0)]) * scale)[None, :]

    q_scale = LOG2E / math.sqrt(QK_DIM)
    gq = head_gain(b_q_gain[0], q_scale)
    gk = head_gain(b_k_gain[0], 1.0)
    bound = 1.02 * QK_DIM * q_scale * jnp.max(jnp.abs(b_q_gain[0])) * jnp.max(jnp.abs(b_k_gain[0]))
    const_lane = (jnp.arange(LANES) == CONST_LANE).astype(F32)[None, :]

    def half_roll(w, gain):
        return jnp.roll(w * gain, LANES // 2, axis=-1)

    w_in = b_w_in[0]
    k_rope_w = _spread_rope(w_in[:, Q_LORA + KV_LORA:], 1)
    w_in = jnp.concatenate([w_in[:, :Q_LORA + KV_LORA], k_rope_w, half_roll(k_rope_w, gk[:, LANES:])], 1)
    w_qb = b_w_qb[0].reshape(Q_LORA, B_HEADS, QK_DIM)
    q_rope_w = _spread_rope(w_qb[..., NOPE_DIM:], 2)
    w_qb = jnp.concatenate([w_qb[..., :NOPE_DIM], q_rope_w, half_roll(q_rope_w, gq[:, LANES:])], -1)
    w_qb = w_qb.reshape(Q_LORA, B_HEADS * Q3_PAD)
    w_kvb = b_w_kvb[0].reshape(KV_LORA, B_HEADS, NOPE_DIM + V_DIM)
    w_kn = w_kvb[..., :NOPE_DIM].reshape(KV_LORA, B_HEADS * NOPE_DIM)
    w_v = w_kvb[..., NOPE_DIM:].reshape(KV_LORA, B_HEADS * V_DIM).T

    consts = (norm_mix[1][None, :], w_in.astype(BF16), b_q_a_gain[0][None, :], b_kv_a_gain[0][None, :],
              w_qb.astype(BF16), w_kn.astype(BF16), w_v.astype(BF16), gq, gk,
              -bound * const_lane, const_lane)
    tables = (cos_b * gq[:, LANES:], cos_b * gk[:, LANES:], sin_b)
    q, k, v = _b_proj(x2d, consts, tables, seq, tm=512)
    o = _mla_attn(q, k, v, batch, seq)
    x2d = _ffn(x2d, o, b_w_o[0].astype(BF16), norm_ffn[1][None, :],
               ffn_w1[1].astype(BF16), ffn_w2[1].astype(BF16), tm=512)
    return x2d.reshape(batch, seq, d_model)
```

```python
import functools
import math

import jax
import jax.numpy as jnp
from jax import lax
from jax.experimental import pallas as pl
from jax.experimental.pallas import tpu as pltpu

EPS = 1e-6
ROPE_THETA = 10000.0
NEG_FILL = -1e30
LOG2E = math.log2(math.e)

DIL_CONFIGS = ((128, 1), (512, 4), (2048, 16))
N_GROUPS = len(DIL_CONFIGS)
A_HEADS = 8
A_HEAD_DIM = 128
A_WIDTH = A_HEADS * A_HEAD_DIM

B_HEADS = 8
Q_LORA = 256
KV_LORA = 128
NOPE_DIM = 128
ROPE_DIM = 64
V_DIM = 128
QK_DIM = NOPE_DIM + ROPE_DIM
QK_PAD = 256
Q3_PAD = 384
VT_ROWS = 144
CONST_LANE = 32
L_MIN = 2.0 ** -80

LANES = 128
VMEM_LIMIT = 48 * 1024 * 1024

BF16 = jnp.bfloat16
F32 = jnp.float32
NT_DIMS = (((1,), (1,)), ((), ()))


def _params(sem):
    return pltpu.CompilerParams(dimension_semantics=sem, vmem_limit_bytes=VMEM_LIMIT)


def _resident(shape):
    zeros = (0,) * len(shape)
    return pl.BlockSpec(shape, lambda *_: zeros, pipeline_mode=pl.Buffered(1))


def _rms_scale(x, width):
    return lax.rsqrt(jnp.sum(x * x, axis=-1, keepdims=True) * (1.0 / width) + EPS)


def _a_proj_kernel(x_ref, nrm_ref, w_ref, gain_ref, tab_ref, o_ref, h_ref, hs_ref=None, *, dil):
    tm = x_ref.shape[0]
    rows = tm // dil
    x = x_ref[...]
    hn = x * _rms_scale(x, x.shape[-1]) * nrm_ref[...]
    if dil == 1:
        h_ref[...] = hn.astype(BF16)
    else:
        for kb in range(hs_ref.shape[0]):
            hs_ref[kb] = hn[:, kb * LANES:(kb + 1) * LANES]
        for c in range(dil):
            for kb in range(hs_ref.shape[0]):
                h_ref[c * rows:(c + 1) * rows, kb * LANES:(kb + 1) * LANES] = (
                    hs_ref[kb, pl.ds(c, rows, stride=dil), :].astype(BF16))

    def store(which, col0, val):
        for c in range(dil):
            o_ref[which, c, :, col0:col0 + LANES] = val[c * rows:(c + 1) * rows].astype(BF16)

    for which in range(2):
        y = jnp.dot(h_ref[...], w_ref[which], preferred_element_type=F32)
        ga, gb = gain_ref[which, 0:1, :], gain_ref[which, 1:2, :]
        cos, sin = tab_ref[0], tab_ref[1]
        ca, sb, cb, sa = ga * cos, gb * sin, gb * cos, ga * sin
        for p in range(A_HEADS // 2):
            y2 = y[:, p * 2 * LANES:(p + 1) * 2 * LANES]
            t = y2[:, :LANES] * y2[:, :LANES] + y2[:, LANES:] * y2[:, LANES:]
            lo_lane = lax.broadcasted_iota(jnp.int32, t.shape, 1) < LANES // 2
            s_lo = jnp.sum(jnp.where(lo_lane, t, 0.0), axis=-1, keepdims=True)
            s_all = jnp.sum(t, axis=-1, keepdims=True)
            ms = jnp.where(lo_lane, s_lo, s_all - s_lo) * (1.0 / A_HEAD_DIM)
            rinv = lax.rsqrt(ms + EPS)
            ar = y2[:, :LANES] * rinv
            br = y2[:, LANES:] * rinv
            store(which, p * 2 * LANES, ar * ca - br * sb)
            store(which, p * 2 * LANES + LANES, br * cb + ar * sa)

    y = jnp.dot(h_ref[...], w_ref[2], preferred_element_type=F32)
    for hh in range(A_HEADS):
        store(2, hh * LANES, y[:, hh * LANES:(hh + 1) * LANES])


def _a_proj(x2d, nrm, w_all, g, gains, tables, batch, seq, dil, tm):
    T, D = x2d.shape
    spt = seq // tm
    rows = tm // dil
    return pl.pallas_call(
        functools.partial(_a_proj_kernel, dil=dil),
        grid=(T // tm,),
        in_specs=[
            pl.BlockSpec((tm, D), lambda i: (i, 0)),
            _resident(nrm.shape),
            pl.BlockSpec((None,) + w_all.shape[1:], lambda i: (g, 0, 0, 0), pipeline_mode=pl.Buffered(1)),
            _resident(gains.shape),
            pl.BlockSpec((2, tm, LANES), lambda i: (0, i % spt, 0)),
        ],
        out_specs=pl.BlockSpec((3, None, dil, rows, A_WIDTH), lambda i: (0, i // spt, 0, i % spt, 0)),
        out_shape=jax.ShapeDtypeStruct((3, batch, dil, seq // dil, A_WIDTH), BF16),
        scratch_shapes=[pltpu.VMEM((tm, D), BF16)] + ([pltpu.VMEM((D // LANES, tm, LANES), F32)] if dil > 1 else []),
        compiler_params=_params(("parallel",)),
        name=f"a_proj_d{dil}",
    )(x2d, nrm, w_all, gains, tables)


def _dil_attn_kernel(bound_ref, q_ref, k_ref, kp_ref, kn_ref, v_ref, vp_ref, vn_ref, hot_ref,
                     o_ref, st_ref, *, length, chunk, sub, half):
    i = pl.program_id(2)
    nk = sub + 2 * half

    def window(ref, prev_ref, next_ref, r, j, cols):
        lo, hi = j * sub - half, (j + 1) * sub + half
        parts = [prev_ref[r, :, cols]] if lo < 0 else []
        parts.append(ref[r, max(lo, 0):min(hi, chunk), cols])
        if hi > chunk:
            parts.append(next_ref[r, :, cols])
        return parts[0] if len(parts) == 1 else jnp.concatenate(parts, axis=0)

    qi = lax.broadcasted_iota(jnp.int32, (2 * sub, nk), 0) % sub
    kj = lax.broadcasted_iota(jnp.int32, (2 * sub, nk), 1)
    band = jnp.abs(qi + half - kj) <= half
    lane = lax.broadcasted_iota(jnp.int32, (sub, LANES), 1)
    lane2 = lax.broadcasted_iota(jnp.int32, (1, 2 * LANES), 1) % LANES
    first = (lane2 < LANES // 2).astype(BF16)
    second = (lane2 >= LANES // 2).astype(BF16)

    bound = bound_ref[0]

    def sub_block(r, j, use_bound):
        r0 = j * sub
        base = i * chunk + j * sub
        lo = half - base
        hi = length + half - base
        mask = band & (kj >= lo) & (kj < hi)
        bias = jnp.where(mask, -bound if use_bound else 0.0, NEG_FILL)
        stats = jnp.zeros((sub, LANES), F32)
        for p in range(A_HEADS // 2):
            cols = slice(p * 2 * LANES, (p + 1) * 2 * LANES)
            q2 = q_ref[r, pl.ds(r0, sub), cols]
            qq = jnp.concatenate([q2 * first, q2 * second], axis=0)
            s = lax.dot_general(qq, window(k_ref, kp_ref, kn_ref, r, j, cols), NT_DIMS,
                                preferred_element_type=F32) + bias
            if use_bound:
                eb = jnp.exp2(s).astype(BF16)
            else:
                ref = jnp.max(s, axis=-1, keepdims=True)
                e = jnp.exp2(s - ref)
                den = jnp.sum(e, axis=-1, keepdims=True)
                eb = e.astype(BF16)
                lse = (ref + jnp.log2(den)) * (1.0 / LOG2E)
            for t in range(2):
                h = 2 * p + t
                hc = slice(h * LANES, (h + 1) * LANES)
                rs = slice(t * sub, (t + 1) * sub)
                v = window(v_ref, vp_ref, vn_ref, r, j, hc)
                if use_bound:
                    ov = jnp.dot(eb[rs], jnp.concatenate([v, hot_ref[h]], axis=1), preferred_element_type=F32)
                    o_ref[r, pl.ds(r0, sub), hc] = ov[:, :LANES].astype(BF16)
                    stats = stats + ov[:, LANES:]
                else:
                    o = jnp.dot(eb[rs], v, preferred_element_type=F32) / den[rs]
                    o_ref[r, pl.ds(r0, sub), hc] = o.astype(BF16)
                    stats = jnp.where(lane == h, lse[rs], stats)
        if use_bound:
            l2 = jnp.log2(stats)
            stats = jnp.where(lane < A_HEADS, (bound + l2) * (1.0 / LOG2E), jnp.where(lane < 2 * A_HEADS, l2, 0.0))
        st_ref[r, pl.ds(r0, sub), :] = stats

    def run(use_bound):
        for r in range(q_ref.shape[0]):
            for j in range(chunk // sub):
                sub_block(r, j, use_bound)

    run(True)
    head_lanes = lax.broadcasted_iota(jnp.int32, st_ref.shape, 2) < A_HEADS
    lse_min = jnp.min(jnp.where(head_lanes, st_ref[...], jnp.inf), keepdims=True)
    trusted = jnp.min(lse_min * LOG2E - bound) >= math.log2(L_MIN)

    @pl.when(jnp.logical_not(trusted))
    def _():
        run(False)


def _dil_attn(qkv, bound, g, step_rows=2048, sub=128):
    window, dil = DIL_CONFIGS[g]
    half = window // (2 * dil)
    _, batch, _, length, _ = qkv.shape
    chunk = min(step_rows, length)
    rc = max(1, min(dil, step_rows // chunk))
    assert length % chunk == 0 and chunk % sub == 0 and chunk % half == 0 and dil % rc == 0
    cph = chunk // half
    last = length // half - 1

    def main(which):
        return pl.BlockSpec((None, None, rc, chunk, A_WIDTH), lambda b, c, i: (which, b, c, i, 0))

    def prev(which):
        return pl.BlockSpec((None, None, rc, half, A_WIDTH),
                            lambda b, c, i: (which, b, c, jnp.maximum(i * cph - 1, 0), 0))

    def nxt(which):
        return pl.BlockSpec((None, None, rc, half, A_WIDTH),
                            lambda b, c, i: (which, b, c, jnp.minimum((i + 1) * cph, last), 0))

    col = jnp.arange(LANES)[None, None, :]
    head = jnp.arange(A_HEADS)[:, None, None]
    hot = jnp.broadcast_to((col == head) | (col == head + A_HEADS), (A_HEADS, sub + 2 * half, LANES)).astype(BF16)
    kern = functools.partial(_dil_attn_kernel, length=length, chunk=chunk, sub=sub, half=half)
    return pl.pallas_call(
        kern,
        grid=(batch, dil // rc, length // chunk),
        in_specs=[pl.BlockSpec(memory_space=pltpu.SMEM), main(0), main(1), prev(1), nxt(1), main(2), prev(2), nxt(2),
                  _resident(hot.shape)],
        out_specs=[
            pl.BlockSpec((None, rc, chunk, A_WIDTH), lambda b, c, i: (b, c, i, 0)),
            pl.BlockSpec((None, rc, chunk, LANES), lambda b, c, i: (b, c, i, 0)),
        ],
        out_shape=[
            jax.ShapeDtypeStruct((batch, dil, length, A_WIDTH), BF16),
            jax.ShapeDtypeStruct((batch, dil, length, LANES), F32),
        ],
        compiler_params=_params(("parallel", "parallel", "parallel")),
        name=f"dil_attn_g{g}",
    )(bound, qkv, qkv, qkv, qkv, qkv, qkv, qkv, hot)


def _merge_tile(o_refs, l_refs, e_ref, out_ref, os_ref, ls_ref):
    nlb = os_ref.shape[1]
    natural = [o_refs[g].shape[0] == 1 for g in range(N_GROUPS)]
    for g in range(N_GROUPS):
        if natural[g]:
            continue
        dil, rows = o_refs[g].shape[0], o_refs[g].shape[1]
        for c in range(dil):
            oc = o_refs[g][c].astype(F32)
            for kb in range(nlb):
                os_ref[g, kb, pl.ds(c, rows, stride=dil), :] = oc[:, kb * LANES:(kb + 1) * LANES]
            ls_ref[g, pl.ds(c, rows, stride=dil), :] = l_refs[g][c]
    stats = [l_refs[g][0] if natural[g] else ls_ref[g] for g in range(N_GROUPS)]
    mx = jnp.maximum(jnp.maximum(stats[0], stats[1]), stats[2])
    e = [jnp.exp(st - mx) for st in stats]
    inv = 1.0 / (e[0] + e[1] + e[2])
    coef = []
    for g in range(N_GROUPS):
        pending = jnp.exp2(-pltpu.roll(stats[g], LANES - A_HEADS, 1))
        coef.append((e[g] * inv * pending).astype(BF16))
    for kb in range(nlb):
        cols = slice(kb * LANES, (kb + 1) * LANES)
        acc = None
        for g in range(N_GROUPS):
            w = jnp.dot(coef[g], e_ref[:, cols], preferred_element_type=F32)
            o_nat = o_refs[g][0, :, cols].astype(F32) if natural[g] else os_ref[g, kb]
            acc = w * o_nat if acc is None else acc + w * o_nat
        out_ref[:, cols] = acc.astype(BF16)


def _ffn_tile(x, a, wo_ref, nrm_ref, w1_ref, w2_ref, ff_chunk):
    x1 = x + jnp.dot(a, wo_ref[...], preferred_element_type=F32)
    h = (x1 * _rms_scale(x1, x1.shape[-1]) * nrm_ref[...]).astype(BF16)
    acc = x1
    for c in range(w1_ref.shape[1] // ff_chunk):
        cols = slice(c * ff_chunk, (c + 1) * ff_chunk)
        a = jnp.maximum(jnp.dot(h, w1_ref[:, cols], preferred_element_type=F32), 0.0)
        acc = acc + jnp.dot((a * a).astype(BF16), w2_ref[cols, :], preferred_element_type=F32)
    return acc


def _ffn_kernel(x_ref, a_ref, wo_ref, nrm_ref, w1_ref, w2_ref, out_ref, *, ff_chunk):
    out_ref[...] = _ffn_tile(x_ref[...], a_ref[...], wo_ref, nrm_ref, w1_ref, w2_ref, ff_chunk)


def _ffn(x2d, a2d, w_o, nrm, w1, w2, tm, ff_chunk=1024):
    T, D = x2d.shape
    return pl.pallas_call(
        functools.partial(_ffn_kernel, ff_chunk=ff_chunk),
        grid=(T // tm,),
        in_specs=[
            pl.BlockSpec((tm, D), lambda i: (i, 0)),
            pl.BlockSpec((tm, a2d.shape[1]), lambda i: (i, 0)),
            _resident(w_o.shape),
            _resident(nrm.shape),
            _resident(w1.shape),
            _resident(w2.shape),
        ],
        out_specs=pl.BlockSpec((tm, D), lambda i: (i, 0)),
        out_shape=jax.ShapeDtypeStruct((T, D), F32),
        compiler_params=_params(("parallel",)),
        name="outproj_ffn",
    )(x2d, a2d, w_o, nrm, w1, w2)


def _merge_kernel(o0_ref, o1_ref, o2_ref, l0_ref, l1_ref, l2_ref, e_ref, out_ref, os_ref, ls_ref):
    _merge_tile((o0_ref, o1_ref, o2_ref), (l0_ref, l1_ref, l2_ref), e_ref, out_ref, os_ref, ls_ref)


def _merge(outs, stats, seq, tm):
    batch = outs[0].shape[0]
    spt = seq // tm
    expand = (jnp.arange(LANES)[:, None] == (jnp.arange(A_WIDTH)[None, :] // A_HEAD_DIM)).astype(BF16)

    def spec(arr):
        dil, width = arr.shape[1], arr.shape[3]
        return pl.BlockSpec((None, dil, tm // dil, width), lambda i: (i // spt, 0, i % spt, 0))

    return pl.pallas_call(
        _merge_kernel,
        grid=(batch * spt,),
        in_specs=[spec(o) for o in outs] + [spec(l) for l in stats] + [_resident(expand.shape)],
        out_specs=pl.BlockSpec((tm, A_WIDTH), lambda i: (i, 0)),
        out_shape=jax.ShapeDtypeStruct((batch * seq, A_WIDTH), BF16),
        scratch_shapes=[pltpu.VMEM((N_GROUPS, A_WIDTH // LANES, tm, LANES), F32),
                        pltpu.VMEM((N_GROUPS, tm, LANES), F32)],
        compiler_params=_params(("parallel",)),
        name="a_merge",
    )(*outs, *stats, expand)


def _b_proj_kernel(x_ref, nrm_ref, win_ref, qag_ref, kvag_ref, wqb_ref, wkn_ref, wv_ref,
                   gq_ref, gk_ref, cq_ref, ck_ref, cosq_ref, cosk_ref, sin_ref,
                   q_ref, k_ref, v_ref):
    x = x_ref[...]
    h = (x * _rms_scale(x, x.shape[-1]) * nrm_ref[...]).astype(BF16)
    lat = jnp.dot(h, win_ref[...], preferred_element_type=F32)
    c_q = lat[:, :Q_LORA]
    c_kv = lat[:, Q_LORA:Q_LORA + KV_LORA]
    k_rope = lat[:, Q_LORA + KV_LORA:Q_LORA + KV_LORA + LANES]
    k_roll = lat[:, Q_LORA + KV_LORA + LANES:]
    cqn = (c_q * _rms_scale(c_q, Q_LORA) * qag_ref[...]).astype(BF16)
    ckvn = (c_kv * _rms_scale(c_kv, KV_LORA) * kvag_ref[...]).astype(BF16)
    sin = sin_ref[...]

    vt_all = lax.dot_general(wv_ref[...], ckvn, NT_DIMS, preferred_element_type=F32)
    ones_rows = (lax.broadcasted_iota(jnp.int32, (VT_ROWS - V_DIM, x.shape[0]), 0) == 0).astype(BF16)
    for hh in range(B_HEADS):
        v_ref[hh * VT_ROWS:hh * VT_ROWS + V_DIM, :] = vt_all[hh * V_DIM:(hh + 1) * V_DIM, :].astype(BF16)
        v_ref[hh * VT_ROWS + V_DIM:(hh + 1) * VT_ROWS, :] = ones_rows

    q3 = jnp.dot(cqn, wqb_ref[...], preferred_element_type=F32)
    gq_n = gq_ref[:, :LANES]
    cos_q = cosq_ref[...]
    for hh in range(B_HEADS):
        qn = q3[:, hh * Q3_PAD:hh * Q3_PAD + LANES]
        qr = q3[:, hh * Q3_PAD + LANES:hh * Q3_PAD + QK_PAD]
        rq = lax.rsqrt(jnp.sum(qn * qn + qr * qr, axis=-1, keepdims=True) * (1.0 / QK_DIM) + EPS)
        qr = qr * cos_q + q3[:, hh * Q3_PAD + QK_PAD:(hh + 1) * Q3_PAD] * sin
        q_ref[:, hh * QK_PAD:hh * QK_PAD + LANES] = (qn * rq * gq_n).astype(BF16)
        q_ref[:, hh * QK_PAD + LANES:(hh + 1) * QK_PAD] = (qr * rq + cq_ref[...]).astype(BF16)

    kn_all = jnp.dot(ckvn, wkn_ref[...], preferred_element_type=F32)
    ss_kr = jnp.sum(k_rope * k_rope, axis=-1, keepdims=True)
    kr = k_rope * cosk_ref[...] + k_roll * sin
    gk_n = gk_ref[:, :LANES]
    for hh in range(B_HEADS):
        kn = kn_all[:, hh * LANES:(hh + 1) * LANES]
        rk = lax.rsqrt((jnp.sum(kn * kn, axis=-1, keepdims=True) + ss_kr) * (1.0 / QK_DIM) + EPS)
        k_ref[:, hh * QK_PAD:hh * QK_PAD + LANES] = (kn * rk * gk_n).astype(BF16)
        k_ref[:, hh * QK_PAD + LANES:(hh + 1) * QK_PAD] = (kr * rk + ck_ref[...]).astype(BF16)


def _b_proj(x2d, consts, tables, seq, tm):
    T, D = x2d.shape
    spt = seq // tm
    row = lambda i: (i, 0)
    pos = pl.BlockSpec((tm, LANES), lambda i: (i % spt, 0))
    return pl.pallas_call(
        _b_proj_kernel,
        grid=(T // tm,),
        in_specs=[pl.BlockSpec((tm, D), row)] + [_resident(c.shape) for c in consts] + [pos] * len(tables),
        out_specs=[
            pl.BlockSpec((tm, B_HEADS * QK_PAD), row),
            pl.BlockSpec((tm, B_HEADS * QK_PAD), row),
            pl.BlockSpec((B_HEADS * VT_ROWS, tm), lambda i: (0, i)),
        ],
        out_shape=[
            jax.ShapeDtypeStruct((T, B_HEADS * QK_PAD), BF16),
            jax.ShapeDtypeStruct((T, B_HEADS * QK_PAD), BF16),
            jax.ShapeDtypeStruct((B_HEADS * VT_ROWS, T), BF16),
        ],
        compiler_params=_params(("parallel",)),
        name="b_proj",
    )(x2d, *consts, *tables)


def _mla_attn_kernel(q_ref, k_ref, vt_ref, o_ref, acc_ref, *, tk):
    parts, _, tp = acc_ref.shape
    nkv = k_ref.shape[0] // tk

    def scores_t(part, r0):
        st = lax.dot_general(k_ref[pl.ds(r0, tk), :], q_ref[part * tp:(part + 1) * tp, :], NT_DIMS,
                             preferred_element_type=F32)
        return st, vt_ref[:, pl.ds(r0, tk)]

    def finish(part):
        inv = 1.0 / acc_ref[part, V_DIM:V_DIM + 1, :]
        o_ref[part * tp:(part + 1) * tp, :] = (acc_ref[part, :V_DIM, :] * inv).T.astype(BF16)

    for part in range(parts):
        acc_ref[part] = jnp.zeros(acc_ref.shape[1:], F32)
        for kb in range(nkv):
            st, vt = scores_t(part, kb * tk)
            acc_ref[part] += jnp.dot(vt, jnp.exp2(st).astype(BF16), preferred_element_type=F32)
        finish(part)
    trusted = jnp.min(acc_ref[:, V_DIM:V_DIM + 1, :]) >= L_MIN

    @pl.when(jnp.logical_not(trusted))
    def _():
        for part in range(parts):
            def slow(kb, m):
                st, vt = scores_t(part, pl.multiple_of(kb * tk, tk))
                m_new = jnp.maximum(m, jnp.max(st, axis=0, keepdims=True))
                pt = jnp.exp2(st - m_new).astype(BF16)
                acc_ref[part] = (jnp.exp2(m - m_new) * acc_ref[part]
                                 + jnp.dot(vt, pt, preferred_element_type=F32))
                return m_new

            acc_ref[part] = jnp.zeros(acc_ref.shape[1:], F32)
            lax.fori_loop(0, nkv, slow, jnp.full((1, tp), -jnp.inf, F32))
            finish(part)


def _mla_attn(q, k, vt, batch, seq, tq=2048, tk=1024, parts=1):
    nq = seq // tq
    tk = min(tk, seq)
    assert seq % tk == 0 and tq % parts == 0
    return pl.pallas_call(
        functools.partial(_mla_attn_kernel, tk=tk),
        grid=(batch, B_HEADS, nq),
        in_specs=[
            pl.BlockSpec((tq, QK_PAD), lambda b, h, i: (b * nq + i, h)),
            pl.BlockSpec((seq, QK_PAD), lambda b, h, i: (b, h)),
            pl.BlockSpec((VT_ROWS, seq), lambda b, h, i: (h, b)),
        ],
        out_specs=pl.BlockSpec((tq, V_DIM), lambda b, h, i: (b * nq + i, h)),
        out_shape=jax.ShapeDtypeStruct((batch * seq, B_HEADS * V_DIM), BF16),
        scratch_shapes=[pltpu.VMEM((parts, VT_ROWS, tq // parts), F32)],
        compiler_params=_params(("parallel", "parallel", "parallel")),
        name="mla_attn",
    )(q, k, vt)


def _rope_angles(seq, d):
    pos = jnp.arange(seq, dtype=F32)
    freqs = ROPE_THETA ** (-jnp.arange(0, d, 2, dtype=F32) / d)
    ang = pos[:, None] * freqs[None, :]
    return jnp.cos(ang), jnp.sin(ang)


def _spread_rope(t, axis):
    a, b = jnp.split(t, 2, axis=axis)
    z = jnp.zeros_like(a)
    return jnp.concatenate([a, z, b, z], axis=axis)


def _residue_major(table, tm, dil):
    s, w = table.shape
    return table.reshape(s // tm, tm // dil, dil, w).transpose(0, 2, 1, 3).reshape(s, w)


def kernel(x, norm_mix, norm_ffn, a_w_qkv, a_q_gain, a_k_gain, a_w_o, b_w_in, b_q_a_gain, b_w_qb,
           b_kv_a_gain, b_w_kvb, b_q_gain, b_k_gain, b_w_o, ffn_w1, ffn_w2):
    batch, seq, d_model = x.shape
    T = batch * seq
    x2d = x.reshape(T, d_model)

    tm_a = 1024
    cos, sin = _rope_angles(seq, A_HEAD_DIM)
    cos_a = jnp.concatenate([cos, cos], -1)
    sin_a = jnp.concatenate([sin, sin], -1)
    w_qkv = a_w_qkv[0].astype(BF16).reshape(d_model, 3, N_GROUPS, A_HEADS // 2, 2, 2, A_HEAD_DIM // 2)
    w_qkv = jnp.concatenate([
        w_qkv[:, :2].transpose(2, 1, 0, 3, 5, 4, 6).reshape(N_GROUPS, 2, d_model, A_WIDTH),
        w_qkv[:, 2:].transpose(2, 1, 0, 3, 4, 5, 6).reshape(N_GROUPS, 1, d_model, A_WIDTH)], axis=1)
    hd = A_HEAD_DIM // 2
    outs, lses = [], []
    for g, (_, dil) in enumerate(DIL_CONFIGS):
        tables = jnp.stack([_residue_major(cos_a, tm_a, dil), _residue_major(sin_a, tm_a, dil)])

        def pair_gain(gain):
            return jnp.stack([jnp.tile(gain[:hd], 2), jnp.tile(gain[hd:], 2)])

        q_scale = LOG2E / math.sqrt(A_HEAD_DIM)
        gains = jnp.stack([pair_gain(a_q_gain[0, g] * q_scale), pair_gain(a_k_gain[0, g])])
        bound = 1.02 * A_HEAD_DIM * q_scale * jnp.max(jnp.abs(a_q_gain[0, g])) * jnp.max(jnp.abs(a_k_gain[0, g]))
        qkv = _a_proj(x2d, norm_mix[0][None, :], w_qkv, g, gains, tables, batch, seq, dil, tm_a)
        o, lse = _dil_attn(qkv, jnp.reshape(bound, (1,)).astype(F32), g)
        outs.append(o)
        lses.append(lse)
    merged = _merge(outs, lses, seq, tm=1024)
    x2d = _ffn(x2d, merged, a_w_o[0].astype(BF16), norm_ffn[0][None, :],
               ffn_w1[0].astype(BF16), ffn_w2[0].astype(BF16), tm=512)

    cos, sin = _rope_angles(seq, ROPE_DIM)
    cos_b = _spread_rope(jnp.concatenate([cos, cos], -1), -1)
    sin_b = _spread_rope(jnp.concatenate([-sin, sin], -1), -1)

    def head_gain(gain, scale):
        return (jnp.concatenate([gain[:NOPE_DIM], _spread_rope(gain[NOPE_DIM:], 0)]) * scale)[None, :]

    q_scale = LOG2E / math.sqrt(QK_DIM)
    gq = head_gain(b_q_gain[0], q_scale)
    gk = head_gain(b_k_gain[0], 1.0)
    bound = 1.02 * QK_DIM * q_scale * jnp.max(jnp.abs(b_q_gain[0])) * jnp.max(jnp.abs(b_k_gain[0]))
    const_lane = (jnp.arange(LANES) == CONST_LANE).astype(F32)[None, :]

    def half_roll(w, gain):
        return jnp.roll(w * gain, LANES // 2, axis=-1)

    w_in = b_w_in[0]
    k_rope_w = _spread_rope(w_in[:, Q_LORA + KV_LORA:], 1)
    w_in = jnp.concatenate([w_in[:, :Q_LORA + KV_LORA], k_rope_w, half_roll(k_rope_w, gk[:, LANES:])], 1)
    w_qb = b_w_qb[0].reshape(Q_LORA, B_HEADS, QK_DIM)
    q_rope_w = _spread_rope(w_qb[..., NOPE_DIM:], 2)
    w_qb = jnp.concatenate([w_qb[..., :NOPE_DIM], q_rope_w, half_roll(q_rope_w, gq[:, LANES:])], -1)
    w_qb = w_qb.reshape(Q_LORA, B_HEADS * Q3_PAD)
    w_kvb = b_w_kvb[0].reshape(KV_LORA, B_HEADS, NOPE_DIM + V_DIM)
    w_kn = w_kvb[..., :NOPE_DIM].reshape(KV_LORA, B_HEADS * NOPE_DIM)
    w_v = w_kvb[..., NOPE_DIM:].reshape(KV_LORA, B_HEADS * V_DIM).T

    consts = (norm_mix[1][None, :], w_in.astype(BF16), b_q_a_gain[0][None, :], b_kv_a_gain[0][None, :],
              w_qb.astype(BF16), w_kn.astype(BF16), w_v.astype(BF16), gq, gk,
              -bound * const_lane, const_lane)
    tables = (cos_b * gq[:, LANES:], cos_b * gk[:, LANES:], sin_b)
    q, k, v = _b_proj(x2d, consts, tables, seq, tm=1024)
    o = _mla_attn(q, k, v, batch, seq)
    x2d = _ffn(x2d, o, b_w_o[0].astype(BF16), norm_ffn[1][None, :],
               ffn_w1[1].astype(BF16), ffn_w2[1].astype(BF16), tm=512)
    return x2d.reshape(batch, seq, d_model)
```

```python
import functools
import math

import jax
import jax.numpy as jnp
from jax import lax
from jax.experimental import pallas as pl
from jax.experimental.pallas import tpu as pltpu

EPS = 1e-6
ROPE_THETA = 10000.0
NEG_FILL = -1e30
LOG2E = math.log2(math.e)

DIL_CONFIGS = ((128, 1), (512, 4), (2048, 16))
N_GROUPS = len(DIL_CONFIGS)
A_HEADS = 8
A_HEAD_DIM = 128
A_WIDTH = A_HEADS * A_HEAD_DIM

B_HEADS = 8
Q_LORA = 256
KV_LORA = 128
NOPE_DIM = 128
ROPE_DIM = 64
V_DIM = 128
QK_DIM = NOPE_DIM + ROPE_DIM
QK_PAD = 256
Q3_PAD = 384
VT_ROWS = 144
CONST_LANE = 32
L_MIN = 2.0 ** -80

LANES = 128
VMEM_LIMIT = 48 * 1024 * 1024
FFN_VMEM_LIMIT = 54 * 1024 * 1024

BF16 = jnp.bfloat16
F32 = jnp.float32
NT_DIMS = (((1,), (1,)), ((), ()))


def _params(sem, vmem_limit=VMEM_LIMIT):
    return pltpu.CompilerParams(dimension_semantics=sem, vmem_limit_bytes=vmem_limit)


def _resident(shape):
    zeros = (0,) * len(shape)
    return pl.BlockSpec(shape, lambda *_: zeros, pipeline_mode=pl.Buffered(1))


def _rms_scale(x, width):
    return lax.rsqrt(jnp.sum(x * x, axis=-1, keepdims=True) * (1.0 / width) + EPS)


def _a_proj_kernel(x_ref, nrm_ref, w_ref, gain_ref, tab_ref, o_ref, h_ref, hs_ref=None, *, dil):
    tm = x_ref.shape[0]
    rows = tm // dil
    x = x_ref[...]
    hn = x * _rms_scale(x, x.shape[-1]) * nrm_ref[...]
    if dil == 1:
        h_ref[...] = hn.astype(BF16)
    else:
        for kb in range(hs_ref.shape[0]):
            hs_ref[kb] = hn[:, kb * LANES:(kb + 1) * LANES]
        for c in range(dil):
            for kb in range(hs_ref.shape[0]):
                h_ref[c * rows:(c + 1) * rows, kb * LANES:(kb + 1) * LANES] = (
                    hs_ref[kb, pl.ds(c, rows, stride=dil), :].astype(BF16))

    def store(which, col0, val):
        for c in range(dil):
            o_ref[which, c, :, col0:col0 + LANES] = val[c * rows:(c + 1) * rows].astype(BF16)

    for which in range(2):
        y = jnp.dot(h_ref[...], w_ref[which], preferred_element_type=F32)
        ga, gb = gain_ref[which, 0:1, :], gain_ref[which, 1:2, :]
        cos, sin = tab_ref[0], tab_ref[1]
        ca, sb, cb, sa = ga * cos, gb * sin, gb * cos, ga * sin
        for p in range(A_HEADS // 2):
            y2 = y[:, p * 2 * LANES:(p + 1) * 2 * LANES]
            t = y2[:, :LANES] * y2[:, :LANES] + y2[:, LANES:] * y2[:, LANES:]
            lo_lane = lax.broadcasted_iota(jnp.int32, t.shape, 1) < LANES // 2
            s_lo = jnp.sum(jnp.where(lo_lane, t, 0.0), axis=-1, keepdims=True)
            s_all = jnp.sum(t, axis=-1, keepdims=True)
            ms = jnp.where(lo_lane, s_lo, s_all - s_lo) * (1.0 / A_HEAD_DIM)
            rinv = lax.rsqrt(ms + EPS)
            ar = y2[:, :LANES] * rinv
            br = y2[:, LANES:] * rinv
            store(which, p * 2 * LANES, ar * ca - br * sb)
            store(which, p * 2 * LANES + LANES, br * cb + ar * sa)

    y = jnp.dot(h_ref[...], w_ref[2], preferred_element_type=F32)
    for hh in range(A_HEADS):
        store(2, hh * LANES, y[:, hh * LANES:(hh + 1) * LANES])


def _a_proj(x2d, nrm, w_all, g, gains, tables, batch, seq, dil, tm):
    T, D = x2d.shape
    spt = seq // tm
    rows = tm // dil
    return pl.pallas_call(
        functools.partial(_a_proj_kernel, dil=dil),
        grid=(T // tm,),
        in_specs=[
            pl.BlockSpec((tm, D), lambda i: (i, 0)),
            _resident(nrm.shape),
            pl.BlockSpec((None,) + w_all.shape[1:], lambda i: (g, 0, 0, 0), pipeline_mode=pl.Buffered(1)),
            _resident(gains.shape),
            pl.BlockSpec((2, tm, LANES), lambda i: (0, i % spt, 0)),
        ],
        out_specs=pl.BlockSpec((3, None, dil, rows, A_WIDTH), lambda i: (0, i // spt, 0, i % spt, 0)),
        out_shape=jax.ShapeDtypeStruct((3, batch, dil, seq // dil, A_WIDTH), BF16),
        scratch_shapes=[pltpu.VMEM((tm, D), BF16)] + ([pltpu.VMEM((D // LANES, tm, LANES), F32)] if dil > 1 else []),
        compiler_params=_params(("parallel",)),
        name=f"a_proj_d{dil}",
    )(x2d, nrm, w_all, gains, tables)


def _dil_attn_kernel(bound_ref, q_ref, k_ref, kp_ref, kn_ref, v_ref, vp_ref, vn_ref, hot_ref,
                     o_ref, st_ref, *, length, chunk, sub, half):
    i = pl.program_id(2)
    nk = sub + 2 * half

    def window(ref, prev_ref, next_ref, r, j, cols):
        lo, hi = j * sub - half, (j + 1) * sub + half
        parts = [prev_ref[r, :, cols]] if lo < 0 else []
        parts.append(ref[r, max(lo, 0):min(hi, chunk), cols])
        if hi > chunk:
            parts.append(next_ref[r, :, cols])
        return parts[0] if len(parts) == 1 else jnp.concatenate(parts, axis=0)

    qi = lax.broadcasted_iota(jnp.int32, (2 * sub, nk), 0) % sub
    kj = lax.broadcasted_iota(jnp.int32, (2 * sub, nk), 1)
    band = jnp.abs(qi + half - kj) <= half
    lane = lax.broadcasted_iota(jnp.int32, (sub, LANES), 1)
    lane2 = lax.broadcasted_iota(jnp.int32, (1, 2 * LANES), 1) % LANES
    first = (lane2 < LANES // 2).astype(BF16)
    second = (lane2 >= LANES // 2).astype(BF16)

    bound = bound_ref[0]

    def sub_block(r, j, use_bound):
        r0 = j * sub
        base = i * chunk + j * sub
        lo = half - base
        hi = length + half - base
        mask = band & (kj >= lo) & (kj < hi)
        bias = jnp.where(mask, -bound if use_bound else 0.0, NEG_FILL)
        stats = jnp.zeros((sub, LANES), F32)
        for p in range(A_HEADS // 2):
            cols = slice(p * 2 * LANES, (p + 1) * 2 * LANES)
            q2 = q_ref[r, pl.ds(r0, sub), cols]
            qq = jnp.concatenate([q2 * first, q2 * second], axis=0)
            s = lax.dot_general(qq, window(k_ref, kp_ref, kn_ref, r, j, cols), NT_DIMS,
                                preferred_element_type=F32) + bias
            if use_bound:
                eb = jnp.exp2(s).astype(BF16)
            else:
                ref = jnp.max(s, axis=-1, keepdims=True)
                e = jnp.exp2(s - ref)
                den = jnp.sum(e, axis=-1, keepdims=True)
                eb = e.astype(BF16)
                lse = (ref + jnp.log2(den)) * (1.0 / LOG2E)
            for t in range(2):
                h = 2 * p + t
                hc = slice(h * LANES, (h + 1) * LANES)
                rs = slice(t * sub, (t + 1) * sub)
                v = window(v_ref, vp_ref, vn_ref, r, j, hc)
                if use_bound:
                    ov = jnp.dot(eb[rs], jnp.concatenate([v, hot_ref[h]], axis=1), preferred_element_type=F32)
                    o_ref[r, pl.ds(r0, sub), hc] = ov[:, :LANES].astype(BF16)
                    stats = stats + ov[:, LANES:]
                else:
                    o = jnp.dot(eb[rs], v, preferred_element_type=F32) / den[rs]
                    o_ref[r, pl.ds(r0, sub), hc] = o.astype(BF16)
                    stats = jnp.where(lane == h, lse[rs], stats)
        if use_bound:
            l2 = jnp.log2(stats)
            stats = jnp.where(lane < A_HEADS, (bound + l2) * (1.0 / LOG2E), jnp.where(lane < 2 * A_HEADS, l2, 0.0))
        st_ref[r, pl.ds(r0, sub), :] = stats

    def run(use_bound):
        for r in range(q_ref.shape[0]):
            for j in range(chunk // sub):
                sub_block(r, j, use_bound)

    run(True)
    head_lanes = lax.broadcasted_iota(jnp.int32, st_ref.shape, 2) < A_HEADS
    lse_min = jnp.min(jnp.where(head_lanes, st_ref[...], jnp.inf), keepdims=True)
    trusted = jnp.min(lse_min * LOG2E - bound) >= math.log2(L_MIN)

    @pl.when(jnp.logical_not(trusted))
    def _():
        run(False)


def _dil_attn(qkv, bound, g, step_rows=2048, sub=128):
    window, dil = DIL_CONFIGS[g]
    half = window // (2 * dil)
    _, batch, _, length, _ = qkv.shape
    chunk = min(step_rows, length)
    rc = max(1, min(dil, step_rows // chunk))
    assert length % chunk == 0 and chunk % sub == 0 and chunk % half == 0 and dil % rc == 0
    cph = chunk // half
    last = length // half - 1

    def main(which):
        return pl.BlockSpec((None, None, rc, chunk, A_WIDTH), lambda b, c, i: (which, b, c, i, 0))

    def prev(which):
        return pl.BlockSpec((None, None, rc, half, A_WIDTH),
                            lambda b, c, i: (which, b, c, jnp.maximum(i * cph - 1, 0), 0))

    def nxt(which):
        return pl.BlockSpec((None, None, rc, half, A_WIDTH),
                            lambda b, c, i: (which, b, c, jnp.minimum((i + 1) * cph, last), 0))

    col = jnp.arange(LANES)[None, None, :]
    head = jnp.arange(A_HEADS)[:, None, None]
    hot = jnp.broadcast_to((col == head) | (col == head + A_HEADS), (A_HEADS, sub + 2 * half, LANES)).astype(BF16)
    kern = functools.partial(_dil_attn_kernel, length=length, chunk=chunk, sub=sub, half=half)
    return pl.pallas_call(
        kern,
        grid=(batch, dil // rc, length // chunk),
        in_specs=[pl.BlockSpec(memory_space=pltpu.SMEM), main(0), main(1), prev(1), nxt(1), main(2), prev(2), nxt(2),
                  _resident(hot.shape)],
        out_specs=[
            pl.BlockSpec((None, rc, chunk, A_WIDTH), lambda b, c, i: (b, c, i, 0)),
            pl.BlockSpec((None, rc, chunk, LANES), lambda b, c, i: (b, c, i, 0)),
        ],
        out_shape=[
            jax.ShapeDtypeStruct((batch, dil, length, A_WIDTH), BF16),
            jax.ShapeDtypeStruct((batch, dil, length, LANES), F32),
        ],
        compiler_params=_params(("parallel", "parallel", "parallel")),
        name=f"dil_attn_g{g}",
    )(bound, qkv, qkv, qkv, qkv, qkv, qkv, qkv, hot)


def _merge_tile(o_refs, l_refs, e_ref, out_ref, os_ref, ls_ref):
    nlb = os_ref.shape[1]
    natural = [o_refs[g].shape[0] == 1 for g in range(N_GROUPS)]
    for g in range(N_GROUPS):
        if natural[g]:
            continue
        dil, rows = o_refs[g].shape[0], o_refs[g].shape[1]
        for c in range(dil):
            oc = o_refs[g][c].astype(F32)
            for kb in range(nlb):
                os_ref[g, kb, pl.ds(c, rows, stride=dil), :] = oc[:, kb * LANES:(kb + 1) * LANES]
            ls_ref[g, pl.ds(c, rows, stride=dil), :] = l_refs[g][c]
    stats = [l_refs[g][0] if natural[g] else ls_ref[g] for g in range(N_GROUPS)]
    mx = jnp.maximum(jnp.maximum(stats[0], stats[1]), stats[2])
    e = [jnp.exp(st - mx) for st in stats]
    inv = 1.0 / (e[0] + e[1] + e[2])
    coef = []
    for g in range(N_GROUPS):
        pending = jnp.exp2(-pltpu.roll(stats[g], LANES - A_HEADS, 1))
        coef.append((e[g] * inv * pending).astype(BF16))
    for kb in range(nlb):
        cols = slice(kb * LANES, (kb + 1) * LANES)
        acc = None
        for g in range(N_GROUPS):
            w = jnp.dot(coef[g], e_ref[:, cols], preferred_element_type=F32)
            o_nat = o_refs[g][0, :, cols].astype(F32) if natural[g] else os_ref[g, kb]
            acc = w * o_nat if acc is None else acc + w * o_nat
        out_ref[:, cols] = acc.astype(BF16)


def _ffn_tile(x, a, wo_ref, nrm_ref, w1_ref, w2_ref, ff_chunk):
    x1 = x + jnp.dot(a, wo_ref[...], preferred_element_type=F32)
    h = (x1 * _rms_scale(x1, x1.shape[-1]) * nrm_ref[...]).astype(BF16)
    acc = x1
    for c in range(w1_ref.shape[1] // ff_chunk):
        cols = slice(c * ff_chunk, (c + 1) * ff_chunk)
        a = jnp.maximum(jnp.dot(h, w1_ref[:, cols], preferred_element_type=F32), 0.0)
        acc = acc + jnp.dot((a * a).astype(BF16), w2_ref[cols, :], preferred_element_type=F32)
    return acc


def _ffn_kernel(x_ref, a_ref, wo_ref, nrm_ref, w1_ref, w2_ref, out_ref, *, ff_chunk):
    out_ref[...] = _ffn_tile(x_ref[...], a_ref[...], wo_ref, nrm_ref, w1_ref, w2_ref, ff_chunk)


def _ffn(x2d, a2d, w_o, nrm, w1, w2, tm, ff_chunk=1024):
    T, D = x2d.shape
    return pl.pallas_call(
        functools.partial(_ffn_kernel, ff_chunk=ff_chunk),
        grid=(T // tm,),
        in_specs=[
            pl.BlockSpec((tm, D), lambda i: (i, 0)),
            pl.BlockSpec((tm, a2d.shape[1]), lambda i: (i, 0)),
            _resident(w_o.shape),
            _resident(nrm.shape),
            _resident(w1.shape),
            _resident(w2.shape),
        ],
        out_specs=pl.BlockSpec((tm, D), lambda i: (i, 0)),
        out_shape=jax.ShapeDtypeStruct((T, D), F32),
        compiler_params=_params(("parallel",), vmem_limit=FFN_VMEM_LIMIT),
        name="outproj_ffn",
    )(x2d, a2d, w_o, nrm, w1, w2)


def _merge_kernel(o0_ref, o1_ref, o2_ref, l0_ref, l1_ref, l2_ref, e_ref, out_ref, os_ref, ls_ref):
    _merge_tile((o0_ref, o1_ref, o2_ref), (l0_ref, l1_ref, l2_ref), e_ref, out_ref, os_ref, ls_ref)


def _merge(outs, stats, seq, tm):
    batch = outs[0].shape[0]
    spt = seq // tm
    expand = (jnp.arange(LANES)[:, None] == (jnp.arange(A_WIDTH)[None, :] // A_HEAD_DIM)).astype(BF16)

    def spec(arr):
        dil, width = arr.shape[1], arr.shape[3]
        return pl.BlockSpec((None, dil, tm // dil, width), lambda i: (i // spt, 0, i % spt, 0))

    return pl.pallas_call(
        _merge_kernel,
        grid=(batch * spt,),
        in_specs=[spec(o) for o in outs] + [spec(l) for l in stats] + [_resident(expand.shape)],
        out_specs=pl.BlockSpec((tm, A_WIDTH), lambda i: (i, 0)),
        out_shape=jax.ShapeDtypeStruct((batch * seq, A_WIDTH), BF16),
        scratch_shapes=[pltpu.VMEM((N_GROUPS, A_WIDTH // LANES, tm, LANES), F32),
                        pltpu.VMEM((N_GROUPS, tm, LANES), F32)],
        compiler_params=_params(("parallel",)),
        name="a_merge",
    )(*outs, *stats, expand)


def _b_proj_kernel(x_ref, nrm_ref, win_ref, qag_ref, kvag_ref, wqb_ref, wkn_ref, wv_ref,
                   gq_ref, gk_ref, cq_ref, ck_ref, cosq_ref, cosk_ref, sin_ref,
                   q_ref, k_ref, v_ref):
    x = x_ref[...]
    h = (x * _rms_scale(x, x.shape[-1]) * nrm_ref[...]).astype(BF16)
    lat = jnp.dot(h, win_ref[...], preferred_element_type=F32)
    c_q = lat[:, :Q_LORA]
    c_kv = lat[:, Q_LORA:Q_LORA + KV_LORA]
    k_rope = lat[:, Q_LORA + KV_LORA:Q_LORA + KV_LORA + LANES]
    k_roll = lat[:, Q_LORA + KV_LORA + LANES:]
    cqn = (c_q * _rms_scale(c_q, Q_LORA) * qag_ref[...]).astype(BF16)
    ckvn = (c_kv * _rms_scale(c_kv, KV_LORA) * kvag_ref[...]).astype(BF16)
    sin = sin_ref[...]

    vt_all = lax.dot_general(wv_ref[...], ckvn, NT_DIMS, preferred_element_type=F32)
    ones_rows = (lax.broadcasted_iota(jnp.int32, (VT_ROWS - V_DIM, x.shape[0]), 0) == 0).astype(BF16)
    for hh in range(B_HEADS):
        v_ref[hh * VT_ROWS:hh * VT_ROWS + V_DIM, :] = vt_all[hh * V_DIM:(hh + 1) * V_DIM, :].astype(BF16)
        v_ref[hh * VT_ROWS + V_DIM:(hh + 1) * VT_ROWS, :] = ones_rows

    q3 = jnp.dot(cqn, wqb_ref[...], preferred_element_type=F32)
    gq_n = gq_ref[:, :LANES]
    cos_q = cosq_ref[...]
    for hh in range(B_HEADS):
        qn = q3[:, hh * Q3_PAD:hh * Q3_PAD + LANES]
        qr = q3[:, hh * Q3_PAD + LANES:hh * Q3_PAD + QK_PAD]
        rq = lax.rsqrt(jnp.sum(qn * qn + qr * qr, axis=-1, keepdims=True) * (1.0 / QK_DIM) + EPS)
        qr = qr * cos_q + q3[:, hh * Q3_PAD + QK_PAD:(hh + 1) * Q3_PAD] * sin
        q_ref[:, hh * QK_PAD:hh * QK_PAD + LANES] = (qn * rq * gq_n).astype(BF16)
        q_ref[:, hh * QK_PAD + LANES:(hh + 1) * QK_PAD] = (qr * rq + cq_ref[...]).astype(BF16)

    kn_all = jnp.dot(ckvn, wkn_ref[...], preferred_element_type=F32)
    ss_kr = jnp.sum(k_rope * k_rope, axis=-1, keepdims=True)
    kr = k_rope * cosk_ref[...] + k_roll * sin
    gk_n = gk_ref[:, :LANES]
    for hh in range(B_HEADS):
        kn = kn_all[:, hh * LANES:(hh + 1) * LANES]
        rk = lax.rsqrt((jnp.sum(kn * kn, axis=-1, keepdims=True) + ss_kr) * (1.0 / QK_DIM) + EPS)
        k_ref[:, hh * QK_PAD:hh * QK_PAD + LANES] = (kn * rk * gk_n).astype(BF16)
        k_ref[:, hh * QK_PAD + LANES:(hh + 1) * QK_PAD] = (kr * rk + ck_ref[...]).astype(BF16)


def _b_proj(x2d, consts, tables, seq, tm):
    T, D = x2d.shape
    spt = seq // tm
    row = lambda i: (i, 0)
    pos = pl.BlockSpec((tm, LANES), lambda i: (i % spt, 0))
    return pl.pallas_call(
        _b_proj_kernel,
        grid=(T // tm,),
        in_specs=[pl.BlockSpec((tm, D), row)] + [_resident(c.shape) for c in consts] + [pos] * len(tables),
        out_specs=[
            pl.BlockSpec((tm, B_HEADS * QK_PAD), row),
            pl.BlockSpec((tm, B_HEADS * QK_PAD), row),
            pl.BlockSpec((B_HEADS * VT_ROWS, tm), lambda i: (0, i)),
        ],
        out_shape=[
            jax.ShapeDtypeStruct((T, B_HEADS * QK_PAD), BF16),
            jax.ShapeDtypeStruct((T, B_HEADS * QK_PAD), BF16),
            jax.ShapeDtypeStruct((B_HEADS * VT_ROWS, T), BF16),
        ],
        compiler_params=_params(("parallel",)),
        name="b_proj",
    )(x2d, *consts, *tables)


def _mla_attn_kernel(q_ref, k_ref, vt_ref, o_ref, acc_ref, *, tk):
    parts, _, tp = acc_ref.shape
    nkv = k_ref.shape[0] // tk

    def scores_t(part, r0):
        st = lax.dot_general(k_ref[pl.ds(r0, tk), :], q_ref[part * tp:(part + 1) * tp, :], NT_DIMS,
                             preferred_element_type=F32)
        return st, vt_ref[:, pl.ds(r0, tk)]

    def finish(part):
        inv = 1.0 / acc_ref[part, V_DIM:V_DIM + 1, :]
        o_ref[part * tp:(part + 1) * tp, :] = (acc_ref[part, :V_DIM, :] * inv).T.astype(BF16)

    for part in range(parts):
        acc_ref[part] = jnp.zeros(acc_ref.shape[1:], F32)
        for kb in range(nkv):
            st, vt = scores_t(part, kb * tk)
            acc_ref[part] += jnp.dot(vt, jnp.exp2(st).astype(BF16), preferred_element_type=F32)
        finish(part)
    trusted = jnp.min(acc_ref[:, V_DIM:V_DIM + 1, :]) >= L_MIN

    @pl.when(jnp.logical_not(trusted))
    def _():
        for part in range(parts):
            def slow(kb, m):
                st, vt = scores_t(part, pl.multiple_of(kb * tk, tk))
                m_new = jnp.maximum(m, jnp.max(st, axis=0, keepdims=True))
                pt = jnp.exp2(st - m_new).astype(BF16)
                acc_ref[part] = (jnp.exp2(m - m_new) * acc_ref[part]
                                 + jnp.dot(vt, pt, preferred_element_type=F32))
                return m_new

            acc_ref[part] = jnp.zeros(acc_ref.shape[1:], F32)
            lax.fori_loop(0, nkv, slow, jnp.full((1, tp), -jnp.inf, F32))
            finish(part)


def _mla_attn(q, k, vt, batch, seq, tq=2048, tk=1024, parts=1):
    nq = seq // tq
    tk = min(tk, seq)
    assert seq % tk == 0 and tq % parts == 0
    return pl.pallas_call(
        functools.partial(_mla_attn_kernel, tk=tk),
        grid=(batch, B_HEADS, nq),
        in_specs=[
            pl.BlockSpec((tq, QK_PAD), lambda b, h, i: (b * nq + i, h)),
            pl.BlockSpec((seq, QK_PAD), lambda b, h, i: (b, h)),
            pl.BlockSpec((VT_ROWS, seq), lambda b, h, i: (h, b)),
        ],
        out_specs=pl.BlockSpec((tq, V_DIM), lambda b, h, i: (b * nq + i, h)),
        out_shape=jax.ShapeDtypeStruct((batch * seq, B_HEADS * V_DIM), BF16),
        scratch_shapes=[pltpu.VMEM((parts, VT_ROWS, tq // parts), F32)],
        compiler_params=_params(("parallel", "parallel", "parallel")),
        name="mla_attn",
    )(q, k, vt)


def _rope_angles(seq, d):
    pos = jnp.arange(seq, dtype=F32)
    freqs = ROPE_THETA ** (-jnp.arange(0, d, 2, dtype=F32) / d)
    ang = pos[:, None] * freqs[None, :]
    return jnp.cos(ang), jnp.sin(ang)


def _spread_rope(t, axis):
    a, b = jnp.split(t, 2, axis=axis)
    z = jnp.zeros_like(a)
    return jnp.concatenate([a, z, b, z], axis=axis)


def _residue_major(table, tm, dil):
    s, w = table.shape
    return table.reshape(s // tm, tm // dil, dil, w).transpose(0, 2, 1, 3).reshape(s, w)


def kernel(x, norm_mix, norm_ffn, a_w_qkv, a_q_gain, a_k_gain, a_w_o, b_w_in, b_q_a_gain, b_w_qb,
           b_kv_a_gain, b_w_kvb, b_q_gain, b_k_gain, b_w_o, ffn_w1, ffn_w2):
    batch, seq, d_model = x.shape
    T = batch * seq
    x2d = x.reshape(T, d_model)

    tm_a = 1024
    cos, sin = _rope_angles(seq, A_HEAD_DIM)
    cos_a = jnp.concatenate([cos, cos], -1)
    sin_a = jnp.concatenate([sin, sin], -1)
    w_qkv = a_w_qkv[0].astype(BF16).reshape(d_model, 3, N_GROUPS, A_HEADS // 2, 2, 2, A_HEAD_DIM // 2)
    w_qkv = jnp.concatenate([
        w_qkv[:, :2].transpose(2, 1, 0, 3, 5, 4, 6).reshape(N_GROUPS, 2, d_model, A_WIDTH),
        w_qkv[:, 2:].transpose(2, 1, 0, 3, 4, 5, 6).reshape(N_GROUPS, 1, d_model, A_WIDTH)], axis=1)
    hd = A_HEAD_DIM // 2
    outs, lses = [], []
    for g, (_, dil) in enumerate(DIL_CONFIGS):
        tables = jnp.stack([_residue_major(cos_a, tm_a, dil), _residue_major(sin_a, tm_a, dil)])

        def pair_gain(gain):
            return jnp.stack([jnp.tile(gain[:hd], 2), jnp.tile(gain[hd:], 2)])

        q_scale = LOG2E / math.sqrt(A_HEAD_DIM)
        gains = jnp.stack([pair_gain(a_q_gain[0, g] * q_scale), pair_gain(a_k_gain[0, g])])
        bound = 1.02 * A_HEAD_DIM * q_scale * jnp.max(jnp.abs(a_q_gain[0, g])) * jnp.max(jnp.abs(a_k_gain[0, g]))
        qkv = _a_proj(x2d, norm_mix[0][None, :], w_qkv, g, gains, tables, batch, seq, dil, tm_a)
        o, lse = _dil_attn(qkv, jnp.reshape(bound, (1,)).astype(F32), g)
        outs.append(o)
        lses.append(lse)
    merged = _merge(outs, lses, seq, tm=1024)
    x2d = _ffn(x2d, merged, a_w_o[0].astype(BF16), norm_ffn[0][None, :],
               ffn_w1[0].astype(BF16), ffn_w2[0].astype(BF16), tm=1024)

    cos, sin = _rope_angles(seq, ROPE_DIM)
    cos_b = _spread_rope(jnp.concatenate([cos, cos], -1), -1)
    sin_b = _spread_rope(jnp.concatenate([-sin, sin], -1), -1)

    def head_gain(gain, scale):
        return (jnp.concatenate([gain[:NOPE_DIM], _spread_rope(gain[NOPE_DIM:], 0)]) * scale)[None, :]

    q_scale = LOG2E / math.sqrt(QK_DIM)
    gq = head_gain(b_q_gain[0], q_scale)
    gk = head_gain(b_k_gain[0], 1.0)
    bound = 1.02 * QK_DIM * q_scale * jnp.max(jnp.abs(b_q_gain[0])) * jnp.max(jnp.abs(b_k_gain[0]))
    const_lane = (jnp.arange(LANES) == CONST_LANE).astype(F32)[None, :]

    def half_roll(w, gain):
        return jnp.roll(w * gain, LANES // 2, axis=-1)

    w_in = b_w_in[0]
    k_rope_w = _spread_rope(w_in[:, Q_LORA + KV_LORA:], 1)
    w_in = jnp.concatenate([w_in[:, :Q_LORA + KV_LORA], k_rope_w, half_roll(k_rope_w, gk[:, LANES:])], 1)
    w_qb = b_w_qb[0].reshape(Q_LORA, B_HEADS, QK_DIM)
    q_rope_w = _spread_rope(w_qb[..., NOPE_DIM:], 2)
    w_qb = jnp.concatenate([w_qb[..., :NOPE_DIM], q_rope_w, half_roll(q_rope_w, gq[:, LANES:])], -1)
    w_qb = w_qb.reshape(Q_LORA, B_HEADS * Q3_PAD)
    w_kvb = b_w_kvb[0].reshape(KV_LORA, B_HEADS, NOPE_DIM + V_DIM)
    w_kn = w_kvb[..., :NOPE_DIM].reshape(KV_LORA, B_HEADS * NOPE_DIM)
    w_v = w_kvb[..., NOPE_DIM:].reshape(KV_LORA, B_HEADS * V_DIM).T

    consts = (norm_mix[1][None, :], w_in.astype(BF16), b_q_a_gain[0][None, :], b_kv_a_gain[0][None, :],
              w_qb.astype(BF16), w_kn.astype(BF16), w_v.astype(BF16), gq, gk,
              -bound * const_lane, const_lane)
    tables = (cos_b * gq[:, LANES:], cos_b * gk[:, LANES:], sin_b)
    q, k, v = _b_proj(x2d, consts, tables, seq, tm=1024)
    o = _mla_attn(q, k, v, batch, seq)
    x2d = _ffn(x2d, o, b_w_o[0].astype(BF16), norm_ffn[1][None, :],
               ffn_w1[1].astype(BF16), ffn_w2[1].astype(BF16), tm=1024)
    return x2d.reshape(batch, seq, d_model)
```
